```python
import math
import numpy as np
import jax
import jax.numpy as jnp
from jax import lax

D_MODEL = 1024
BATCH = 4
SEQ = 8192
DEPTH = 2

CHUNK = 64
Q_BLOCK = 128
HEAD_DIM = 128
N_MIX_HEADS = D_MODEL // HEAD_DIM
MIX_WIDTH = N_MIX_HEADS * HEAD_DIM
A_HEADS = N_MIX_HEADS // 2
B_HEADS = N_MIX_HEADS - A_HEADS
C_HEADS = N_MIX_HEADS // 2
D_HEADS = N_MIX_HEADS - C_HEADS
A_HALF = HEAD_DIM // 2
A_W = A_HEADS * HEAD_DIM
B_W = B_HEADS * HEAD_DIM
C_W = C_HEADS * HEAD_DIM
D_W = D_HEADS * HEAD_DIM
CONV_W = 4
ROPE_THETA = 10000.0
RMS_EPS = 1e-6
PLE_DIM = 256
N_GROUPS = 4
EXPERTS_PER_GROUP = 4
N_EXPERTS = N_GROUPS * EXPERTS_PER_GROUP
TOP_K_IN_GROUP = 2
EXPERT_FF = D_MODEL // 2

EVEN_SPLITS = (A_W, A_W, A_W, B_W, B_W, B_W, B_W, B_HEADS, B_HEADS)
ODD_SPLITS = (C_W, C_W, C_W, C_W, C_HEADS, C_HEADS, D_W, D_W, D_W, D_HEADS)
EVEN_IN = sum(EVEN_SPLITS)
ODD_IN = sum(ODD_SPLITS)

kernel_name = 'hybrid_chunk_causal_trunk'


def rmsnorm(x, gain):
    xf = x.astype(jnp.float32)
    y = xf * lax.rsqrt(jnp.mean(xf * xf, axis=-1, keepdims=True) + RMS_EPS)
    return (y * gain.astype(jnp.float32)).astype(x.dtype)


def l2norm(x):
    xf = x.astype(jnp.float32)
    return xf * lax.rsqrt(jnp.sum(xf * xf, axis=-1, keepdims=True) + RMS_EPS)


def _split(z, sizes):
    offs = np.cumsum(sizes)[:-1].tolist()
    return jnp.split(z, offs, axis=-1)


def _heads(t, n, d):
    return t.reshape(t.shape[0], t.shape[1], n, d).transpose(0, 2, 1, 3)


def _merge(t):
    b, h, s, d = t.shape
    return t.transpose(0, 2, 1, 3).reshape(b, s, h * d)


def rope(x):
    d = x.shape[-1]
    inv = ROPE_THETA ** (-jnp.arange(0, d, 2, dtype=jnp.float32) / d)
    ang = jnp.arange(x.shape[2], dtype=jnp.float32)[:, None] * inv[None, :]
    cos, sin = jnp.cos(ang), jnp.sin(ang)
    xf = x.astype(jnp.float32)
    x1, x2 = xf[..., : d // 2], xf[..., d // 2:]
    return jnp.concatenate([x1 * cos - x2 * sin, x2 * cos + x1 * sin], axis=-1).astype(x.dtype)


def causal_conv(x, w):
    return lax.conv_general_dilated(
        x, w[:, None, :].astype(x.dtype), window_strides=(1,), padding=[(CONV_W - 1, 0)],
        dimension_numbers=('NWC', 'WIO', 'NWC'), feature_group_count=x.shape[-1])


def diff_attention(q, k, v, lam, gain, lam_init):
    b, h2, s, d = q.shape
    nh = h2 // 2
    nb = s // Q_BLOCK
    key_chunk = jnp.arange(s) // CHUNK
    q_blocks = jnp.moveaxis(q.reshape(b, h2, nb, Q_BLOCK, d), 2, 0)

    def one_block(args):
        qi, bi = args
        logits = jnp.einsum('bhqd,bhkd->bhqk', qi, k).astype(jnp.float32) * (d ** -0.5)
        q_chunk = (bi * Q_BLOCK + jnp.arange(Q_BLOCK)) // CHUNK
        logits = jnp.where(key_chunk[None, :] <= q_chunk[:, None], logits, -jnp.inf)
        probs = jax.nn.softmax(logits, axis=-1).reshape(b, nh, 2, Q_BLOCK, s)
        weights = probs[:, :, 0] - lam * probs[:, :, 1]
        return jnp.einsum('bhqk,bhkd->bhqd', weights.astype(v.dtype), v)

    out = lax.map(one_block, (q_blocks, jnp.arange(nb)))
    out = jnp.moveaxis(out, 0, 2).reshape(b, nh, s, v.shape[-1])
    return rmsnorm(out, gain) * (1.0 - lam_init)


def mlstm(q, k, v, i_pre, f_pre):
    q, k, v = (t.astype(jnp.float32) for t in (q, k, v))
    b, nh, s, d = q.shape
    nc = s // CHUNK
    k = k * (d ** -0.5)
    log_f = jax.nn.log_sigmoid(f_pre.astype(jnp.float32))
    i_pre = i_pre.astype(jnp.float32)
    to_c = lambda t: jnp.moveaxis(t.reshape(b, nh, nc, CHUNK, *t.shape[3:]), 2, 0)
    qc, kc, vc, ic = to_c(q), to_c(k), to_c(v), to_c(i_pre)
    bc = jnp.cumsum(to_c(log_f), axis=-1)
    causal = jnp.tril(jnp.ones((CHUNK, CHUNK), bool))

    def step(carry, xs):
        c_st, n_st, m_st = carry
        qi, ki, vi, bi, ii = xs
        d_log = jnp.where(causal, bi[..., :, None] - bi[..., None, :] + ii[..., None, :], -jnp.inf)
        inter_log = bi + m_st[..., None]
        m_t = jnp.maximum(inter_log, jnp.max(d_log, axis=-1))
        d_w = jnp.exp(d_log - m_t[..., None])
        inter_w = jnp.exp(inter_log - m_t)
        a = jnp.einsum('bhtd,bhsd->bhts', qi, ki) * d_w
        num = jnp.einsum('bhts,bhsv->bhtv', a, vi) + inter_w[..., None] * jnp.einsum('bhvk,bhtk->bhtv', c_st, qi)
        den = jnp.sum(a, axis=-1) + inter_w * jnp.einsum('bhk,bhtk->bht', n_st, qi)
        h = num / jnp.maximum(jnp.abs(den), jnp.exp(-m_t))[..., None]
        b_last = bi[..., -1]
        w_log = b_last[..., None] - bi + ii
        m_new = jnp.maximum(b_last + m_st, jnp.max(w_log, axis=-1))
        sw = jnp.exp(w_log - m_new[..., None])
        decay = jnp.exp(b_last + m_st - m_new)
        c_new = decay[..., None, None] * c_st + jnp.einsum('bhs,bhsv,bhsk->bhvk', sw, vi, ki)
        n_new = decay[..., None] * n_st + jnp.einsum('bhs,bhsk->bhk', sw, ki)
        return (c_new, n_new, m_new), h

    init = (jnp.zeros((b, nh, d, d), jnp.float32), jnp.zeros((b, nh, d), jnp.float32),
            jnp.zeros((b, nh), jnp.float32))
    _, hs = lax.scan(step, init, (qc, kc, vc, bc, ic))
    return jnp.moveaxis(hs, 0, 2).reshape(b, nh, s, d)


def gated_deltanet(q, k, v, g, beta):
    q, k, v, g, beta = (t.astype(jnp.float32) for t in (q, k, v, g, beta))
    b, nh, s, dk = q.shape
    dv = v.shape[-1]
    nc = s // CHUNK
    q = q * (dk ** -0.5)
    to_c = lambda t: t.reshape(b, nh, nc, CHUNK, *t.shape[3:])
    qc, kc, vc, betac = to_c(q), to_c(k), to_c(v), to_c(beta)
    bc = jnp.cumsum(to_c(g), axis=-1)
    incl = jnp.tril(jnp.ones((CHUNK, CHUNK), bool))
    strict = jnp.tril(jnp.ones((CHUNK, CHUNK), bool), -1)
    gam = jnp.exp(jnp.where(incl, bc[..., :, None] - bc[..., None, :], -jnp.inf))
    kb = kc * betac[..., None]
    a = jnp.where(strict, jnp.einsum('bhcid,bhcjd->bhcij', kb, kc) * gam, 0.0) + jnp.eye(CHUNK, dtype=jnp.float32)
    rhs = jnp.concatenate([vc * betac[..., None], kb * jnp.exp(bc)[..., None]], axis=-1)
    sol = lax.linalg.triangular_solve(a, rhs, left_side=True, lower=True, unit_diagonal=True)
    u, w = sol[..., :dv], sol[..., dv:]
    qk = jnp.einsum('bhcid,bhcjd->bhcij', qc, kc) * gam
    q_dec = qc * jnp.exp(bc)[..., None]
    k_dec = kc * jnp.exp(bc[..., -1:] - bc)[..., None]
    chunk_decay = jnp.exp(bc[..., -1])
    mv = lambda t: jnp.moveaxis(t, 2, 0)

    def step(st, xs):
        ui, wi, qki, qdi, kdi, cdi = xs
        v_new = ui - jnp.einsum('bhlk,bhkv->bhlv', wi, st)
        o = jnp.einsum('bhlk,bhkv->bhlv', qdi, st) + jnp.einsum('bhls,bhsv->bhlv', qki, v_new)
        st = cdi[..., None, None] * st + jnp.einsum('bhsk,bhsv->bhkv', kdi, v_new)
        return st, o

    _, os = lax.scan(step, jnp.zeros((b, nh, dk, dv), jnp.float32),
                     (mv(u), mv(w), mv(qk), mv(q_dec), mv(k_dec), mv(chunk_decay)))
    return jnp.moveaxis(os, 0, 2).reshape(b, nh, s, dv)


def forgetting_attention(q, k, v, log_f):
    b, nh, s, d = q.shape
    nb = s // Q_BLOCK
    cum = jnp.cumsum(log_f, axis=-1)
    key_pos = jnp.arange(s)
    q_blocks = jnp.moveaxis(q.reshape(b, nh, nb, Q_BLOCK, d), 2, 0)
    c_blocks = jnp.moveaxis(cum.reshape(b, nh, nb, Q_BLOCK), 2, 0)

    def one_block(args):
        qi, ci, bi = args
        logits = (jnp.einsum('bhqd,bhkd->bhqk', qi, k).astype(jnp.float32) * (d ** -0.5)
                  + ci[..., :, None] - cum[..., None, :])
        q_pos = bi * Q_BLOCK + jnp.arange(Q_BLOCK)
        logits = jnp.where(key_pos[None, :] <= q_pos[:, None], logits, -jnp.inf)
        probs = jax.nn.softmax(logits, axis=-1)
        return jnp.einsum('bhqk,bhkd->bhqd', probs.astype(v.dtype), v)

    out = lax.map(one_block, (q_blocks, c_blocks, jnp.arange(nb)))
    return jnp.moveaxis(out, 0, 2).reshape(b, nh, s, d)


def even_mixer(hn, w_in, w_out, lam_q1, lam_k1, lam_q2, lam_k2, subln, conv_b, ig_bias, fg_bias, norm_b, layer_idx):
    qa, ka, va, qb, kb, vb, ob, ib, fb = _split(hn @ w_in, EVEN_SPLITS)
    qa = rope(_heads(qa, 2 * A_HEADS, A_HALF))
    ka = rope(_heads(ka, 2 * A_HEADS, A_HALF))
    va = _heads(va, A_HEADS, HEAD_DIM)
    lam_init = 0.8 - 0.6 * math.exp(-0.3 * layer_idx)
    lam = (jnp.exp(jnp.sum(lam_q1 * lam_k1).astype(jnp.float32))
           - jnp.exp(jnp.sum(lam_q2 * lam_k2).astype(jnp.float32)) + lam_init)
    out_a = diff_attention(qa, ka, va, lam, subln, lam_init)
    qk = jax.nn.silu(causal_conv(jnp.concatenate([qb, kb], axis=-1), conv_b))
    qb, kb = jnp.split(qk, 2, axis=-1)
    i_pre = jnp.swapaxes(ib + ig_bias, 1, 2)
    f_pre = jnp.swapaxes(fb + fg_bias, 1, 2)
    hb = mlstm(_heads(qb, B_HEADS, HEAD_DIM), _heads(kb, B_HEADS, HEAD_DIM), _heads(vb, B_HEADS, HEAD_DIM),
               i_pre, f_pre).astype(hn.dtype)
    out_b = rmsnorm(hb, norm_b) * jax.nn.sigmoid(_heads(ob, B_HEADS, HEAD_DIM))
    return jnp.concatenate([_merge(out_a), _merge(out_b)], axis=-1) @ w_out


def odd_mixer(hn, w_in, w_out, conv_c, a_log, dt_bias, norm_c, fd_bias):
    qc, kc, vc, gc, ac, bc, qd, kd, vd, fd = _split(hn @ w_in, ODD_SPLITS)
    qkv = jax.nn.silu(causal_conv(jnp.concatenate([qc, kc, vc], axis=-1), conv_c))
    qc, kc, vc = jnp.split(qkv, 3, axis=-1)
    qc = l2norm(_heads(qc, C_HEADS, HEAD_DIM))
    kc = l2norm(_heads(kc, C_HEADS, HEAD_DIM))
    vc = _heads(vc, C_HEADS, HEAD_DIM)
    g = -jnp.exp(a_log.astype(jnp.float32)) * jax.nn.softplus((ac + dt_bias).astype(jnp.float32))
    beta = jax.nn.sigmoid(bc.astype(jnp.float32))
    oc = gated_deltanet(qc, kc, vc, jnp.swapaxes(g, 1, 2), jnp.swapaxes(beta, 1, 2)).astype(hn.dtype)
    out_c = rmsnorm(oc, norm_c) * jax.nn.silu(_heads(gc, C_HEADS, HEAD_DIM))
    log_f = jax.nn.log_sigmoid(jnp.swapaxes(fd + fd_bias, 1, 2).astype(jnp.float32))
    out_d = forgetting_attention(_heads(qd, D_HEADS, HEAD_DIM), _heads(kd, D_HEADS, HEAD_DIM),
                                 _heads(vd, D_HEADS, HEAD_DIM), log_f)
    return jnp.concatenate([_merge(out_c), _merge(out_d)], axis=-1) @ w_out


def hier_moe(x, w_group, b_group, w_router, b_router, w_gate, w_up, w_down):
    b, s, d = x.shape
    xt = x.reshape(-1, d)
    n = xt.shape[0]
    p_group = jax.nn.softmax((xt @ w_group + b_group).astype(jnp.float32), axis=-1)
    g_sel = jnp.argmax(p_group, axis=-1)
    p_top = jnp.take_along_axis(p_group, g_sel[:, None], axis=-1)
    e_logits = (xt @ w_router + b_router).astype(jnp.float32).reshape(n, N_GROUPS, EXPERTS_PER_GROUP)
    e_sel = jnp.take_along_axis(e_logits, g_sel[:, None, None], axis=1)[:, 0]
    top_v, top_i = lax.top_k(e_sel, TOP_K_IN_GROUP)
    w_k = jax.nn.softmax(top_v, axis=-1) * p_top
    e_idx = g_sel[:, None] * EXPERTS_PER_GROUP + top_i
    combine = jnp.sum(jax.nn.one_hot(e_idx, N_EXPERTS, dtype=jnp.float32) * w_k[..., None], axis=1).astype(x.dtype)
    y = jnp.zeros_like(xt)
    for e in range(N_EXPERTS):
        he = jax.nn.silu(xt @ w_gate[e]) * (xt @ w_up[e])
        y = y + combine[:, e:e + 1] * (he @ w_down[e])
    return y.reshape(b, s, d)


def setup_inputs(seed: int = 0) -> dict:
    key = jax.random.key(seed)
    keys = iter(jax.random.split(key, 48))

    def nrm(shape, scale):
        return jax.random.normal(next(keys), shape, jnp.float32) * scale

    def unif(shape, lo, hi):
        return jax.random.uniform(next(keys), shape, jnp.float32, lo, hi)

    ne = (DEPTH + 1) // 2
    no = DEPTH // 2
    dt = jnp.exp(unif((no, C_HEADS), math.log(1e-3), math.log(1e-1)))
    return {
        'x': nrm((BATCH, SEQ, D_MODEL), 1.0),
        'p': nrm((DEPTH, BATCH, SEQ, PLE_DIM), 1.0),
        'norm_mix': 1.0 + nrm((DEPTH, D_MODEL), 0.02),
        'norm_ffn': 1.0 + nrm((DEPTH, D_MODEL), 0.02),
        'norm_final': 1.0 + nrm((D_MODEL,), 0.02),
        'ab_w_in': nrm((ne, D_MODEL, EVEN_IN), D_MODEL ** -0.5),
        'ab_w_out': nrm((ne, MIX_WIDTH, D_MODEL), MIX_WIDTH ** -0.5),
        'a_lam_q1': nrm((ne, A_HALF), 0.1),
        'a_lam_k1': nrm((ne, A_HALF), 0.1),
        'a_lam_q2': nrm((ne, A_HALF), 0.1),
        'a_lam_k2': nrm((ne, A_HALF), 0.1),
        'a_subln': 1.0 + nrm((ne, HEAD_DIM), 0.02),
        'b_conv': nrm((ne, CONV_W, 2 * B_W), CONV_W ** -0.5),
        'b_igate_bias': nrm((ne, B_HEADS), 0.1),
        'b_fgate_bias': jnp.linspace(3.0, 6.0, B_HEADS)[None, :] + nrm((ne, B_HEADS), 0.1),
        'b_norm': 1.0 + nrm((ne, HEAD_DIM), 0.02),
        'cd_w_in': nrm((no, D_MODEL, ODD_IN), D_MODEL ** -0.5),
        'cd_w_out': nrm((no, MIX_WIDTH, D_MODEL), MIX_WIDTH ** -0.5),
        'c_conv': nrm((no, CONV_W, 3 * C_W), CONV_W ** -0.5),
        'c_a_log': jnp.log(unif((no, C_HEADS), 1.0, 16.0)),
        'c_dt_bias': dt + jnp.log(-jnp.expm1(-dt)),
        'c_norm': 1.0 + nrm((no, HEAD_DIM), 0.02),
        'd_fgate_bias': unif((no, D_HEADS), 2.0, 5.0),
        'moe_w_group': nrm((DEPTH, D_MODEL, N_GROUPS), D_MODEL ** -0.5),
        'moe_b_group': nrm((DEPTH, N_GROUPS), 0.01),
        'moe_w_router': nrm((DEPTH, D_MODEL, N_EXPERTS), D_MODEL ** -0.5),
        'moe_b_router': nrm((DEPTH, N_EXPERTS), 0.01),
        'moe_w_gate': nrm((DEPTH, N_EXPERTS, D_MODEL, EXPERT_FF), D_MODEL ** -0.5),
        'moe_w_up': nrm((DEPTH, N_EXPERTS, D_MODEL, EXPERT_FF), D_MODEL ** -0.5),
        'moe_w_down': nrm((DEPTH, N_EXPERTS, EXPERT_FF, D_MODEL), EXPERT_FF ** -0.5),
        'ple_w_gate': nrm((DEPTH, D_MODEL, D_MODEL), D_MODEL ** -0.5),
        'ple_w_proj': nrm((DEPTH, PLE_DIM, D_MODEL), PLE_DIM ** -0.5),
    }


def reference(x, p, norm_mix, norm_ffn, norm_final,
              ab_w_in, ab_w_out, a_lam_q1, a_lam_k1, a_lam_q2, a_lam_k2, a_subln,
              b_conv, b_igate_bias, b_fgate_bias, b_norm,
              cd_w_in, cd_w_out, c_conv, c_a_log, c_dt_bias, c_norm, d_fgate_bias,
              moe_w_group, moe_b_group, moe_w_router, moe_b_router, moe_w_gate, moe_w_up, moe_w_down,
              ple_w_gate, ple_w_proj):
    h = x
    for i in range(DEPTH):
        j = i // 2
        hn = rmsnorm(h, norm_mix[i])
        if i % 2 == 0:
            mix = even_mixer(hn, ab_w_in[j], ab_w_out[j], a_lam_q1[j], a_lam_k1[j], a_lam_q2[j], a_lam_k2[j],
                             a_subln[j], b_conv[j], b_igate_bias[j], b_fgate_bias[j], b_norm[j], i)
        else:
            mix = odd_mixer(hn, cd_w_in[j], cd_w_out[j], c_conv[j], c_a_log[j], c_dt_bias[j], c_norm[j],
                            d_fgate_bias[j])
        h = h + mix
        h = h + hier_moe(rmsnorm(h, norm_ffn[i]), moe_w_group[i], moe_b_group[i], moe_w_router[i],
                         moe_b_router[i], moe_w_gate[i], moe_w_up[i], moe_w_down[i])
        h = h + jax.nn.sigmoid(h @ ple_w_gate[i]) * (p[i] @ ple_w_proj[i])
    return rmsnorm(h, norm_final)
```

```python
import functools
import math

import jax
import jax.numpy as jnp
from jax import lax
from jax.experimental import pallas as pl
from jax.experimental.pallas import tpu as pltpu

F32 = jnp.float32
BF16 = jnp.bfloat16
HIGHEST = lax.Precision.HIGHEST

D_MODEL = 1024
HEAD_DIM = 128
N_HEADS = 4
MIX_W = N_HEADS * HEAD_DIM
A_HALF = HEAD_DIM // 2
CHUNK = 64
CONV_W = 4
RMS_EPS = 1e-6
PLE_DIM = 256
N_GROUPS = 4
EXPERTS_PER_GROUP = 4
N_EXPERTS = 16
EXPERT_FF = D_MODEL // 2
ROPE_THETA = 10000.0
Z_MAIN = 7 * MIX_W
LANES = 128

VMEM_LIMIT = 56 * 1024 * 1024

NEG_INF = float("-inf")


def _cparams(sem):
    return pltpu.CompilerParams(dimension_semantics=sem, vmem_limit_bytes=VMEM_LIMIT)


def _dot(a, b):
    return jnp.dot(a, b, preferred_element_type=F32)


def _dot_nt(a, b):
    return lax.dot_general(a, b, (((1,), (1,)), ((), ())), preferred_element_type=F32)


def _dot_exact(a, b):
    return jnp.dot(a, b, preferred_element_type=F32, precision=HIGHEST)


def _sigmoid(x):
    return 1.0 / (1.0 + jnp.exp(-x))


def _silu(x):
    return x * _sigmoid(x)


def _log_sigmoid(x):
    return jnp.minimum(x, 0.0) - jnp.log(1.0 + jnp.exp(-jnp.abs(x)))


def _softplus(x):
    return jnp.maximum(x, 0.0) + jnp.log(1.0 + jnp.exp(-jnp.abs(x)))


def _rms(x, gain):
    return x * lax.rsqrt(jnp.mean(x * x, axis=-1, keepdims=True) + RMS_EPS) * gain


IN_TM = 512
IN_TN = 512


def _inproj_kernel(h_ref, g_ref, w_ref, wg_ref, cos_ref, sin_ref, z_ref, zg_ref, *, rope_chunks):
    hn = _rms(h_ref[...], g_ref[...])
    hb = hn.astype(BF16)
    zg_ref[...] = _dot(hb, wg_ref[...])
    if rope_chunks:
        lane = lax.broadcasted_iota(jnp.int32, (IN_TM, IN_TN), 1)
        first_half = (lane & 32) == 0
        cos = jnp.concatenate([cos_ref[...]] * (IN_TN // LANES), axis=1)
        sin = jnp.concatenate([sin_ref[...]] * (IN_TN // LANES), axis=1)
    for c in range(Z_MAIN // IN_TN):
        zc = _dot(hb, w_ref[:, c * IN_TN:(c + 1) * IN_TN])
        if c < rope_chunks:
            partner = jnp.where(first_half, pltpu.roll(zc, IN_TN - 32, axis=1), pltpu.roll(zc, 32, axis=1))
            zc = zc * cos + partner * sin
        z_ref[:, c * IN_TN:(c + 1) * IN_TN] = zc.astype(BF16)


def _inproj(h, gain, w_main, w_gate, cos, sin, seq, rope_chunks):
    t = h.shape[0]
    per_seq = seq // IN_TM
    return pl.pallas_call(
        functools.partial(_inproj_kernel, rope_chunks=rope_chunks),
        grid=(t // IN_TM,),
        in_specs=[
            pl.BlockSpec((IN_TM, D_MODEL), lambda i: (i, 0)),
            pl.BlockSpec((1, D_MODEL), lambda i: (0, 0)),
            pl.BlockSpec((D_MODEL, Z_MAIN), lambda i: (0, 0)),
            pl.BlockSpec((D_MODEL, LANES), lambda i: (0, 0)),
            pl.BlockSpec((IN_TM, LANES), lambda i: (i % per_seq, 0)),
            pl.BlockSpec((IN_TM, LANES), lambda i: (i % per_seq, 0)),
        ],
        out_specs=[
            pl.BlockSpec((IN_TM, Z_MAIN), lambda i: (i, 0)),
            pl.BlockSpec((IN_TM, LANES), lambda i: (i, 0)),
        ],
        out_shape=[jax.ShapeDtypeStruct((t, Z_MAIN), BF16), jax.ShapeDtypeStruct((t, LANES), F32)],
        compiler_params=_cparams(("parallel",)),
        name="inproj",
    )(h, gain, w_main, w_gate, cos, sin)


def _rope_tables(seq):
    inv = ROPE_THETA ** (-jnp.arange(0, A_HALF, 2, dtype=F32) / A_HALF)
    ang = jnp.arange(seq, dtype=F32)[:, None] * inv[None, :]
    cos, sin = jnp.cos(ang), jnp.sin(ang)
    return jnp.tile(cos, (1, 4)), jnp.tile(jnp.concatenate([-sin, sin], axis=1), (1, 2))


ATT_T = 256


def _softmax_step(s, v, m_ref, l_ref, acc_ref, c):
    m_prev = m_ref[c]
    m_new = jnp.maximum(m_prev, jnp.max(s, axis=-1, keepdims=True))
    alpha = jnp.exp(m_prev - m_new)
    p = jnp.exp(s - m_new)
    l_ref[c] = alpha * l_ref[c] + jnp.sum(p, axis=-1, keepdims=True)
    acc_ref[c] = alpha * acc_ref[c] + _dot(p.astype(BF16), v)
    m_ref[c] = m_new


def _attn_a_kernel(q_ref, k_ref, v_ref, lam_ref, gain_ref, o_ref, m_ref, l_ref, acc_ref, *, lam_init):
    i = pl.program_id(2)
    t = ATT_T
    m_ref[...] = jnp.full(m_ref.shape, NEG_INF, F32)
    l_ref[...] = jnp.zeros(l_ref.shape, F32)
    acc_ref[...] = jnp.zeros(acc_ref.shape, F32)
    q = q_ref[...].astype(F32) * (A_HALF ** -0.5)
    lane = lax.broadcasted_iota(jnp.int32, (t, HEAD_DIM), 1)
    qs = (jnp.where(lane < A_HALF, q, 0.0).astype(BF16), jnp.where(lane >= A_HALF, q, 0.0).astype(BF16))

    def tile(j, mask):
        off = pl.multiple_of(j * t, t)
        k = k_ref[pl.ds(off, t), :]
        v = v_ref[pl.ds(off, t), :]
        for c in range(2):
            s = _dot_nt(qs[c], k)
            if mask is not None:
                s = jnp.where(mask, s, NEG_INF)
            _softmax_step(s, v, m_ref, l_ref, acc_ref, c)

    def body(j, carry):
        tile(j, None)
        return carry

    lax.fori_loop(0, i, body, 0)
    row = lax.broadcasted_iota(jnp.int32, (t, t), 0)
    col = lax.broadcasted_iota(jnp.int32, (t, t), 1)
    tile(i, (col >> 6) <= (row >> 6))

    lam_p = lam_ref[...]
    lam = (jnp.exp(jnp.sum(lam_p[0:1] * lam_p[1:2], axis=-1, keepdims=True))
           - jnp.exp(jnp.sum(lam_p[2:3] * lam_p[3:4], axis=-1, keepdims=True)) + lam_init)
    out = acc_ref[0] / l_ref[0] - lam * (acc_ref[1] / l_ref[1])
    o_ref[...] = (_rms(out, gain_ref[...]) * (1.0 - lam_init)).astype(BF16)


def _attn_a(z, lam_p, gain, batch, seq, lam_init):
    t = batch * seq
    nq = seq // ATT_T
    return pl.pallas_call(
        functools.partial(_attn_a_kernel, lam_init=lam_init),
        grid=(batch, N_HEADS, nq),
        in_specs=[
            pl.BlockSpec((ATT_T, HEAD_DIM), lambda b, h, i: (b * nq + i, h)),
            pl.BlockSpec((seq, HEAD_DIM), lambda b, h, i: (b, N_HEADS + h)),
            pl.BlockSpec((seq, HEAD_DIM), lambda b, h, i: (b, 2 * N_HEADS + h)),
            pl.BlockSpec((4, A_HALF), lambda b, h, i: (0, 0)),
            pl.BlockSpec((1, HEAD_DIM), lambda b, h, i: (0, 0)),
        ],
        out_specs=pl.BlockSpec((ATT_T, HEAD_DIM), lambda b, h, i: (b * nq + i, h)),
        out_shape=jax.ShapeDtypeStruct((t, MIX_W), BF16),
        scratch_shapes=[pltpu.VMEM((2, ATT_T, 1), F32), pltpu.VMEM((2, ATT_T, 1), F32),
                        pltpu.VMEM((2, ATT_T, HEAD_DIM), F32)],
        compiler_params=_cparams(("parallel", "parallel", "arbitrary")),
        name="attn_a",
    )(z, z, z, lam_p, gain)


def _attn_d_kernel(q_ref, k_ref, v_ref, ccol_ref, crow_ref, o_ref, m_ref, l_ref, acc_ref):
    i = pl.program_id(2)
    t = ATT_T
    m_ref[...] = jnp.full(m_ref.shape, NEG_INF, F32)
    l_ref[...] = jnp.zeros(l_ref.shape, F32)
    acc_ref[...] = jnp.zeros(acc_ref.shape, F32)
    q = (q_ref[...].astype(F32) * (HEAD_DIM ** -0.5)).astype(BF16)
    ci = ccol_ref[0]

    def tile(j, mask):
        off = pl.multiple_of(j * t, t)
        k = k_ref[pl.ds(off, t), :]
        v = v_ref[pl.ds(off, t), :]
        cj = crow_ref[0, pl.ds(j, 1), :]
        s = _dot_nt(q, k) + (ci - cj)
        if mask is not None:
            s = jnp.where(mask, s, NEG_INF)
        _softmax_step(s, v, m_ref, l_ref, acc_ref, 0)

    def body(j, carry):
        tile(j, None)
        return carry

    lax.fori_loop(0, i, body, 0)
    row = lax.broadcasted_iota(jnp.int32, (t, t), 0)
    col = lax.broadcasted_iota(jnp.int32, (t, t), 1)
    tile(i, col <= row)
    o_ref[...] = (acc_ref[0] / l_ref[0]).astype(BF16)


def _attn_d(z, cum_col, cum_row, batch, seq):
    t = batch * seq
    nq = seq // ATT_T
    base = 4 * N_HEADS
    return pl.pallas_call(
        _attn_d_kernel,
        grid=(batch, N_HEADS, nq),
        in_specs=[
            pl.BlockSpec((ATT_T, HEAD_DIM), lambda b, h, i: (b * nq + i, base + h)),
            pl.BlockSpec((seq, HEAD_DIM), lambda b, h, i: (b, base + N_HEADS + h)),
            pl.BlockSpec((seq, HEAD_DIM), lambda b, h, i: (b, base + 2 * N_HEADS + h)),
            pl.BlockSpec((1, ATT_T, 1), lambda b, h, i: (b * N_HEADS + h, i, 0)),
            pl.BlockSpec((1, nq, ATT_T), lambda b, h, i: (b * N_HEADS + h, 0, 0)),
        ],
        out_specs=pl.BlockSpec((ATT_T, HEAD_DIM), lambda b, h, i: (b * nq + i, h)),
        out_shape=jax.ShapeDtypeStruct((t, MIX_W), BF16),
        scratch_shapes=[pltpu.VMEM((1, ATT_T, 1), F32), pltpu.VMEM((1, ATT_T, 1), F32),
                        pltpu.VMEM((1, ATT_T, HEAD_DIM), F32)],
        compiler_params=_cparams(("parallel", "parallel", "arbitrary")),
        name="attn_d",
    )(z, z, z, cum_col, cum_row)


def _logf_cumsum_kernel(x_ref, b_ref, o_ref, *, rows_per_seq):
    r = x_ref.shape[0]
    lf = _log_sigmoid(x_ref[...] + b_ref[...])
    a = lax.broadcasted_iota(jnp.int32, (LANES, LANES), 0)
    b = lax.broadcasted_iota(jnp.int32, (LANES, LANES), 1)
    within = _dot_exact(lf, (a <= b).astype(F32))
    totals = _dot_exact(lf, jnp.ones((LANES, LANES), F32))
    ra = lax.broadcasted_iota(jnp.int32, (r, r), 0)
    rb = lax.broadcasted_iota(jnp.int32, (r, r), 1)
    earlier = ((rb < ra) & ((rb // rows_per_seq) == (ra // rows_per_seq))).astype(F32)
    o_ref[...] = within + _dot_exact(earlier, totals)


def _logf_cumsum(x, bias_rows, rows_per_seq):
    r = x.shape[0]
    return pl.pallas_call(
        functools.partial(_logf_cumsum_kernel, rows_per_seq=rows_per_seq),
        out_shape=jax.ShapeDtypeStruct((r, LANES), F32),
        compiler_params=pltpu.CompilerParams(vmem_limit_bytes=VMEM_LIMIT),
        name="logf_cumsum",
    )(x, bias_rows)


def _causal_conv_silu(x, prev, w):
    row = lax.broadcasted_iota(jnp.int32, x.shape, 0)
    acc = x * w[CONV_W - 1:CONV_W, :]
    for s in range(1, CONV_W):
        xs = jnp.where(row >= s, pltpu.roll(x, s, axis=0), pltpu.roll(prev, s, axis=0))
        acc = acc + xs * w[CONV_W - 1 - s:CONV_W - s, :]
    return _silu(acc)


def _chunk_masks():
    r = lax.broadcasted_iota(jnp.int32, (CHUNK, CHUNK), 0)
    c = lax.broadcasted_iota(jnp.int32, (CHUNK, CHUNK), 1)
    return r, c


def _mlstm_kernel(q_ref, k_ref, v_ref, og_ref, g_ref, conv_ref, bias_ref, norm_ref, out_ref,
                  pq_ref, pk_ref, c_ref, n_ref, m_ref):
    @pl.when(pl.program_id(1) == 0)
    def _():
        pq_ref[...] = jnp.zeros(pq_ref.shape, F32)
        pk_ref[...] = jnp.zeros(pk_ref.shape, F32)
        c_ref[...] = jnp.zeros(c_ref.shape, F32)
        n_ref[...] = jnp.zeros(n_ref.shape, F32)
        m_ref[...] = jnp.zeros(m_ref.shape, F32)

    xq = q_ref[...].astype(F32)
    xk = k_ref[...].astype(F32)
    q_all = _causal_conv_silu(xq, pq_ref[...], conv_ref[:, 0:MIX_W])
    k_all = _causal_conv_silu(xk, pk_ref[...], conv_ref[:, MIX_W:2 * MIX_W]) * (HEAD_DIM ** -0.5)
    pq_ref[...] = xq
    pk_ref[...] = xk

    lane = lax.broadcasted_iota(jnp.int32, (CHUNK, LANES), 1)
    pre = g_ref[...] + bias_ref[...]
    e = jnp.where(lane < N_HEADS, pre, _log_sigmoid(pre))
    r, c = _chunk_masks()
    cum = _dot_exact((c <= r).astype(F32), e)
    e_t = e.T
    cum_t = cum.T
    causal = c <= r

    for h in range(N_HEADS):
        sl = slice(h * HEAD_DIM, (h + 1) * HEAD_DIM)
        q = q_all[:, sl]
        k = k_all[:, sl]
        v = v_ref[:, sl].astype(F32)
        qb = q.astype(BF16)
        kb = k.astype(BF16)
        i_col = e[:, h:h + 1]
        i_row = e_t[h:h + 1, :]
        b_col = cum[:, N_HEADS + h:N_HEADS + h + 1]
        b_row = cum_t[N_HEADS + h:N_HEADS + h + 1, :]
        b_last = b_col[CHUNK - 1:CHUNK, :]
        m_st = m_ref[h:h + 1, 0:1]
        c_st = c_ref[h]
        n_st = n_ref[h:h + 1, :]

        d_log = jnp.where(causal, b_col - b_row + i_row, NEG_INF)
        inter_log = b_col + m_st
        m_t = jnp.maximum(inter_log, jnp.max(d_log, axis=-1, keepdims=True))
        d_w = jnp.exp(d_log - m_t)
        inter_w = jnp.exp(inter_log - m_t)
        a = _dot_nt(qb, kb) * d_w
        num = _dot(a.astype(BF16), v.astype(BF16)) + inter_w * _dot_nt(qb, c_st.astype(BF16))
        den = jnp.sum(a, axis=-1, keepdims=True) + inter_w * jnp.sum(q * n_st, axis=-1, keepdims=True)
        hh = num / jnp.maximum(jnp.abs(den), jnp.exp(-m_t))

        w_log = b_last - b_col + i_col
        m_new = jnp.maximum(b_last + m_st, jnp.max(w_log, axis=0, keepdims=True))
        sw = jnp.exp(w_log - m_new)
        decay = jnp.exp(b_last + m_st - m_new)
        c_ref[h] = decay * c_st + _dot((v * sw).T.astype(BF16), kb)
        n_ref[h:h + 1, :] = decay * n_st + jnp.sum(k * sw, axis=0, keepdims=True)
        m_ref[h:h + 1, :] = jnp.broadcast_to(m_new, (1, LANES))

        gate = _sigmoid(og_ref[:, sl].astype(F32))
        out_ref[:, sl] = (_rms(hh, norm_ref[...]) * gate).astype(BF16)


def _mlstm(z, zg, conv_w, gate_bias, norm, batch, seq):
    t = batch * seq
    nc = seq // CHUNK
    col = lambda j: pl.BlockSpec((CHUNK, MIX_W), lambda b, c: (b * nc + c, j))
    return pl.pallas_call(
        _mlstm_kernel,
        grid=(batch, nc),
        in_specs=[
            col(3), col(4), col(5), col(6),
            pl.BlockSpec((CHUNK, LANES), lambda b, c: (b * nc + c, 0)),
            pl.BlockSpec((CONV_W, 2 * MIX_W), lambda b, c: (0, 0)),
            pl.BlockSpec((1, LANES), lambda b, c: (0, 0)),
            pl.BlockSpec((1, HEAD_DIM), lambda b, c: (0, 0)),
        ],
        out_specs=pl.BlockSpec((CHUNK, MIX_W), lambda b, c: (b * nc + c, 0)),
        out_shape=jax.ShapeDtypeStruct((t, MIX_W), BF16),
        scratch_shapes=[pltpu.VMEM((CHUNK, MIX_W), F32), pltpu.VMEM((CHUNK, MIX_W), F32),
                        pltpu.VMEM((N_HEADS, HEAD_DIM, HEAD_DIM), F32), pltpu.VMEM((8, HEAD_DIM), F32),
                        pltpu.VMEM((8, LANES), F32)],
        compiler_params=_cparams(("parallel", "arbitrary")),
        name="mlstm",
    )(z, z, z, z, zg, conv_w, gate_bias, norm)


def _unit_lower_inverse(l_strict, r, c):
    same16 = (r >> 4) == (c >> 4)
    same32 = (r >> 5) == (c >> 5)
    eye = (r == c).astype(F32)
    n = jnp.where(same16, -l_strict, 0.0)
    inv = eye + n
    p = n
    for _ in range(3):
        p = _dot(p, p)
        inv = inv + _dot(inv, p)
    off32 = jnp.where(same32 & jnp.logical_not(same16), l_strict, 0.0)
    inv = inv - _dot(inv, _dot(off32, inv))
    off64 = jnp.where(same32, 0.0, l_strict)
    inv = inv - _dot(inv, _dot(off64, inv))
    return inv


def _deltanet_kernel(q_ref, k_ref, v_ref, og_ref, g_ref, conv_ref, par_ref, norm_ref, out_ref,
                     pq_ref, pk_ref, pv_ref, s_ref):
    @pl.when(pl.program_id(1) == 0)
    def _():
        pq_ref[...] = jnp.zeros(pq_ref.shape, F32)
        pk_ref[...] = jnp.zeros(pk_ref.shape, F32)
        pv_ref[...] = jnp.zeros(pv_ref.shape, F32)
        s_ref[...] = jnp.zeros(s_ref.shape, F32)

    xq = q_ref[...].astype(F32)
    xk = k_ref[...].astype(F32)
    xv = v_ref[...].astype(F32)
    q_all = _causal_conv_silu(xq, pq_ref[...], conv_ref[:, 0:MIX_W])
    k_all = _causal_conv_silu(xk, pk_ref[...], conv_ref[:, MIX_W:2 * MIX_W])
    v_all = _causal_conv_silu(xv, pv_ref[...], conv_ref[:, 2 * MIX_W:3 * MIX_W])
    pq_ref[...] = xq
    pk_ref[...] = xk
    pv_ref[...] = xv

    lane = lax.broadcasted_iota(jnp.int32, (CHUNK, LANES), 1)
    pre = g_ref[...]
    e = jnp.where(lane < N_HEADS, -jnp.exp(par_ref[0:1, :]) * _softplus(pre + par_ref[1:2, :]), _sigmoid(pre))
    r, c = _chunk_masks()
    cum = _dot_exact((c <= r).astype(F32), e)
    cum_t = cum.T

    for h in range(N_HEADS):
        sl = slice(h * HEAD_DIM, (h + 1) * HEAD_DIM)
        q = q_all[:, sl]
        k = k_all[:, sl]
        v = v_all[:, sl]
        q = q * lax.rsqrt(jnp.sum(q * q, axis=-1, keepdims=True) + RMS_EPS) * (HEAD_DIM ** -0.5)
        k = k * lax.rsqrt(jnp.sum(k * k, axis=-1, keepdims=True) + RMS_EPS)
        beta = e[:, N_HEADS + h:N_HEADS + h + 1]
        b_col = cum[:, h:h + 1]
        b_row = cum_t[h:h + 1, :]
        b_last = b_col[CHUNK - 1:CHUNK, :]
        gam = jnp.exp(jnp.where(c <= r, b_col - b_row, NEG_INF))
        kbeta = k * beta
        kb16 = k.astype(BF16)
        l_strict = jnp.where(c < r, _dot_nt(kbeta.astype(BF16), kb16) * gam, 0.0)
        inv = _unit_lower_inverse(l_strict, r, c)
        e_col = jnp.exp(b_col)
        u = _dot(inv, v * beta)
        w = _dot(inv, kbeta * e_col)
        qk = _dot_nt(q.astype(BF16), kb16) * gam
        q_dec = q * e_col
        k_dec = k * jnp.exp(b_last - b_col)
        st = s_ref[h]
        v_new = u - _dot(w, st)
        o = _dot(q_dec, st) + _dot(qk, v_new)
        s_ref[h] = jnp.exp(b_last) * st + _dot(k_dec.T, v_new)
        gate = _silu(og_ref[:, sl].astype(F32))
        out_ref[:, sl] = (_rms(o, norm_ref[...]) * gate).astype(BF16)


def _deltanet(z, zg, conv_w, params, norm, batch, seq):
    t = batch * seq
    nc = seq // CHUNK
    col = lambda j: pl.BlockSpec((CHUNK, MIX_W), lambda b, c: (b * nc + c, j))
    return pl.pallas_call(
        _deltanet_kernel,
        grid=(batch, nc),
        in_specs=[
            col(0), col(1), col(2), col(3),
            pl.BlockSpec((CHUNK, LANES), lambda b, c: (b * nc + c, 0)),
            pl.BlockSpec((CONV_W, 3 * MIX_W), lambda b, c: (0, 0)),
            pl.BlockSpec((2, LANES), lambda b, c: (0, 0)),
            pl.BlockSpec((1, HEAD_DIM), lambda b, c: (0, 0)),
        ],
        out_specs=pl.BlockSpec((CHUNK, MIX_W), lambda b, c: (b * nc + c, 0)),
        out_shape=jax.ShapeDtypeStruct((t, MIX_W), BF16),
        scratch_shapes=[pltpu.VMEM((CHUNK, MIX_W), F32)] * 3
        + [pltpu.VMEM((N_HEADS, HEAD_DIM, HEAD_DIM), F32)],
        compiler_params=_cparams(("parallel", "arbitrary")),
        name="deltanet",
    )(z, z, z, z, zg, conv_w, params, norm)


OUT_TM = 512
G_LANE0 = N_EXPERTS


def _outproj_router_kernel(h_ref, a_ref, b_ref, wo_ref, gn_ref, whi_ref, wlo_ref, br_ref,
                           hout_ref, xn_ref, comb_ref):
    h_new = h_ref[...] + _dot(a_ref[...], wo_ref[0:MIX_W, :]) + _dot(b_ref[...], wo_ref[MIX_W:2 * MIX_W, :])
    hout_ref[...] = h_new
    hn = _rms(h_new, gn_ref[...])
    hi = hn.astype(BF16)
    xn_ref[...] = hi
    lo = (hn - hi.astype(F32)).astype(BF16)
    logits = _dot(hi, whi_ref[...]) + _dot(lo, whi_ref[...]) + _dot(hi, wlo_ref[...]) + br_ref[...]

    lane_i = lax.broadcasted_iota(jnp.int32, logits.shape, 1)
    lane = lane_i.astype(F32)
    big = 1e9
    is_group = (lane_i >= G_LANE0) & (lane_i < G_LANE0 + N_GROUPS)
    gl = jnp.where(is_group, logits, NEG_INF)
    gmax = jnp.max(gl, axis=-1, keepdims=True)
    g_sel = jnp.min(jnp.where(gl == gmax, lane, big), axis=-1, keepdims=True) - G_LANE0
    p_top = 1.0 / jnp.sum(jnp.exp(gl - gmax), axis=-1, keepdims=True)
    group_of_lane = (lane_i >> 2).astype(F32)
    in_group = (lane_i < N_EXPERTS) & (group_of_lane == g_sel)
    el = jnp.where(in_group, logits, NEG_INF)
    v1 = jnp.max(el, axis=-1, keepdims=True)
    i1 = jnp.min(jnp.where(el == v1, lane, big), axis=-1, keepdims=True)
    el2 = jnp.where(lane == i1, NEG_INF, el)
    v2 = jnp.max(el2, axis=-1, keepdims=True)
    i2 = jnp.min(jnp.where(el2 == v2, lane, big), axis=-1, keepdims=True)
    e21 = jnp.exp(v2 - v1)
    w1 = p_top / (1.0 + e21)
    comb_ref[...] = jnp.where(lane == i1, w1, 0.0) + jnp.where(lane == i2, w1 * e21, 0.0)


def _outproj_router(h, oa, ob, w_out, gain, wr_hi, wr_lo, b_r):
    t = h.shape[0]
    row = lambda w: pl.BlockSpec((OUT_TM, w), lambda i: (i, 0))
    full = lambda s: pl.BlockSpec(s, lambda i: (0, 0))
    return pl.pallas_call(
        _outproj_router_kernel,
        grid=(t // OUT_TM,),
        in_specs=[row(D_MODEL), row(MIX_W), row(MIX_W), full((D_MODEL, D_MODEL)), full((1, D_MODEL)),
                  full((D_MODEL, LANES)), full((D_MODEL, LANES)), full((1, LANES))],
        out_specs=[row(D_MODEL), row(D_MODEL), row(LANES)],
        out_shape=[jax.ShapeDtypeStruct((t, D_MODEL), F32), jax.ShapeDtypeStruct((t, D_MODEL), BF16),
                   jax.ShapeDtypeStruct((t, LANES), F32)],
        compiler_params=_cparams(("parallel",)),
        name="outproj_router",
    )(h, oa, ob, w_out, gain, wr_hi, wr_lo, b_r)


MOE_TM = 1024


def _moe_kernel(x_ref, comb_ref, h_ref, wg_ref, wu_ref, wd_ref, out_ref, acc_ref):
    e = pl.program_id(1)

    @pl.when(e == 0)
    def _():
        acc_ref[...] = jnp.zeros(acc_ref.shape, F32)

    x = x_ref[...]
    he = _silu(_dot(x, wg_ref[0])) * _dot(x, wu_ref[0])
    lane = lax.broadcasted_iota(jnp.int32, comb_ref.shape, 1)
    ce = jnp.sum(jnp.where(lane == e, comb_ref[...], 0.0), axis=-1, keepdims=True)
    acc_ref[...] += _dot((he * ce).astype(BF16), wd_ref[0])

    @pl.when(e == N_EXPERTS - 1)
    def _():
        out_ref[...] = h_ref[...] + acc_ref[...]


def _moe(xn, comb, h, wg, wu, wd):
    t = h.shape[0]
    return pl.pallas_call(
        _moe_kernel,
        grid=(t // MOE_TM, N_EXPERTS),
        in_specs=[
            pl.BlockSpec((MOE_TM, D_MODEL), lambda i, e: (i, 0)),
            pl.BlockSpec((MOE_TM, LANES), lambda i, e: (i, 0)),
            pl.BlockSpec((MOE_TM, D_MODEL), lambda i, e: (i, 0)),
            pl.BlockSpec((1, D_MODEL, EXPERT_FF), lambda i, e: (e, 0, 0)),
            pl.BlockSpec((1, D_MODEL, EXPERT_FF), lambda i, e: (e, 0, 0)),
            pl.BlockSpec((1, EXPERT_FF, D_MODEL), lambda i, e: (e, 0, 0)),
        ],
        out_specs=pl.BlockSpec((MOE_TM, D_MODEL), lambda i, e: (i, 0)),
        out_shape=jax.ShapeDtypeStruct((t, D_MODEL), F32),
        scratch_shapes=[pltpu.VMEM((MOE_TM, D_MODEL), F32)],
        compiler_params=_cparams(("parallel", "arbitrary")),
        name="moe",
    )(xn, comb, h, wg, wu, wd)


PLE_TM = 512


def _ple_kernel(h_ref, p_ref, wg_ref, wp_ref, gn_ref, out_ref, *, final_norm):
    h = h_ref[...]
    gate = _sigmoid(_dot(h.astype(BF16), wg_ref[...]))
    out = h + gate * _dot(p_ref[...].astype(BF16), wp_ref[...])
    if final_norm:
        out = _rms(out, gn_ref[...])
    out_ref[...] = out


def _ple(h, p, w_gate, w_proj, gain, final_norm):
    t = h.shape[0]
    return pl.pallas_call(
        functools.partial(_ple_kernel, final_norm=final_norm),
        grid=(t // PLE_TM,),
        in_specs=[
            pl.BlockSpec((PLE_TM, D_MODEL), lambda i: (i, 0)),
            pl.BlockSpec((PLE_TM, PLE_DIM), lambda i: (i, 0)),
            pl.BlockSpec((D_MODEL, D_MODEL), lambda i: (0, 0)),
            pl.BlockSpec((PLE_DIM, D_MODEL), lambda i: (0, 0)),
            pl.BlockSpec((1, D_MODEL), lambda i: (0, 0)),
        ],
        out_specs=pl.BlockSpec((PLE_TM, D_MODEL), lambda i: (i, 0)),
        out_shape=jax.ShapeDtypeStruct((t, D_MODEL), F32),
        compiler_params=_cparams(("parallel",)),
        name="ple",
    )(h, p, w_gate, w_proj, gain)


def _pad_lanes(x):
    return jnp.pad(x, ((0, 0), (0, LANES - x.shape[-1])))


def _row(x):
    return x.reshape(1, -1).astype(F32)


def kernel(x, p, norm_mix, norm_ffn, norm_final, ab_w_in, ab_w_out, a_lam_q1, a_lam_k1, a_lam_q2, a_lam_k2,
           a_subln, b_conv, b_igate_bias, b_fgate_bias, b_norm, cd_w_in, cd_w_out, c_conv, c_a_log, c_dt_bias,
           c_norm, d_fgate_bias, moe_w_group, moe_b_group, moe_w_router, moe_b_router, moe_w_gate, moe_w_up,
           moe_w_down, ple_w_gate, ple_w_proj):
    batch, seq, _ = x.shape
    depth = p.shape[0]
    t = batch * seq
    assert seq % IN_TM == 0 and seq % ATT_T == 0 and t % MOE_TM == 0
    h = x.reshape(t, D_MODEL)
    cos, sin = _rope_tables(seq)
    ones_tab = jnp.ones((seq, LANES), F32)

    for i in range(depth):
        j = i // 2
        if i % 2 == 0:
            w_in = ab_w_in[j]
            z, zg = _inproj(h, _row(norm_mix[i]), w_in[:, :Z_MAIN].astype(BF16),
                            _pad_lanes(w_in[:, Z_MAIN:]).astype(BF16), cos, sin, seq, rope_chunks=2)
            lam_init = 0.8 - 0.6 * math.exp(-0.3 * i)
            lam_p = jnp.stack([a_lam_q1[j], a_lam_k1[j], a_lam_q2[j], a_lam_k2[j]]).astype(F32)
            out_1 = _attn_a(z, lam_p, _row(a_subln[j]), batch, seq, lam_init)
            gate_bias = _pad_lanes(jnp.concatenate([b_igate_bias[j], b_fgate_bias[j]]).reshape(1, -1))
            out_2 = _mlstm(z, zg, b_conv[j], gate_bias, _row(b_norm[j]), batch, seq)
            w_out = ab_w_out[j]
        else:
            w_in = cd_w_in[j]
            c_main, d_main = 4 * MIX_W, 3 * MIX_W
            w_main = jnp.concatenate([w_in[:, :c_main], w_in[:, c_main + 8:c_main + 8 + d_main]], axis=1)
            w_small = jnp.concatenate([w_in[:, c_main:c_main + 8], w_in[:, c_main + 8 + d_main:]], axis=1)
            z, zg = _inproj(h, _row(norm_mix[i]), w_main.astype(BF16), _pad_lanes(w_small).astype(BF16),
                            ones_tab, ones_tab, seq, rope_chunks=0)
            params = jnp.stack([_pad_lanes(c_a_log[j].reshape(1, -1))[0], _pad_lanes(c_dt_bias[j].reshape(1, -1))[0]])
            out_1 = _deltanet(z, zg, c_conv[j], params, _row(c_norm[j]), batch, seq)
            fd = zg[:, 8:8 + N_HEADS].reshape(batch, seq, N_HEADS).transpose(0, 2, 1)
            bias_rows = jnp.repeat(jnp.tile(d_fgate_bias[j], batch), seq // LANES).reshape(-1, 1)
            cum = _logf_cumsum(fd.reshape(-1, LANES), jnp.broadcast_to(bias_rows, (bias_rows.shape[0], LANES)),
                               seq // LANES)
            cum = cum.reshape(batch * N_HEADS, seq)
            out_2 = _attn_d(z, cum.reshape(batch * N_HEADS, seq, 1),
                            cum.reshape(batch * N_HEADS, seq // ATT_T, ATT_T), batch, seq)
            w_out = cd_w_out[j]

        w_r = _pad_lanes(jnp.concatenate([moe_w_router[i], moe_w_group[i]], axis=1))
        w_r_hi = w_r.astype(BF16)
        w_r_lo = (w_r - w_r_hi.astype(F32)).astype(BF16)
        b_r = _pad_lanes(jnp.concatenate([moe_b_router[i], moe_b_group[i]]).reshape(1, -1))
        h, xn, comb = _outproj_router(h, out_1, out_2, w_out.astype(BF16), _row(norm_ffn[i]), w_r_hi, w_r_lo, b_r)
        h = _moe(xn, comb, h, moe_w_gate[i].astype(BF16), moe_w_up[i].astype(BF16), moe_w_down[i].astype(BF16))
        h = _ple(h, p[i].reshape(t, PLE_DIM), ple_w_gate[i].astype(BF16), ple_w_proj[i].astype(BF16),
                 _row(norm_final), final_norm=(i == depth - 1))
    return h.reshape(batch, seq, D_MODEL)
```

```python
import functools
import math

import jax
import jax.numpy as jnp
from jax import lax
from jax.experimental import pallas as pl
from jax.experimental.pallas import tpu as pltpu

F32 = jnp.float32
BF16 = jnp.bfloat16
HIGHEST = lax.Precision.HIGHEST

D_MODEL = 1024
HEAD_DIM = 128
N_HEADS = 4
MIX_W = N_HEADS * HEAD_DIM
A_HALF = HEAD_DIM // 2
CHUNK = 64
CONV_W = 4
RMS_EPS = 1e-6
PLE_DIM = 256
N_GROUPS = 4
EXPERTS_PER_GROUP = 4
N_EXPERTS = 16
EXPERT_FF = D_MODEL // 2
ROPE_THETA = 10000.0
Z_MAIN = 7 * MIX_W
LANES = 128

VMEM_LIMIT = 56 * 1024 * 1024

NEG_INF = float("-inf")


def _cparams(sem):
    return pltpu.CompilerParams(dimension_semantics=sem, vmem_limit_bytes=VMEM_LIMIT)


def _dot(a, b):
    return jnp.dot(a, b, preferred_element_type=F32)


def _dot_nt(a, b):
    return lax.dot_general(a, b, (((1,), (1,)), ((), ())), preferred_element_type=F32)


def _dot_exact(a, b):
    return jnp.dot(a, b, preferred_element_type=F32, precision=HIGHEST)


def _bf16_part(x):
    bits = lax.bitcast_convert_type(x, jnp.int32) & jnp.int32(-65536)
    return lax.bitcast_convert_type(bits, F32)


def _sigmoid(x):
    return 1.0 / (1.0 + jnp.exp(-x))


def _silu(x):
    return x * _sigmoid(x)


def _log_sigmoid(x):
    return jnp.minimum(x, 0.0) - jnp.log(1.0 + jnp.exp(-jnp.abs(x)))


def _softplus(x):
    return jnp.maximum(x, 0.0) + jnp.log(1.0 + jnp.exp(-jnp.abs(x)))


def _rms(x, gain):
    return x * lax.rsqrt(jnp.mean(x * x, axis=-1, keepdims=True) + RMS_EPS) * gain


IN_TM = 512
IN_TN = 512


def _inproj_kernel(h_ref, g_ref, w_ref, wg_ref, cos_ref, sin_ref, z_ref, zg_ref, *, rope_chunks):
    hn = _rms(h_ref[...], g_ref[...])
    hb = hn.astype(BF16)
    zg_ref[...] = _dot(hb, wg_ref[...])
    if rope_chunks:
        lane = lax.broadcasted_iota(jnp.int32, (IN_TM, IN_TN), 1)
        first_half = (lane & 32) == 0
        cos = jnp.concatenate([cos_ref[...]] * (IN_TN // LANES), axis=1)
        sin = jnp.concatenate([sin_ref[...]] * (IN_TN // LANES), axis=1)
    for c in range(Z_MAIN // IN_TN):
        zc = _dot(hb, w_ref[:, c * IN_TN:(c + 1) * IN_TN])
        if c < rope_chunks:
            partner = jnp.where(first_half, pltpu.roll(zc, IN_TN - 32, axis=1), pltpu.roll(zc, 32, axis=1))
            zc = zc * cos + partner * sin
        z_ref[:, c * IN_TN:(c + 1) * IN_TN] = zc.astype(BF16)


def _inproj(h, gain, w_main, w_gate, cos, sin, seq, rope_chunks):
    t = h.shape[0]
    per_seq = seq // IN_TM
    return pl.pallas_call(
        functools.partial(_inproj_kernel, rope_chunks=rope_chunks),
        grid=(t // IN_TM,),
        in_specs=[
            pl.BlockSpec((IN_TM, D_MODEL), lambda i: (i, 0)),
            pl.BlockSpec((1, D_MODEL), lambda i: (0, 0)),
            pl.BlockSpec((D_MODEL, Z_MAIN), lambda i: (0, 0)),
            pl.BlockSpec((D_MODEL, LANES), lambda i: (0, 0)),
            pl.BlockSpec((IN_TM, LANES), lambda i: (i % per_seq, 0)),
            pl.BlockSpec((IN_TM, LANES), lambda i: (i % per_seq, 0)),
        ],
        out_specs=[
            pl.BlockSpec((IN_TM, Z_MAIN), lambda i: (i, 0)),
            pl.BlockSpec((IN_TM, LANES), lambda i: (i, 0)),
        ],
        out_shape=[jax.ShapeDtypeStruct((t, Z_MAIN), BF16), jax.ShapeDtypeStruct((t, LANES), F32)],
        compiler_params=_cparams(("parallel",)),
        name="inproj",
    )(h, gain, w_main, w_gate, cos, sin)


def _rope_tables(seq):
    inv = ROPE_THETA ** (-jnp.arange(0, A_HALF, 2, dtype=F32) / A_HALF)
    ang = jnp.arange(seq, dtype=F32)[:, None] * inv[None, :]
    cos, sin = jnp.cos(ang), jnp.sin(ang)
    return jnp.tile(cos, (1, 4)), jnp.tile(jnp.concatenate([-sin, sin], axis=1), (1, 2))


ATT_W = 512
ATT_TK = 512
ATT_TQ_A = ATT_W // 2
LOG2E = 1.4426950408889634


def _attn_init(v_ref, vt_ref, m_ref, l_ref, acc_ref):
    @pl.when(pl.program_id(2) == 0)
    def _():
        def body(j, carry):
            off = pl.multiple_of(j * ATT_TK, ATT_TK)
            vt_ref[j] = v_ref[pl.ds(off, ATT_TK), :].astype(F32).T.astype(BF16)
            return carry
        lax.fori_loop(0, vt_ref.shape[0], body, 0)

    m_ref[...] = jnp.full(m_ref.shape, NEG_INF, F32)
    l_ref[...] = jnp.zeros(l_ref.shape, F32)
    acc_ref[...] = jnp.zeros(acc_ref.shape, F32)


def _attn_logits(k_ref, q_t, s_ref, blk, mask):
    off = pl.multiple_of(blk * ATT_TK, ATT_TK)
    s = _dot(k_ref[pl.ds(off, ATT_TK), :], q_t)
    if mask is not None:
        s = jnp.where(mask, s, NEG_INF)
    s_ref[...] = s


def _attn_consume(s_ref, vt_ref, blk, m_ref, l_ref, acc_ref):
    s = s_ref[...]
    m_prev = m_ref[...]
    m_new = jnp.maximum(m_prev, jnp.max(s, axis=0, keepdims=True))
    alpha = jnp.exp2(m_prev - m_new)
    p = jnp.exp2(s - m_new)
    l_ref[...] = alpha * l_ref[...] + jnp.sum(p, axis=0, keepdims=True)
    acc_ref[...] = alpha * acc_ref[...] + _dot(vt_ref[blk], p.astype(BF16))
    m_ref[...] = m_new


def _attn_pipeline(k_ref, vt_ref, q_t, s0_ref, s1_ref, m_ref, l_ref, acc_ref, n_full, diag_blk, diag_mask):
    def consume(s_ref, blk):
        _attn_consume(s_ref, vt_ref, blk, m_ref, l_ref, acc_ref)

    _attn_logits(k_ref, q_t, s0_ref, diag_blk, diag_mask)

    def body(t, carry):
        _attn_logits(k_ref, q_t, s1_ref, 2 * t, None)
        consume(s0_ref, jnp.where(t == 0, diag_blk, 2 * t - 1))
        _attn_logits(k_ref, q_t, s0_ref, 2 * t + 1, None)
        consume(s1_ref, 2 * t)
        return carry

    npairs = n_full // 2
    lax.fori_loop(0, npairs, body, 0)
    in_s0 = jnp.where(npairs == 0, diag_blk, 2 * npairs - 1)

    @pl.when(n_full % 2 == 1)
    def _():
        _attn_logits(k_ref, q_t, s1_ref, n_full - 1, None)
        consume(s0_ref, in_s0)
        consume(s1_ref, n_full - 1)

    @pl.when(n_full % 2 == 0)
    def _():
        consume(s0_ref, in_s0)


def _attn_a_kernel(q_ref, k_ref, v_ref, lam_ref, gain_ref, o_ref, vt_ref, s0_ref, s1_ref, m_ref, l_ref, acc_ref,
                   *, lam_init):
    _attn_init(v_ref, vt_ref, m_ref, l_ref, acc_ref)
    i = pl.program_id(2)
    tq = ATT_TQ_A
    q = (q_ref[...].astype(F32) * (A_HALF ** -0.5 * LOG2E)).T
    dim = lax.broadcasted_iota(jnp.int32, q.shape, 0)
    q_t = jnp.concatenate([jnp.where(dim < A_HALF, q, 0.0), jnp.where(dim >= A_HALF, q, 0.0)], axis=1).astype(BF16)
    key = lax.broadcasted_iota(jnp.int32, (ATT_TK, ATT_W), 0)
    qry = lax.broadcasted_iota(jnp.int32, (ATT_TK, ATT_W), 1) & (tq - 1)
    diag_mask = ((key >> 6) - (qry >> 6)) <= 4 * (i % 2)
    _attn_pipeline(k_ref, vt_ref, q_t, s0_ref, s1_ref, m_ref, l_ref, acc_ref, i // 2, i // 2, diag_mask)

    lam_p = lam_ref[...]
    lam = (jnp.exp(jnp.sum(lam_p[0:1] * lam_p[1:2], axis=-1, keepdims=True))
           - jnp.exp(jnp.sum(lam_p[2:3] * lam_p[3:4], axis=-1, keepdims=True)) + lam_init)
    o_all = acc_ref[...] * (1.0 / l_ref[...])
    out = o_all[:, 0:tq] - lam * o_all[:, tq:2 * tq]
    out = out * lax.rsqrt(jnp.mean(out * out, axis=0, keepdims=True) + RMS_EPS) * (gain_ref[...] * (1.0 - lam_init))
    o_ref[...] = out.T.astype(BF16)


def _attn_scratch(seq):
    return [pltpu.VMEM((seq // ATT_TK, HEAD_DIM, ATT_TK), BF16), pltpu.VMEM((ATT_TK, ATT_W), F32),
            pltpu.VMEM((ATT_TK, ATT_W), F32), pltpu.VMEM((1, ATT_W), F32), pltpu.VMEM((1, ATT_W), F32),
            pltpu.VMEM((HEAD_DIM, ATT_W), F32)]


def _attn_a(z, lam_p, gain_col, batch, seq, lam_init):
    t = batch * seq
    nq = seq // ATT_TQ_A
    return pl.pallas_call(
        functools.partial(_attn_a_kernel, lam_init=lam_init),
        grid=(batch, N_HEADS, nq),
        in_specs=[
            pl.BlockSpec((ATT_TQ_A, HEAD_DIM), lambda b, h, i: (b * nq + i, h)),
            pl.BlockSpec((seq, HEAD_DIM), lambda b, h, i: (b, N_HEADS + h)),
            pl.BlockSpec((seq, HEAD_DIM), lambda b, h, i: (b, 2 * N_HEADS + h)),
            pl.BlockSpec((4, A_HALF), lambda b, h, i: (0, 0)),
            pl.BlockSpec((HEAD_DIM, 1), lambda b, h, i: (0, 0)),
        ],
        out_specs=pl.BlockSpec((ATT_TQ_A, HEAD_DIM), lambda b, h, i: (b * nq + i, h)),
        out_shape=jax.ShapeDtypeStruct((t, MIX_W), BF16),
        scratch_shapes=_attn_scratch(seq),
        compiler_params=_cparams(("parallel", "parallel", "arbitrary")),
        name="attn_a",
    )(z, z, z, lam_p, gain_col)


D_AUG = 2 * HEAD_DIM
BIAS_ROWS = 16


def _attn_d_kernel(q_ref, k_ref, v_ref, kx_ref, qx_ref, o_ref, kaug_ref, vt_ref, s0_ref, s1_ref, m_ref, l_ref,
                   acc_ref):
    _attn_init(v_ref, vt_ref, m_ref, l_ref, acc_ref)
    i = pl.program_id(2)

    @pl.when(i == 0)
    def _():
        def body(j, carry):
            off = pl.multiple_of(j * ATT_TK, ATT_TK)
            kaug_ref[pl.ds(off, ATT_TK), 0:HEAD_DIM] = k_ref[pl.ds(off, ATT_TK), :]
            kaug_ref[pl.ds(off, ATT_TK), HEAD_DIM:D_AUG] = kx_ref[0, pl.ds(off, ATT_TK), :]
            return carry
        lax.fori_loop(0, vt_ref.shape[0], body, 0)

    q = (q_ref[...].astype(F32) * (HEAD_DIM ** -0.5 * LOG2E)).T.astype(BF16)
    q_t = jnp.concatenate([q, qx_ref[0], jnp.zeros((D_AUG - HEAD_DIM - BIAS_ROWS, ATT_W), BF16)], axis=0)
    key = lax.broadcasted_iota(jnp.int32, (ATT_TK, ATT_W), 0)
    qry = lax.broadcasted_iota(jnp.int32, (ATT_TK, ATT_W), 1)
    _attn_pipeline(kaug_ref, vt_ref, q_t, s0_ref, s1_ref, m_ref, l_ref, acc_ref, i, i, key <= qry)
    o_ref[...] = (acc_ref[...] * (1.0 / l_ref[...])).T.astype(BF16)


def _attn_d(z, kx, qx, batch, seq):
    t = batch * seq
    nq = seq // ATT_W
    base = 4 * N_HEADS
    return pl.pallas_call(
        _attn_d_kernel,
        grid=(batch, N_HEADS, nq),
        in_specs=[
            pl.BlockSpec((ATT_W, HEAD_DIM), lambda b, h, i: (b * nq + i, base + h)),
            pl.BlockSpec((seq, HEAD_DIM), lambda b, h, i: (b, base + N_HEADS + h)),
            pl.BlockSpec((seq, HEAD_DIM), lambda b, h, i: (b, base + 2 * N_HEADS + h)),
            pl.BlockSpec((1, seq, HEAD_DIM), lambda b, h, i: (b * N_HEADS + h, 0, 0)),
            pl.BlockSpec((1, BIAS_ROWS, ATT_W), lambda b, h, i: (b * N_HEADS + h, 0, i)),
        ],
        out_specs=pl.BlockSpec((ATT_W, HEAD_DIM), lambda b, h, i: (b * nq + i, h)),
        out_shape=jax.ShapeDtypeStruct((t, MIX_W), BF16),
        scratch_shapes=[pltpu.VMEM((seq, D_AUG), BF16)] + _attn_scratch(seq),
        compiler_params=_cparams(("parallel", "parallel", "arbitrary")),
        name="attn_d",
    )(z, z, z, kx, qx)


def _forget_bias_operands(cum):
    c = cum * LOG2E
    hi_f = _bf16_part(c)
    mid_f = _bf16_part(c - hi_f)
    hi, mid, lo = hi_f.astype(BF16), mid_f.astype(BF16), (c - hi_f - mid_f).astype(BF16)
    one = jnp.ones_like(hi)
    kx = jnp.stack([hi, mid, lo, one, one, one], axis=-1)
    kx = jnp.pad(kx, ((0, 0), (0, 0), (0, HEAD_DIM - 6)))
    qx = jnp.stack([-one, -one, -one, hi, mid, lo], axis=1)
    qx = jnp.pad(qx, ((0, 0), (0, BIAS_ROWS - 6), (0, 0)))
    return kx, qx


def _logf_cumsum_kernel(x_ref, b_ref, o_ref, *, rows_per_seq):
    r = x_ref.shape[0]
    lf = _log_sigmoid(x_ref[...] + b_ref[...])
    a = lax.broadcasted_iota(jnp.int32, (LANES, LANES), 0)
    b = lax.broadcasted_iota(jnp.int32, (LANES, LANES), 1)
    within = _dot_exact(lf, (a <= b).astype(F32))
    totals = _dot_exact(lf, jnp.ones((LANES, LANES), F32))
    ra = lax.broadcasted_iota(jnp.int32, (r, r), 0)
    rb = lax.broadcasted_iota(jnp.int32, (r, r), 1)
    earlier = ((rb < ra) & ((rb // rows_per_seq) == (ra // rows_per_seq))).astype(F32)
    o_ref[...] = within + _dot_exact(earlier, totals)


def _logf_cumsum(x, bias_rows, rows_per_seq):
    r = x.shape[0]
    return pl.pallas_call(
        functools.partial(_logf_cumsum_kernel, rows_per_seq=rows_per_seq),
        out_shape=jax.ShapeDtypeStruct((r, LANES), F32),
        compiler_params=pltpu.CompilerParams(vmem_limit_bytes=VMEM_LIMIT),
        name="logf_cumsum",
    )(x, bias_rows)


def _causal_conv_silu(x, prev, w):
    row = lax.broadcasted_iota(jnp.int32, x.shape, 0)
    acc = x * w[CONV_W - 1:CONV_W, :]
    for s in range(1, CONV_W):
        xs = jnp.where(row >= s, pltpu.roll(x, s, axis=0), pltpu.roll(prev, s, axis=0))
        acc = acc + xs * w[CONV_W - 1 - s:CONV_W - s, :]
    return _silu(acc)


def _chunk_masks():
    r = lax.broadcasted_iota(jnp.int32, (CHUNK, CHUNK), 0)
    c = lax.broadcasted_iota(jnp.int32, (CHUNK, CHUNK), 1)
    return r, c


def _mlstm_kernel(q_ref, k_ref, v_ref, og_ref, g_ref, conv_ref, bias_ref, norm_ref, out_ref,
                  pq_ref, pk_ref, c_ref, n_ref, m_ref):
    @pl.when(pl.program_id(1) == 0)
    def _():
        pq_ref[...] = jnp.zeros(pq_ref.shape, F32)
        pk_ref[...] = jnp.zeros(pk_ref.shape, F32)
        c_ref[...] = jnp.zeros(c_ref.shape, F32)
        n_ref[...] = jnp.zeros(n_ref.shape, F32)
        m_ref[...] = jnp.zeros(m_ref.shape, F32)

    xq = q_ref[...].astype(F32)
    xk = k_ref[...].astype(F32)
    q_all = _causal_conv_silu(xq, pq_ref[...], conv_ref[:, 0:MIX_W])
    k_all = _causal_conv_silu(xk, pk_ref[...], conv_ref[:, MIX_W:2 * MIX_W]) * (HEAD_DIM ** -0.5)
    pq_ref[...] = xq
    pk_ref[...] = xk

    lane = lax.broadcasted_iota(jnp.int32, (CHUNK, LANES), 1)
    pre = g_ref[...] + bias_ref[...]
    e = jnp.where(lane < N_HEADS, pre, _log_sigmoid(pre))
    r, c = _chunk_masks()
    cum = _dot_exact((c <= r).astype(F32), e)
    e_t = e.T
    cum_t = cum.T
    causal = c <= r

    for h in range(N_HEADS):
        sl = slice(h * HEAD_DIM, (h + 1) * HEAD_DIM)
        q = q_all[:, sl]
        k = k_all[:, sl]
        v = v_ref[:, sl].astype(F32)
        qb = q.astype(BF16)
        kb = k.astype(BF16)
        i_col = e[:, h:h + 1]
        i_row = e_t[h:h + 1, :]
        b_col = cum[:, N_HEADS + h:N_HEADS + h + 1]
        b_row = cum_t[N_HEADS + h:N_HEADS + h + 1, :]
        b_last = b_col[CHUNK - 1:CHUNK, :]
        m_st = m_ref[h:h + 1, 0:1]
        c_st = c_ref[h]
        n_st = n_ref[h:h + 1, :]

        d_log = jnp.where(causal, b_col - b_row + i_row, NEG_INF)
        inter_log = b_col + m_st
        m_t = jnp.maximum(inter_log, jnp.max(d_log, axis=-1, keepdims=True))
        d_w = jnp.exp(d_log - m_t)
        inter_w = jnp.exp(inter_log - m_t)
        a = _dot_nt(qb, kb) * d_w
        num = _dot(a.astype(BF16), v.astype(BF16)) + inter_w * _dot_nt(qb, c_st.astype(BF16))
        den = jnp.sum(a, axis=-1, keepdims=True) + inter_w * jnp.sum(q * n_st, axis=-1, keepdims=True)
        hh = num / jnp.maximum(jnp.abs(den), jnp.exp(-m_t))

        w_log = b_last - b_col + i_col
        m_new = jnp.maximum(b_last + m_st, jnp.max(w_log, axis=0, keepdims=True))
        sw = jnp.exp(w_log - m_new)
        decay = jnp.exp(b_last + m_st - m_new)
        c_ref[h] = decay * c_st + _dot((v * sw).T.astype(BF16), kb)
        n_ref[h:h + 1, :] = decay * n_st + jnp.sum(k * sw, axis=0, keepdims=True)
        m_ref[h:h + 1, :] = jnp.broadcast_to(m_new, (1, LANES))

        gate = _sigmoid(og_ref[:, sl].astype(F32))
        out_ref[:, sl] = (_rms(hh, norm_ref[...]) * gate).astype(BF16)


def _mlstm(z, zg, conv_w, gate_bias, norm, batch, seq):
    t = batch * seq
    nc = seq // CHUNK
    col = lambda j: pl.BlockSpec((CHUNK, MIX_W), lambda b, c: (b * nc + c, j))
    return pl.pallas_call(
        _mlstm_kernel,
        grid=(batch, nc),
        in_specs=[
            col(3), col(4), col(5), col(6),
            pl.BlockSpec((CHUNK, LANES), lambda b, c: (b * nc + c, 0)),
            pl.BlockSpec((CONV_W, 2 * MIX_W), lambda b, c: (0, 0)),
            pl.BlockSpec((1, LANES), lambda b, c: (0, 0)),
            pl.BlockSpec((1, HEAD_DIM), lambda b, c: (0, 0)),
        ],
        out_specs=pl.BlockSpec((CHUNK, MIX_W), lambda b, c: (b * nc + c, 0)),
        out_shape=jax.ShapeDtypeStruct((t, MIX_W), BF16),
        scratch_shapes=[pltpu.VMEM((CHUNK, MIX_W), F32), pltpu.VMEM((CHUNK, MIX_W), F32),
                        pltpu.VMEM((N_HEADS, HEAD_DIM, HEAD_DIM), F32), pltpu.VMEM((8, HEAD_DIM), F32),
                        pltpu.VMEM((8, LANES), F32)],
        compiler_params=_cparams(("parallel", "arbitrary")),
        name="mlstm",
    )(z, z, z, z, zg, conv_w, gate_bias, norm)


def _unit_lower_inverse(l_strict, r, c):
    same16 = (r >> 4) == (c >> 4)
    same32 = (r >> 5) == (c >> 5)
    eye = (r == c).astype(F32)
    n = jnp.where(same16, -l_strict, 0.0)
    inv = eye + n
    p = n
    for _ in range(3):
        p = _dot(p, p)
        inv = inv + _dot(inv, p)
    off32 = jnp.where(same32 & jnp.logical_not(same16), l_strict, 0.0)
    inv = inv - _dot(inv, _dot(off32, inv))
    off64 = jnp.where(same32, 0.0, l_strict)
    inv = inv - _dot(inv, _dot(off64, inv))
    return inv


def _deltanet_kernel(q_ref, k_ref, v_ref, og_ref, g_ref, conv_ref, par_ref, norm_ref, out_ref,
                     pq_ref, pk_ref, pv_ref, s_ref):
    @pl.when(pl.program_id(1) == 0)
    def _():
        pq_ref[...] = jnp.zeros(pq_ref.shape, F32)
        pk_ref[...] = jnp.zeros(pk_ref.shape, F32)
        pv_ref[...] = jnp.zeros(pv_ref.shape, F32)
        s_ref[...] = jnp.zeros(s_ref.shape, F32)

    xq = q_ref[...].astype(F32)
    xk = k_ref[...].astype(F32)
    xv = v_ref[...].astype(F32)
    q_all = _causal_conv_silu(xq, pq_ref[...], conv_ref[:, 0:MIX_W])
    k_all = _causal_conv_silu(xk, pk_ref[...], conv_ref[:, MIX_W:2 * MIX_W])
    v_all = _causal_conv_silu(xv, pv_ref[...], conv_ref[:, 2 * MIX_W:3 * MIX_W])
    pq_ref[...] = xq
    pk_ref[...] = xk
    pv_ref[...] = xv

    lane = lax.broadcasted_iota(jnp.int32, (CHUNK, LANES), 1)
    pre = g_ref[...]
    e = jnp.where(lane < N_HEADS, -jnp.exp(par_ref[0:1, :]) * _softplus(pre + par_ref[1:2, :]), _sigmoid(pre))
    r, c = _chunk_masks()
    cum = _dot_exact((c <= r).astype(F32), e)
    cum_t = cum.T

    for h in range(N_HEADS):
        sl = slice(h * HEAD_DIM, (h + 1) * HEAD_DIM)
        q = q_all[:, sl]
        k = k_all[:, sl]
        v = v_all[:, sl]
        q = q * lax.rsqrt(jnp.sum(q * q, axis=-1, keepdims=True) + RMS_EPS) * (HEAD_DIM ** -0.5)
        k = k * lax.rsqrt(jnp.sum(k * k, axis=-1, keepdims=True) + RMS_EPS)
        beta = e[:, N_HEADS + h:N_HEADS + h + 1]
        b_col = cum[:, h:h + 1]
        b_row = cum_t[h:h + 1, :]
        b_last = b_col[CHUNK - 1:CHUNK, :]
        gam = jnp.exp(jnp.where(c <= r, b_col - b_row, NEG_INF))
        kbeta = k * beta
        kb16 = k.astype(BF16)
        l_strict = jnp.where(c < r, _dot_nt(kbeta.astype(BF16), kb16) * gam, 0.0)
        inv = _unit_lower_inverse(l_strict, r, c)
        e_col = jnp.exp(b_col)
        u = _dot(inv, v * beta)
        w = _dot(inv, kbeta * e_col)
        qk = _dot_nt(q.astype(BF16), kb16) * gam
        q_dec = q * e_col
        k_dec = k * jnp.exp(b_last - b_col)
        st = s_ref[h]
        v_new = u - _dot(w, st)
        o = _dot(q_dec, st) + _dot(qk, v_new)
        s_ref[h] = jnp.exp(b_last) * st + _dot(k_dec.T, v_new)
        gate = _silu(og_ref[:, sl].astype(F32))
        out_ref[:, sl] = (_rms(o, norm_ref[...]) * gate).astype(BF16)


def _deltanet(z, zg, conv_w, params, norm, batch, seq):
    t = batch * seq
    nc = seq // CHUNK
    col = lambda j: pl.BlockSpec((CHUNK, MIX_W), lambda b, c: (b * nc + c, j))
    return pl.pallas_call(
        _deltanet_kernel,
        grid=(batch, nc),
        in_specs=[
            col(0), col(1), col(2), col(3),
            pl.BlockSpec((CHUNK, LANES), lambda b, c: (b * nc + c, 0)),
            pl.BlockSpec((CONV_W, 3 * MIX_W), lambda b, c: (0, 0)),
            pl.BlockSpec((2, LANES), lambda b, c: (0, 0)),
            pl.BlockSpec((1, HEAD_DIM), lambda b, c: (0, 0)),
        ],
        out_specs=pl.BlockSpec((CHUNK, MIX_W), lambda b, c: (b * nc + c, 0)),
        out_shape=jax.ShapeDtypeStruct((t, MIX_W), BF16),
        scratch_shapes=[pltpu.VMEM((CHUNK, MIX_W), F32)] * 3
        + [pltpu.VMEM((N_HEADS, HEAD_DIM, HEAD_DIM), F32)],
        compiler_params=_cparams(("parallel", "arbitrary")),
        name="deltanet",
    )(z, z, z, z, zg, conv_w, params, norm)


OUT_TM = 512
G_LANE0 = N_EXPERTS


def _outproj_router_kernel(h_ref, a_ref, b_ref, wo_ref, gn_ref, whi_ref, wlo_ref, br_ref,
                           hout_ref, xn_ref, comb_ref):
    h_new = h_ref[...] + _dot(a_ref[...], wo_ref[0:MIX_W, :]) + _dot(b_ref[...], wo_ref[MIX_W:2 * MIX_W, :])
    hout_ref[...] = h_new
    hn = _rms(h_new, gn_ref[...])
    xn_ref[...] = hn.astype(BF16)
    hi_f = _bf16_part(hn)
    hi = hi_f.astype(BF16)
    lo = (hn - hi_f).astype(BF16)
    logits = (_dot(hi, whi_ref[...]) + _dot(lo, whi_ref[...]) + _dot(hi, wlo_ref[...]) + _dot(lo, wlo_ref[...])
              + br_ref[...])

    lane_i = lax.broadcasted_iota(jnp.int32, logits.shape, 1)
    lane = lane_i.astype(F32)
    big = 1e9
    is_group = (lane_i >= G_LANE0) & (lane_i < G_LANE0 + N_GROUPS)
    gl = jnp.where(is_group, logits, NEG_INF)
    gmax = jnp.max(gl, axis=-1, keepdims=True)
    g_sel = jnp.min(jnp.where(gl == gmax, lane, big), axis=-1, keepdims=True) - G_LANE0
    p_top = 1.0 / jnp.sum(jnp.exp(gl - gmax), axis=-1, keepdims=True)
    group_of_lane = (lane_i >> 2).astype(F32)
    in_group = (lane_i < N_EXPERTS) & (group_of_lane == g_sel)
    el = jnp.where(in_group, logits, NEG_INF)
    v1 = jnp.max(el, axis=-1, keepdims=True)
    i1 = jnp.min(jnp.where(el == v1, lane, big), axis=-1, keepdims=True)
    el2 = jnp.where(lane == i1, NEG_INF, el)
    v2 = jnp.max(el2, axis=-1, keepdims=True)
    i2 = jnp.min(jnp.where(el2 == v2, lane, big), axis=-1, keepdims=True)
    e21 = jnp.exp(v2 - v1)
    w1 = p_top / (1.0 + e21)
    comb_ref[...] = jnp.where(lane == i1, w1, 0.0) + jnp.where(lane == i2, w1 * e21, 0.0)


def _outproj_router(h, oa, ob, w_out, gain, wr_hi, wr_lo, b_r):
    t = h.shape[0]
    row = lambda w: pl.BlockSpec((OUT_TM, w), lambda i: (i, 0))
    full = lambda s: pl.BlockSpec(s, lambda i: (0, 0))
    return pl.pallas_call(
        _outproj_router_kernel,
        grid=(t // OUT_TM,),
        in_specs=[row(D_MODEL), row(MIX_W), row(MIX_W), full((D_MODEL, D_MODEL)), full((1, D_MODEL)),
                  full((D_MODEL, LANES)), full((D_MODEL, LANES)), full((1, LANES))],
        out_specs=[row(D_MODEL), row(D_MODEL), row(LANES)],
        out_shape=[jax.ShapeDtypeStruct((t, D_MODEL), F32), jax.ShapeDtypeStruct((t, D_MODEL), BF16),
                   jax.ShapeDtypeStruct((t, LANES), F32)],
        compiler_params=_cparams(("parallel",)),
        name="outproj_router",
    )(h, oa, ob, w_out, gain, wr_hi, wr_lo, b_r)


MOE_TM = 1024


def _moe_kernel(x_ref, comb_ref, h_ref, wg_ref, wu_ref, wd_ref, out_ref, acc_ref):
    e = pl.program_id(1)

    @pl.when(e == 0)
    def _():
        acc_ref[...] = jnp.zeros(acc_ref.shape, F32)

    x = x_ref[...]
    he = _silu(_dot(x, wg_ref[0])) * _dot(x, wu_ref[0])
    lane = lax.broadcasted_iota(jnp.int32, comb_ref.shape, 1)
    ce = jnp.sum(jnp.where(lane == e, comb_ref[...], 0.0), axis=-1, keepdims=True)
    acc_ref[...] += _dot((he * ce).astype(BF16), wd_ref[0])

    @pl.when(e == N_EXPERTS - 1)
    def _():
        out_ref[...] = h_ref[...] + acc_ref[...]


def _moe(xn, comb, h, wg, wu, wd):
    t = h.shape[0]
    return pl.pallas_call(
        _moe_kernel,
        grid=(t // MOE_TM, N_EXPERTS),
        in_specs=[
            pl.BlockSpec((MOE_TM, D_MODEL), lambda i, e: (i, 0)),
            pl.BlockSpec((MOE_TM, LANES), lambda i, e: (i, 0)),
            pl.BlockSpec((MOE_TM, D_MODEL), lambda i, e: (i, 0)),
            pl.BlockSpec((1, D_MODEL, EXPERT_FF), lambda i, e: (e, 0, 0)),
            pl.BlockSpec((1, D_MODEL, EXPERT_FF), lambda i, e: (e, 0, 0)),
            pl.BlockSpec((1, EXPERT_FF, D_MODEL), lambda i, e: (e, 0, 0)),
        ],
        out_specs=pl.BlockSpec((MOE_TM, D_MODEL), lambda i, e: (i, 0)),
        out_shape=jax.ShapeDtypeStruct((t, D_MODEL), F32),
        scratch_shapes=[pltpu.VMEM((MOE_TM, D_MODEL), F32)],
        compiler_params=_cparams(("parallel", "arbitrary")),
        name="moe",
    )(xn, comb, h, wg, wu, wd)


PLE_TM = 512


def _ple_kernel(h_ref, p_ref, wg_ref, wp_ref, gn_ref, out_ref, *, final_norm):
    h = h_ref[...]
    gate = _sigmoid(_dot(h.astype(BF16), wg_ref[...]))
    out = h + gate * _dot(p_ref[...].astype(BF16), wp_ref[...])
    if final_norm:
        out = _rms(out, gn_ref[...])
    out_ref[...] = out


def _ple(h, p, w_gate, w_proj, gain, final_norm):
    t = h.shape[0]
    return pl.pallas_call(
        functools.partial(_ple_kernel, final_norm=final_norm),
        grid=(t // PLE_TM,),
        in_specs=[
            pl.BlockSpec((PLE_TM, D_MODEL), lambda i: (i, 0)),
            pl.BlockSpec((PLE_TM, PLE_DIM), lambda i: (i, 0)),
            pl.BlockSpec((D_MODEL, D_MODEL), lambda i: (0, 0)),
            pl.BlockSpec((PLE_DIM, D_MODEL), lambda i: (0, 0)),
            pl.BlockSpec((1, D_MODEL), lambda i: (0, 0)),
        ],
        out_specs=pl.BlockSpec((PLE_TM, D_MODEL), lambda i: (i, 0)),
        out_shape=jax.ShapeDtypeStruct((t, D_MODEL), F32),
        compiler_params=_cparams(("parallel",)),
        name="ple",
    )(h, p, w_gate, w_proj, gain)


def _pad_lanes(x):
    return jnp.pad(x, ((0, 0), (0, LANES - x.shape[-1])))


def _row(x):
    return x.reshape(1, -1).astype(F32)


def kernel(x, p, norm_mix, norm_ffn, norm_final, ab_w_in, ab_w_out, a_lam_q1, a_lam_k1, a_lam_q2, a_lam_k2,
           a_subln, b_conv, b_igate_bias, b_fgate_bias, b_norm, cd_w_in, cd_w_out, c_conv, c_a_log, c_dt_bias,
           c_norm, d_fgate_bias, moe_w_group, moe_b_group, moe_w_router, moe_b_router, moe_w_gate, moe_w_up,
           moe_w_down, ple_w_gate, ple_w_proj):
    batch, seq, _ = x.shape
    depth = p.shape[0]
    t = batch * seq
    assert seq % IN_TM == 0 and seq % ATT_TK == 0 and t % MOE_TM == 0
    h = x.reshape(t, D_MODEL)
    cos, sin = _rope_tables(seq)
    ones_tab = jnp.ones((seq, LANES), F32)

    for i in range(depth):
        j = i // 2
        if i % 2 == 0:
            w_in = ab_w_in[j]
            z, zg = _inproj(h, _row(norm_mix[i]), w_in[:, :Z_MAIN].astype(BF16),
                            _pad_lanes(w_in[:, Z_MAIN:]).astype(BF16), cos, sin, seq, rope_chunks=2)
            lam_init = 0.8 - 0.6 * math.exp(-0.3 * i)
            lam_p = jnp.stack([a_lam_q1[j], a_lam_k1[j], a_lam_q2[j], a_lam_k2[j]]).astype(F32)
            out_1 = _attn_a(z, lam_p, a_subln[j].reshape(HEAD_DIM, 1).astype(F32), batch, seq, lam_init)
            gate_bias = _pad_lanes(jnp.concatenate([b_igate_bias[j], b_fgate_bias[j]]).reshape(1, -1))
            out_2 = _mlstm(z, zg, b_conv[j], gate_bias, _row(b_norm[j]), batch, seq)
            w_out = ab_w_out[j]
        else:
            w_in = cd_w_in[j]
            c_main, d_main = 4 * MIX_W, 3 * MIX_W
            w_main = jnp.concatenate([w_in[:, :c_main], w_in[:, c_main + 8:c_main + 8 + d_main]], axis=1)
            w_small = jnp.concatenate([w_in[:, c_main:c_main + 8], w_in[:, c_main + 8 + d_main:]], axis=1)
            z, zg = _inproj(h, _row(norm_mix[i]), w_main.astype(BF16), _pad_lanes(w_small).astype(BF16),
                            ones_tab, ones_tab, seq, rope_chunks=0)
            params = jnp.stack([_pad_lanes(c_a_log[j].reshape(1, -1))[0], _pad_lanes(c_dt_bias[j].reshape(1, -1))[0]])
            out_1 = _deltanet(z, zg, c_conv[j], params, _row(c_norm[j]), batch, seq)
            fd = zg[:, 8:8 + N_HEADS].reshape(batch, seq, N_HEADS).transpose(0, 2, 1)
            bias_rows = jnp.repeat(jnp.tile(d_fgate_bias[j], batch), seq // LANES).reshape(-1, 1)
            cum = _logf_cumsum(fd.reshape(-1, LANES), jnp.broadcast_to(bias_rows, (bias_rows.shape[0], LANES)),
                               seq // LANES)
            kx, qx = _forget_bias_operands(cum.reshape(batch * N_HEADS, seq))
            out_2 = _attn_d(z, kx, qx, batch, seq)
            w_out = cd_w_out[j]

        w_r = _pad_lanes(jnp.concatenate([moe_w_router[i], moe_w_group[i]], axis=1))
        w_r_hi_f = _bf16_part(w_r)
        w_r_hi = w_r_hi_f.astype(BF16)
        w_r_lo = (w_r - w_r_hi_f).astype(BF16)
        b_r = _pad_lanes(jnp.concatenate([moe_b_router[i], moe_b_group[i]]).reshape(1, -1))
        h, xn, comb = _outproj_router(h, out_1, out_2, w_out.astype(BF16), _row(norm_ffn[i]), w_r_hi, w_r_lo, b_r)
        h = _moe(xn, comb, h, moe_w_gate[i].astype(BF16), moe_w_up[i].astype(BF16), moe_w_down[i].astype(BF16))
        h = _ple(h, p[i].reshape(t, PLE_DIM), ple_w_gate[i].astype(BF16), ple_w_proj[i].astype(BF16),
                 _row(norm_final), final_norm=(i == depth - 1))
    return h.reshape(batch, seq, D_MODEL)
```

```python
import functools
import math

import jax
import jax.numpy as jnp
from jax import lax
from jax.experimental import pallas as pl
from jax.experimental.pallas import tpu as pltpu

F32 = jnp.float32
BF16 = jnp.bfloat16
HIGHEST = lax.Precision.HIGHEST

D_MODEL = 1024
HEAD_DIM = 128
N_HEADS = 4
MIX_W = N_HEADS * HEAD_DIM
A_HALF = HEAD_DIM // 2
CHUNK = 64
CONV_W = 4
RMS_EPS = 1e-6
PLE_DIM = 256
N_GROUPS = 4
EXPERTS_PER_GROUP = 4
N_EXPERTS = 16
EXPERT_FF = D_MODEL // 2
ROPE_THETA = 10000.0
Z_MAIN = 7 * MIX_W
LANES = 128

VMEM_LIMIT = 56 * 1024 * 1024

NEG_INF = float("-inf")


def _cparams(sem):
    return pltpu.CompilerParams(dimension_semantics=sem, vmem_limit_bytes=VMEM_LIMIT)


def _dot(a, b):
    return jnp.dot(a, b, preferred_element_type=F32)


def _dot_nt(a, b):
    return lax.dot_general(a, b, (((1,), (1,)), ((), ())), preferred_element_type=F32)


def _dot_exact(a, b):
    return jnp.dot(a, b, preferred_element_type=F32, precision=HIGHEST)


def _bf16_part(x):
    bits = lax.bitcast_convert_type(x, jnp.int32) & jnp.int32(-65536)
    return lax.bitcast_convert_type(bits, F32)


def _sigmoid(x):
    return 1.0 / (1.0 + jnp.exp(-x))


def _silu(x):
    return x * _sigmoid(x)


def _log_sigmoid(x):
    return jnp.minimum(x, 0.0) - jnp.log(1.0 + jnp.exp(-jnp.abs(x)))


def _softplus(x):
    return jnp.maximum(x, 0.0) + jnp.log(1.0 + jnp.exp(-jnp.abs(x)))


def _rms(x, gain):
    return x * lax.rsqrt(jnp.mean(x * x, axis=-1, keepdims=True) + RMS_EPS) * gain


IN_TM = 512
IN_TN = 512


def _inproj_kernel(h_ref, g_ref, w_ref, wg_ref, cos_ref, sin_ref, z_ref, zg_ref, *, rope_chunks):
    hn = _rms(h_ref[...], g_ref[...])
    hb = hn.astype(BF16)
    zg_ref[...] = _dot(hb, wg_ref[...])
    if rope_chunks:
        lane = lax.broadcasted_iota(jnp.int32, (IN_TM, IN_TN), 1)
        first_half = (lane & 32) == 0
        cos = jnp.concatenate([cos_ref[...]] * (IN_TN // LANES), axis=1)
        sin = jnp.concatenate([sin_ref[...]] * (IN_TN // LANES), axis=1)
    for c in range(Z_MAIN // IN_TN):
        zc = _dot(hb, w_ref[:, c * IN_TN:(c + 1) * IN_TN])
        if c < rope_chunks:
            partner = jnp.where(first_half, pltpu.roll(zc, IN_TN - 32, axis=1), pltpu.roll(zc, 32, axis=1))
            zc = zc * cos + partner * sin
        z_ref[:, c * IN_TN:(c + 1) * IN_TN] = zc.astype(BF16)


def _inproj(h, gain, w_main, w_gate, cos, sin, seq, rope_chunks):
    t = h.shape[0]
    per_seq = seq // IN_TM
    return pl.pallas_call(
        functools.partial(_inproj_kernel, rope_chunks=rope_chunks),
        grid=(t // IN_TM,),
        in_specs=[
            pl.BlockSpec((IN_TM, D_MODEL), lambda i: (i, 0)),
            pl.BlockSpec((1, D_MODEL), lambda i: (0, 0)),
            pl.BlockSpec((D_MODEL, Z_MAIN), lambda i: (0, 0)),
            pl.BlockSpec((D_MODEL, LANES), lambda i: (0, 0)),
            pl.BlockSpec((IN_TM, LANES), lambda i: (i % per_seq, 0)),
            pl.BlockSpec((IN_TM, LANES), lambda i: (i % per_seq, 0)),
        ],
        out_specs=[
            pl.BlockSpec((IN_TM, Z_MAIN), lambda i: (i, 0)),
            pl.BlockSpec((IN_TM, LANES), lambda i: (i, 0)),
        ],
        out_shape=[jax.ShapeDtypeStruct((t, Z_MAIN), BF16), jax.ShapeDtypeStruct((t, LANES), F32)],
        compiler_params=_cparams(("parallel",)),
        name="inproj",
    )(h, gain, w_main, w_gate, cos, sin)


def _rope_tables(seq):
    inv = ROPE_THETA ** (-jnp.arange(0, A_HALF, 2, dtype=F32) / A_HALF)
    ang = jnp.arange(seq, dtype=F32)[:, None] * inv[None, :]
    cos, sin = jnp.cos(ang), jnp.sin(ang)
    return jnp.tile(cos, (1, 4)), jnp.tile(jnp.concatenate([-sin, sin], axis=1), (1, 2))


ATT_W = 512
ATT_TK = 512
ATT_TQ_A = ATT_W // 2
LOG2E = 1.4426950408889634


def _attn_init(v_ref, vt_ref, m_ref, l_ref, acc_ref):
    @pl.when(pl.program_id(2) == 0)
    def _():
        def body(j, carry):
            off = pl.multiple_of(j * ATT_TK, ATT_TK)
            vt_ref[j] = v_ref[pl.ds(off, ATT_TK), :].astype(F32).T.astype(BF16)
            return carry
        lax.fori_loop(0, vt_ref.shape[0], body, 0)

    m_ref[...] = jnp.full(m_ref.shape, NEG_INF, F32)
    l_ref[...] = jnp.zeros(l_ref.shape, F32)
    acc_ref[...] = jnp.zeros(acc_ref.shape, F32)


def _attn_logits(k_ref, q_t, s_ref, blk, mask):
    off = pl.multiple_of(blk * ATT_TK, ATT_TK)
    s = _dot(k_ref[pl.ds(off, ATT_TK), :], q_t)
    if mask is not None:
        s = jnp.where(mask, s, NEG_INF)
    s_ref[...] = s


def _attn_consume(s_ref, vt_ref, blk, m_ref, l_ref, acc_ref):
    s = s_ref[...]
    m_prev = m_ref[...]
    m_new = jnp.maximum(m_prev, jnp.max(s, axis=0, keepdims=True))
    alpha = jnp.exp2(m_prev - m_new)
    p = jnp.exp2(s - m_new)
    l_ref[...] = alpha * l_ref[...] + jnp.sum(p, axis=0, keepdims=True)
    acc_ref[...] = alpha * acc_ref[...] + _dot(vt_ref[blk], p.astype(BF16))
    m_ref[...] = m_new


def _attn_pipeline(k_ref, vt_ref, q_t, s0_ref, s1_ref, m_ref, l_ref, acc_ref, n_full, diag_blk, diag_mask):
    def consume(s_ref, blk):
        _attn_consume(s_ref, vt_ref, blk, m_ref, l_ref, acc_ref)

    _attn_logits(k_ref, q_t, s0_ref, diag_blk, diag_mask)

    def body(t, carry):
        _attn_logits(k_ref, q_t, s1_ref, 2 * t, None)
        consume(s0_ref, jnp.where(t == 0, diag_blk, 2 * t - 1))
        _attn_logits(k_ref, q_t, s0_ref, 2 * t + 1, None)
        consume(s1_ref, 2 * t)
        return carry

    npairs = n_full // 2
    lax.fori_loop(0, npairs, body, 0)
    in_s0 = jnp.where(npairs == 0, diag_blk, 2 * npairs - 1)

    @pl.when(n_full % 2 == 1)
    def _():
        _attn_logits(k_ref, q_t, s1_ref, n_full - 1, None)
        consume(s0_ref, in_s0)
        consume(s1_ref, n_full - 1)

    @pl.when(n_full % 2 == 0)
    def _():
        consume(s0_ref, in_s0)


def _attn_a_kernel(q_ref, k_ref, v_ref, lam_ref, gain_ref, o_ref, vt_ref, s0_ref, s1_ref, m_ref, l_ref, acc_ref,
                   *, lam_init):
    _attn_init(v_ref, vt_ref, m_ref, l_ref, acc_ref)
    i = pl.program_id(2)
    tq = ATT_TQ_A
    q = (q_ref[...].astype(F32) * (A_HALF ** -0.5 * LOG2E)).T
    dim = lax.broadcasted_iota(jnp.int32, q.shape, 0)
    q_t = jnp.concatenate([jnp.where(dim < A_HALF, q, 0.0), jnp.where(dim >= A_HALF, q, 0.0)], axis=1).astype(BF16)
    key = lax.broadcasted_iota(jnp.int32, (ATT_TK, ATT_W), 0)
    qry = lax.broadcasted_iota(jnp.int32, (ATT_TK, ATT_W), 1) & (tq - 1)
    diag_mask = ((key >> 6) - (qry >> 6)) <= 4 * (i % 2)
    _attn_pipeline(k_ref, vt_ref, q_t, s0_ref, s1_ref, m_ref, l_ref, acc_ref, i // 2, i // 2, diag_mask)

    lam_p = lam_ref[...]
    lam = (jnp.exp(jnp.sum(lam_p[0:1] * lam_p[1:2], axis=-1, keepdims=True))
           - jnp.exp(jnp.sum(lam_p[2:3] * lam_p[3:4], axis=-1, keepdims=True)) + lam_init)
    o_all = acc_ref[...] * (1.0 / l_ref[...])
    out = o_all[:, 0:tq] - lam * o_all[:, tq:2 * tq]
    out = out * lax.rsqrt(jnp.mean(out * out, axis=0, keepdims=True) + RMS_EPS) * (gain_ref[...] * (1.0 - lam_init))
    o_ref[...] = out.T.astype(BF16)


def _attn_scratch(seq):
    return [pltpu.VMEM((seq // ATT_TK, HEAD_DIM, ATT_TK), BF16), pltpu.VMEM((ATT_TK, ATT_W), F32),
            pltpu.VMEM((ATT_TK, ATT_W), F32), pltpu.VMEM((1, ATT_W), F32), pltpu.VMEM((1, ATT_W), F32),
            pltpu.VMEM((HEAD_DIM, ATT_W), F32)]


def _attn_a(z, lam_p, gain_col, batch, seq, lam_init):
    t = batch * seq
    nq = seq // ATT_TQ_A
    return pl.pallas_call(
        functools.partial(_attn_a_kernel, lam_init=lam_init),
        grid=(batch, N_HEADS, nq),
        in_specs=[
            pl.BlockSpec((ATT_TQ_A, HEAD_DIM), lambda b, h, i: (b * nq + i, h)),
            pl.BlockSpec((seq, HEAD_DIM), lambda b, h, i: (b, N_HEADS + h)),
            pl.BlockSpec((seq, HEAD_DIM), lambda b, h, i: (b, 2 * N_HEADS + h)),
            pl.BlockSpec((4, A_HALF), lambda b, h, i: (0, 0)),
            pl.BlockSpec((HEAD_DIM, 1), lambda b, h, i: (0, 0)),
        ],
        out_specs=pl.BlockSpec((ATT_TQ_A, HEAD_DIM), lambda b, h, i: (b * nq + i, h)),
        out_shape=jax.ShapeDtypeStruct((t, MIX_W), BF16),
        scratch_shapes=_attn_scratch(seq),
        compiler_params=_cparams(("parallel", "parallel", "arbitrary")),
        name="attn_a",
    )(z, z, z, lam_p, gain_col)


D_AUG = 2 * HEAD_DIM
BIAS_ROWS = 16


def _attn_d_kernel(q_ref, k_ref, v_ref, kx_ref, qx_ref, o_ref, kaug_ref, vt_ref, s0_ref, s1_ref, m_ref, l_ref,
                   acc_ref):
    _attn_init(v_ref, vt_ref, m_ref, l_ref, acc_ref)
    i = pl.program_id(2)

    @pl.when(i == 0)
    def _():
        def body(j, carry):
            off = pl.multiple_of(j * ATT_TK, ATT_TK)
            kaug_ref[pl.ds(off, ATT_TK), 0:HEAD_DIM] = k_ref[pl.ds(off, ATT_TK), :]
            kaug_ref[pl.ds(off, ATT_TK), HEAD_DIM:D_AUG] = kx_ref[0, pl.ds(off, ATT_TK), :]
            return carry
        lax.fori_loop(0, vt_ref.shape[0], body, 0)

    q = (q_ref[...].astype(F32) * (HEAD_DIM ** -0.5 * LOG2E)).T.astype(BF16)
    q_t = jnp.concatenate([q, qx_ref[0], jnp.zeros((D_AUG - HEAD_DIM - BIAS_ROWS, ATT_W), BF16)], axis=0)
    key = lax.broadcasted_iota(jnp.int32, (ATT_TK, ATT_W), 0)
    qry = lax.broadcasted_iota(jnp.int32, (ATT_TK, ATT_W), 1)
    _attn_pipeline(kaug_ref, vt_ref, q_t, s0_ref, s1_ref, m_ref, l_ref, acc_ref, i, i, key <= qry)
    o_ref[...] = (acc_ref[...] * (1.0 / l_ref[...])).T.astype(BF16)


def _attn_d(z, kx, qx, batch, seq):
    t = batch * seq
    nq = seq // ATT_W
    base = 4 * N_HEADS
    return pl.pallas_call(
        _attn_d_kernel,
        grid=(batch, N_HEADS, nq),
        in_specs=[
            pl.BlockSpec((ATT_W, HEAD_DIM), lambda b, h, i: (b * nq + i, base + h)),
            pl.BlockSpec((seq, HEAD_DIM), lambda b, h, i: (b, base + N_HEADS + h)),
            pl.BlockSpec((seq, HEAD_DIM), lambda b, h, i: (b, base + 2 * N_HEADS + h)),
            pl.BlockSpec((1, seq, HEAD_DIM), lambda b, h, i: (b * N_HEADS + h, 0, 0)),
            pl.BlockSpec((1, BIAS_ROWS, ATT_W), lambda b, h, i: (b * N_HEADS + h, 0, i)),
        ],
        out_specs=pl.BlockSpec((ATT_W, HEAD_DIM), lambda b, h, i: (b * nq + i, h)),
        out_shape=jax.ShapeDtypeStruct((t, MIX_W), BF16),
        scratch_shapes=[pltpu.VMEM((seq, D_AUG), BF16)] + _attn_scratch(seq),
        compiler_params=_cparams(("parallel", "parallel", "arbitrary")),
        name="attn_d",
    )(z, z, z, kx, qx)


def _forget_bias_operands(cum):
    c = cum * LOG2E
    hi_f = _bf16_part(c)
    mid_f = _bf16_part(c - hi_f)
    hi, mid, lo = hi_f.astype(BF16), mid_f.astype(BF16), (c - hi_f - mid_f).astype(BF16)
    one = jnp.ones_like(hi)
    kx = jnp.stack([hi, mid, lo, one, one, one], axis=-1)
    kx = jnp.pad(kx, ((0, 0), (0, 0), (0, HEAD_DIM - 6)))
    qx = jnp.stack([-one, -one, -one, hi, mid, lo], axis=1)
    qx = jnp.pad(qx, ((0, 0), (0, BIAS_ROWS - 6), (0, 0)))
    return kx, qx


def _logf_cumsum_kernel(x_ref, b_ref, o_ref, *, rows_per_seq):
    r = x_ref.shape[0]
    lf = _log_sigmoid(x_ref[...] + b_ref[...])
    a = lax.broadcasted_iota(jnp.int32, (LANES, LANES), 0)
    b = lax.broadcasted_iota(jnp.int32, (LANES, LANES), 1)
    within = _dot_exact(lf, (a <= b).astype(F32))
    totals = _dot_exact(lf, jnp.ones((LANES, LANES), F32))
    ra = lax.broadcasted_iota(jnp.int32, (r, r), 0)
    rb = lax.broadcasted_iota(jnp.int32, (r, r), 1)
    earlier = ((rb < ra) & ((rb // rows_per_seq) == (ra // rows_per_seq))).astype(F32)
    o_ref[...] = within + _dot_exact(earlier, totals)


def _logf_cumsum(x, bias_rows, rows_per_seq):
    r = x.shape[0]
    return pl.pallas_call(
        functools.partial(_logf_cumsum_kernel, rows_per_seq=rows_per_seq),
        out_shape=jax.ShapeDtypeStruct((r, LANES), F32),
        compiler_params=pltpu.CompilerParams(vmem_limit_bytes=VMEM_LIMIT),
        name="logf_cumsum",
    )(x, bias_rows)


def _causal_conv_silu(x, prev, w):
    row = lax.broadcasted_iota(jnp.int32, x.shape, 0)
    acc = x * w[CONV_W - 1:CONV_W, :]
    for s in range(1, CONV_W):
        xs = jnp.where(row >= s, pltpu.roll(x, s, axis=0), pltpu.roll(prev, s, axis=0))
        acc = acc + xs * w[CONV_W - 1 - s:CONV_W - s, :]
    return _silu(acc)


def _chunk_masks():
    r = lax.broadcasted_iota(jnp.int32, (CHUNK, CHUNK), 0)
    c = lax.broadcasted_iota(jnp.int32, (CHUNK, CHUNK), 1)
    return r, c


REC_NB = 2


def _rec_specs(nb, cols):
    main = [pl.BlockSpec((nb, CHUNK, MIX_W), lambda b, c, j=j: (b, c, j)) for j in cols]
    return main + [pl.BlockSpec((nb, CHUNK, LANES), lambda b, c: (b, c, 0))]


def _mlstm_kernel(q_ref, k_ref, v_ref, og_ref, g_ref, conv_ref, bias_ref, norm_ref, out_ref,
                  pq_ref, pk_ref, c_ref, n_ref, m_ref):
    @pl.when(pl.program_id(1) == 0)
    def _():
        pq_ref[...] = jnp.zeros(pq_ref.shape, F32)
        pk_ref[...] = jnp.zeros(pk_ref.shape, F32)
        c_ref[...] = jnp.zeros(c_ref.shape, F32)
        n_ref[...] = jnp.zeros(n_ref.shape, F32)
        m_ref[...] = jnp.zeros(m_ref.shape, F32)

    lane = lax.broadcasted_iota(jnp.int32, (CHUNK, LANES), 1)
    r, c = _chunk_masks()
    causal = c <= r
    tri = causal.astype(F32)
    units = []
    for bi in range(q_ref.shape[0]):
        xq = q_ref[bi].astype(F32)
        xk = k_ref[bi].astype(F32)
        q_all = _causal_conv_silu(xq, pq_ref[bi], conv_ref[:, 0:MIX_W])
        k_all = _causal_conv_silu(xk, pk_ref[bi], conv_ref[:, MIX_W:2 * MIX_W]) * (HEAD_DIM ** -0.5)
        pq_ref[bi] = xq
        pk_ref[bi] = xk
        pre = g_ref[bi] + bias_ref[...]
        e = jnp.where(lane < N_HEADS, pre, _log_sigmoid(pre))
        cum = _dot_exact(tri, e)
        e_t = e.T
        cum_t = cum.T
        for h in range(N_HEADS):
            sl = slice(h * HEAD_DIM, (h + 1) * HEAD_DIM)
            u = dict(bi=bi, h=h, sl=sl, q=q_all[:, sl], k=k_all[:, sl], v=v_ref[bi, :, sl].astype(F32))
            u["qb"] = u["q"].astype(BF16)
            u["kb"] = u["k"].astype(BF16)
            i_col = e[:, h:h + 1]
            i_row = e_t[h:h + 1, :]
            b_col = cum[:, N_HEADS + h:N_HEADS + h + 1]
            b_row = cum_t[N_HEADS + h:N_HEADS + h + 1, :]
            b_last = b_col[CHUNK - 1:CHUNK, :]
            m_st = m_ref[bi, h:h + 1, 0:1]
            u["c_st"] = c_ref[bi, h]
            u["n_st"] = n_ref[bi, h:h + 1, :]
            d_log = jnp.where(causal, b_col - b_row + i_row, NEG_INF)
            inter_log = b_col + m_st
            u["m_t"] = jnp.maximum(inter_log, jnp.max(d_log, axis=-1, keepdims=True))
            u["d_w"] = jnp.exp(d_log - u["m_t"])
            u["inter_w"] = jnp.exp(inter_log - u["m_t"])
            w_log = b_last - b_col + i_col
            u["m_new"] = jnp.maximum(b_last + m_st, jnp.max(w_log, axis=0, keepdims=True))
            u["sw"] = jnp.exp(w_log - u["m_new"])
            u["decay"] = jnp.exp(b_last + m_st - u["m_new"])
            units.append(u)

    for u in units:
        u["qk"] = _dot_nt(u["qb"], u["kb"])
    for u in units:
        u["qc"] = _dot_nt(u["qb"], u["c_st"].astype(BF16))
    for u in units:
        u["upd"] = _dot((u["v"] * u["sw"]).T.astype(BF16), u["kb"])
    for u in units:
        u["a"] = u["qk"] * u["d_w"]
        u["av"] = _dot(u["a"].astype(BF16), u["v"].astype(BF16))
    for u in units:
        bi, h, sl = u["bi"], u["h"], u["sl"]
        num = u["av"] + u["inter_w"] * u["qc"]
        den = (jnp.sum(u["a"], axis=-1, keepdims=True)
               + u["inter_w"] * jnp.sum(u["q"] * u["n_st"], axis=-1, keepdims=True))
        hh = num / jnp.maximum(jnp.abs(den), jnp.exp(-u["m_t"]))
        c_ref[bi, h] = u["decay"] * u["c_st"] + u["upd"]
        n_ref[bi, h:h + 1, :] = u["decay"] * u["n_st"] + jnp.sum(u["k"] * u["sw"], axis=0, keepdims=True)
        m_ref[bi, h:h + 1, :] = jnp.broadcast_to(u["m_new"], (1, LANES))
        gate = _sigmoid(og_ref[bi, :, sl].astype(F32))
        out_ref[bi, :, sl] = (_rms(hh, norm_ref[...]) * gate).astype(BF16)


def _mlstm(z, zg, conv_w, gate_bias, norm, batch, seq):
    nb = REC_NB if batch % REC_NB == 0 else 1
    full = lambda s: pl.BlockSpec(s, lambda b, c: (0, 0))
    out = pl.pallas_call(
        _mlstm_kernel,
        grid=(batch // nb, seq // CHUNK),
        in_specs=_rec_specs(nb, (3, 4, 5, 6)) + [full((CONV_W, 2 * MIX_W)), full((1, LANES)), full((1, HEAD_DIM))],
        out_specs=pl.BlockSpec((nb, CHUNK, MIX_W), lambda b, c: (b, c, 0)),
        out_shape=jax.ShapeDtypeStruct((batch, seq, MIX_W), BF16),
        scratch_shapes=[pltpu.VMEM((nb, CHUNK, MIX_W), F32), pltpu.VMEM((nb, CHUNK, MIX_W), F32),
                        pltpu.VMEM((nb, N_HEADS, HEAD_DIM, HEAD_DIM), F32), pltpu.VMEM((nb, 8, HEAD_DIM), F32),
                        pltpu.VMEM((nb, 8, LANES), F32)],
        compiler_params=_cparams(("parallel", "arbitrary")),
        name="mlstm",
    )(*([z.reshape(batch, seq, Z_MAIN)] * 4), zg.reshape(batch, seq, LANES), conv_w, gate_bias, norm)
    return out.reshape(batch * seq, MIX_W)


def _bdot(a, b):
    return _dot(a.astype(BF16), b.astype(BF16))


def _unit_lower_inverses(l_stricts, r, c):
    same16 = (r >> 4) == (c >> 4)
    same32 = (r >> 5) == (c >> 5)
    eye = (r == c).astype(F32)
    ps = [jnp.where(same16, -l, 0.0) for l in l_stricts]
    invs = [eye + p for p in ps]
    for _ in range(3):
        ps = [_bdot(p, p) for p in ps]
        invs = [inv + _bdot(inv, p) for inv, p in zip(invs, ps)]
    for off_mask in (same32 & jnp.logical_not(same16), jnp.logical_not(same32)):
        ts = [_bdot(jnp.where(off_mask, l, 0.0), inv) for l, inv in zip(l_stricts, invs)]
        invs = [inv - _bdot(inv, t) for inv, t in zip(invs, ts)]
    return invs


def _deltanet_kernel(q_ref, k_ref, v_ref, og_ref, g_ref, conv_ref, par_ref, norm_ref, out_ref,
                     pq_ref, pk_ref, pv_ref, s_ref):
    @pl.when(pl.program_id(1) == 0)
    def _():
        pq_ref[...] = jnp.zeros(pq_ref.shape, F32)
        pk_ref[...] = jnp.zeros(pk_ref.shape, F32)
        pv_ref[...] = jnp.zeros(pv_ref.shape, F32)
        s_ref[...] = jnp.zeros(s_ref.shape, F32)

    lane = lax.broadcasted_iota(jnp.int32, (CHUNK, LANES), 1)
    r, c = _chunk_masks()
    tri = (c <= r).astype(F32)
    units = []
    for bi in range(q_ref.shape[0]):
        xq = q_ref[bi].astype(F32)
        xk = k_ref[bi].astype(F32)
        xv = v_ref[bi].astype(F32)
        q_all = _causal_conv_silu(xq, pq_ref[bi], conv_ref[:, 0:MIX_W])
        k_all = _causal_conv_silu(xk, pk_ref[bi], conv_ref[:, MIX_W:2 * MIX_W])
        v_all = _causal_conv_silu(xv, pv_ref[bi], conv_ref[:, 2 * MIX_W:3 * MIX_W])
        pq_ref[bi] = xq
        pk_ref[bi] = xk
        pv_ref[bi] = xv
        pre = g_ref[bi]
        e = jnp.where(lane < N_HEADS, -jnp.exp(par_ref[0:1, :]) * _softplus(pre + par_ref[1:2, :]), _sigmoid(pre))
        cum = _dot_exact(tri, e)
        cum_t = cum.T
        for h in range(N_HEADS):
            sl = slice(h * HEAD_DIM, (h + 1) * HEAD_DIM)
            q = q_all[:, sl]
            k = k_all[:, sl]
            q = q * lax.rsqrt(jnp.sum(q * q, axis=-1, keepdims=True) + RMS_EPS) * (HEAD_DIM ** -0.5)
            k = k * lax.rsqrt(jnp.sum(k * k, axis=-1, keepdims=True) + RMS_EPS)
            beta = e[:, N_HEADS + h:N_HEADS + h + 1]
            b_col = cum[:, h:h + 1]
            b_row = cum_t[h:h + 1, :]
            b_last = b_col[CHUNK - 1:CHUNK, :]
            e_col = jnp.exp(b_col)
            kbeta = k * beta
            units.append(dict(
                bi=bi, h=h, sl=sl, k=k, kbeta=kbeta, q_dec=q * e_col,
                gam=jnp.exp(jnp.where(c <= r, b_col - b_row, NEG_INF)),
                rhs=jnp.concatenate([v_all[:, sl] * beta, kbeta * e_col], axis=1),
                k_dec=k * jnp.exp(b_last - b_col), chunk_decay=jnp.exp(b_last),
                kq=jnp.concatenate([kbeta, q], axis=0)))

    for u in units:
        both = _dot_nt(u["kq"].astype(BF16), u["k"].astype(BF16)) * jnp.concatenate([u["gam"], u["gam"]], axis=0)
        u["l"] = jnp.where(c < r, both[0:CHUNK], 0.0)
        u["qk"] = both[CHUNK:2 * CHUNK]
    invs = _unit_lower_inverses([u["l"] for u in units], r, c)
    for u, inv in zip(units, invs):
        sol = _bdot(inv, u["rhs"])
        u["u"] = sol[:, 0:HEAD_DIM]
        u["w"] = sol[:, HEAD_DIM:2 * HEAD_DIM]
    for u in units:
        u["st"] = s_ref[u["bi"], u["h"]]
        u["ws"] = _bdot(jnp.concatenate([u["w"], u["q_dec"]], axis=0), u["st"])
    for u in units:
        u["v_new"] = u["u"] - u["ws"][0:CHUNK]
        u["o"] = u["ws"][CHUNK:2 * CHUNK] + _bdot(u["qk"], u["v_new"])
    for u in units:
        s_ref[u["bi"], u["h"]] = u["chunk_decay"] * u["st"] + _bdot(u["k_dec"].T, u["v_new"])
    for u in units:
        bi, sl = u["bi"], u["sl"]
        gate = _silu(og_ref[bi, :, sl].astype(F32))
        out_ref[bi, :, sl] = (_rms(u["o"], norm_ref[...]) * gate).astype(BF16)


def _deltanet(z, zg, conv_w, params, norm, batch, seq):
    nb = REC_NB if batch % REC_NB == 0 else 1
    full = lambda s: pl.BlockSpec(s, lambda b, c: (0, 0))
    out = pl.pallas_call(
        _deltanet_kernel,
        grid=(batch // nb, seq // CHUNK),
        in_specs=_rec_specs(nb, (0, 1, 2, 3)) + [full((CONV_W, 3 * MIX_W)), full((2, LANES)), full((1, HEAD_DIM))],
        out_specs=pl.BlockSpec((nb, CHUNK, MIX_W), lambda b, c: (b, c, 0)),
        out_shape=jax.ShapeDtypeStruct((batch, seq, MIX_W), BF16),
        scratch_shapes=[pltpu.VMEM((nb, CHUNK, MIX_W), F32)] * 3
        + [pltpu.VMEM((nb, N_HEADS, HEAD_DIM, HEAD_DIM), F32)],
        compiler_params=_cparams(("parallel", "arbitrary")),
        name="deltanet",
    )(*([z.reshape(batch, seq, Z_MAIN)] * 4), zg.reshape(batch, seq, LANES), conv_w, params, norm)
    return out.reshape(batch * seq, MIX_W)


OUT_TM = 512
G_LANE0 = N_EXPERTS


def _outproj_router_kernel(h_ref, a_ref, b_ref, wo_ref, gn_ref, whi_ref, wlo_ref, br_ref,
                           hout_ref, xn_ref, comb_ref):
    h_new = h_ref[...] + _dot(a_ref[...], wo_ref[0:MIX_W, :]) + _dot(b_ref[...], wo_ref[MIX_W:2 * MIX_W, :])
    hout_ref[...] = h_new
    hn = _rms(h_new, gn_ref[...])
    xn_ref[...] = hn
    hi_f = _bf16_part(hn)
    hi = hi_f.astype(BF16)
    lo = (hn - hi_f).astype(BF16)
    logits = (_dot(hi, whi_ref[...]) + _dot(lo, whi_ref[...]) + _dot(hi, wlo_ref[...]) + _dot(lo, wlo_ref[...])
              + br_ref[...])

    lane_i = lax.broadcasted_iota(jnp.int32, logits.shape, 1)
    lane = lane_i.astype(F32)
    big = 1e9
    is_group = (lane_i >= G_LANE0) & (lane_i < G_LANE0 + N_GROUPS)
    gl = jnp.where(is_group, logits, NEG_INF)
    gmax = jnp.max(gl, axis=-1, keepdims=True)
    g_sel = jnp.min(jnp.where(gl == gmax, lane, big), axis=-1, keepdims=True) - G_LANE0
    p_top = 1.0 / jnp.sum(jnp.exp(gl - gmax), axis=-1, keepdims=True)
    group_of_lane = (lane_i >> 2).astype(F32)
    in_group = (lane_i < N_EXPERTS) & (group_of_lane == g_sel)
    el = jnp.where(in_group, logits, NEG_INF)
    v1 = jnp.max(el, axis=-1, keepdims=True)
    i1 = jnp.min(jnp.where(el == v1, lane, big), axis=-1, keepdims=True)
    el2 = jnp.where(lane == i1, NEG_INF, el)
    v2 = jnp.max(el2, axis=-1, keepdims=True)
    i2 = jnp.min(jnp.where(el2 == v2, lane, big), axis=-1, keepdims=True)
    e21 = jnp.exp(v2 - v1)
    w1 = p_top / (1.0 + e21)
    comb_ref[...] = (jnp.where(lane_i == 0, i1, 0.0) + jnp.where(lane_i == 1, i2, 0.0)
                     + jnp.where(lane_i == 2, w1, 0.0) + jnp.where(lane_i == 3, w1 * e21, 0.0))


def _outproj_router(h, oa, ob, w_out, gain, wr_hi, wr_lo, b_r):
    t = h.shape[0]
    row = lambda w: pl.BlockSpec((OUT_TM, w), lambda i: (i, 0))
    full = lambda s: pl.BlockSpec(s, lambda i: (0, 0))
    return pl.pallas_call(
        _outproj_router_kernel,
        grid=(t // OUT_TM,),
        in_specs=[row(D_MODEL), row(MIX_W), row(MIX_W), full((D_MODEL, D_MODEL)), full((1, D_MODEL)),
                  full((D_MODEL, LANES)), full((D_MODEL, LANES)), full((1, LANES))],
        out_specs=[row(D_MODEL), row(D_MODEL), row(LANES)],
        out_shape=[jax.ShapeDtypeStruct((t, D_MODEL), F32), jax.ShapeDtypeStruct((t, D_MODEL), F32),
                   jax.ShapeDtypeStruct((t, LANES), F32)],
        compiler_params=_cparams(("parallel",)),
        name="outproj_router",
    )(h, oa, ob, w_out, gain, wr_hi, wr_lo, b_r)


MOE_TM = 512


def _moe_plan(route, t):
    n_asg = 2 * t
    n_rows = n_asg + N_EXPERTS * MOE_TM
    n_tiles = n_rows // MOE_TM
    expert = jnp.concatenate([route[:, 0], route[:, 1]]).astype(jnp.int32)
    weight = jnp.concatenate([route[:, 2], route[:, 3]])
    onehot = (expert[:, None] == jnp.arange(N_EXPERTS, dtype=jnp.int32)[None, :]).astype(jnp.int32)
    csum = jnp.cumsum(onehot, axis=0)
    rank = jnp.sum(csum * onehot, axis=1) - 1
    padded = ((csum[-1] + MOE_TM - 1) // MOE_TM) * MOE_TM
    ends = jnp.cumsum(padded)
    pos = jnp.sum((ends - padded)[None, :] * onehot, axis=1) + rank
    real = jnp.full((n_rows,), -1, jnp.int32).at[pos].set(jnp.arange(n_asg, dtype=jnp.int32), unique_indices=True)
    is_pad = real < 0
    asg = jnp.where(is_pad, n_asg + jnp.cumsum(is_pad.astype(jnp.int32)) - 1, real)
    w_sorted = jnp.zeros((n_rows,), F32).at[pos].set(weight, unique_indices=True)
    starts = jnp.arange(n_tiles, dtype=jnp.int32) * MOE_TM
    tile_expert = jnp.minimum(jnp.sum((starts[:, None] >= ends[None, :]).astype(jnp.int32), axis=1), N_EXPERTS - 1)
    return tile_expert, asg.reshape(n_tiles, MOE_TM), (ends[-1:] // MOE_TM), w_sorted.reshape(n_rows, 1)


def _moe_kernel(te_ref, asg_ref, nv_ref, x_hbm, ws_ref, wg_ref, wu_ref, wd_ref, out_hbm,
                xbuf, ybuf, sem_g, sem_s, *, n_tok):
    t = pl.program_id(0)
    nt = pl.num_programs(0)
    slot = t % 2

    def row_gather(tile, buf, r):
        tok = lax.rem(asg_ref[tile, r], n_tok)
        return pltpu.make_async_copy(x_hbm.at[pl.ds(tok, 1)], xbuf.at[buf, pl.ds(r, 1)], sem_g.at[buf])

    def row_scatter(tile, buf, r):
        return pltpu.make_async_copy(ybuf.at[buf, pl.ds(r, 1)], out_hbm.at[pl.ds(asg_ref[tile, r], 1)], sem_s.at[buf])

    def start_all(make, tile, buf):
        def body(r, carry):
            make(tile, buf, r).start()
            return carry
        lax.fori_loop(0, MOE_TM, body, 0, unroll=8)

    def wait_gather(buf):
        pltpu.make_async_copy(x_hbm.at[pl.ds(0, MOE_TM)], xbuf.at[buf], sem_g.at[buf]).wait()

    def wait_scatter(buf):
        pltpu.make_async_copy(ybuf.at[buf], out_hbm.at[pl.ds(0, MOE_TM)], sem_s.at[buf]).wait()

    @pl.when(t == 0)
    def _():
        start_all(row_gather, 0, 0)

    @pl.when(t + 1 < nt)
    def _():
        start_all(row_gather, t + 1, 1 - slot)

    wait_gather(slot)

    @pl.when(t < nv_ref[0])
    def _():
        x = xbuf[slot].astype(BF16)
        he = _silu(_dot(x, wg_ref[0])) * _dot(x, wu_ref[0])
        ybuf[slot] = _dot((he * ws_ref[...]).astype(BF16), wd_ref[0])

    @pl.when(t >= nv_ref[0])
    def _():
        ybuf[slot] = jnp.zeros((MOE_TM, D_MODEL), F32)

    start_all(row_scatter, t, slot)

    @pl.when(t >= 1)
    def _():
        wait_scatter(1 - slot)

    @pl.when(t == nt - 1)
    def _():
        wait_scatter(slot)


def _moe(xn, route, wg, wu, wd):
    t = xn.shape[0]
    tile_expert, asg, n_valid, w_sorted = _moe_plan(route, t)
    n_tiles = asg.shape[0]
    grid_spec = pltpu.PrefetchScalarGridSpec(
        num_scalar_prefetch=3,
        grid=(n_tiles,),
        in_specs=[
            pl.BlockSpec(memory_space=pl.ANY),
            pl.BlockSpec((MOE_TM, 1), lambda i, te, asg, nv: (i, 0)),
            pl.BlockSpec((1, D_MODEL, EXPERT_FF), lambda i, te, asg, nv: (te[i], 0, 0)),
            pl.BlockSpec((1, D_MODEL, EXPERT_FF), lambda i, te, asg, nv: (te[i], 0, 0)),
            pl.BlockSpec((1, EXPERT_FF, D_MODEL), lambda i, te, asg, nv: (te[i], 0, 0)),
        ],
        out_specs=pl.BlockSpec(memory_space=pl.ANY),
        scratch_shapes=[pltpu.VMEM((2, MOE_TM, D_MODEL), F32), pltpu.VMEM((2, MOE_TM, D_MODEL), F32),
                        pltpu.SemaphoreType.DMA((2,)), pltpu.SemaphoreType.DMA((2,))],
    )
    return pl.pallas_call(
        functools.partial(_moe_kernel, n_tok=t),
        grid_spec=grid_spec,
        out_shape=jax.ShapeDtypeStruct((asg.size, D_MODEL), F32),
        compiler_params=_cparams(("arbitrary",)),
        name="moe",
    )(tile_expert, asg, n_valid, xn, w_sorted, wg, wu, wd)


PLE_TM = 512


def _ple_kernel(h_ref, y0_ref, y1_ref, p_ref, wg_ref, wp_ref, gn_ref, out_ref, *, final_norm):
    h = h_ref[...] + y0_ref[...] + y1_ref[...]
    gate = _sigmoid(_dot(h.astype(BF16), wg_ref[...]))
    out = h + gate * _dot(p_ref[...].astype(BF16), wp_ref[...])
    if final_norm:
        out = _rms(out, gn_ref[...])
    out_ref[...] = out


def _ple(h, y, p, w_gate, w_proj, gain, final_norm):
    t = h.shape[0]
    slot1 = t // PLE_TM
    return pl.pallas_call(
        functools.partial(_ple_kernel, final_norm=final_norm),
        grid=(t // PLE_TM,),
        in_specs=[
            pl.BlockSpec((PLE_TM, D_MODEL), lambda i: (i, 0)),
            pl.BlockSpec((PLE_TM, D_MODEL), lambda i: (i, 0)),
            pl.BlockSpec((PLE_TM, D_MODEL), lambda i: (slot1 + i, 0)),
            pl.BlockSpec((PLE_TM, PLE_DIM), lambda i: (i, 0)),
            pl.BlockSpec((D_MODEL, D_MODEL), lambda i: (0, 0)),
            pl.BlockSpec((PLE_DIM, D_MODEL), lambda i: (0, 0)),
            pl.BlockSpec((1, D_MODEL), lambda i: (0, 0)),
        ],
        out_specs=pl.BlockSpec((PLE_TM, D_MODEL), lambda i: (i, 0)),
        out_shape=jax.ShapeDtypeStruct((t, D_MODEL), F32),
        compiler_params=_cparams(("parallel",)),
        name="ple",
    )(h, y, y, p, w_gate, w_proj, gain)


def _pad_lanes(x):
    return jnp.pad(x, ((0, 0), (0, LANES - x.shape[-1])))


def _row(x):
    return x.reshape(1, -1).astype(F32)


def kernel(x, p, norm_mix, norm_ffn, norm_final, ab_w_in, ab_w_out, a_lam_q1, a_lam_k1, a_lam_q2, a_lam_k2,
           a_subln, b_conv, b_igate_bias, b_fgate_bias, b_norm, cd_w_in, cd_w_out, c_conv, c_a_log, c_dt_bias,
           c_norm, d_fgate_bias, moe_w_group, moe_b_group, moe_w_router, moe_b_router, moe_w_gate, moe_w_up,
           moe_w_down, ple_w_gate, ple_w_proj):
    batch, seq, _ = x.shape
    depth = p.shape[0]
    t = batch * seq
    assert seq % IN_TM == 0 and seq % ATT_TK == 0 and t % MOE_TM == 0
    h = x.reshape(t, D_MODEL)
    cos, sin = _rope_tables(seq)
    ones_tab = jnp.ones((seq, LANES), F32)

    for i in range(depth):
        j = i // 2
        if i % 2 == 0:
            w_in = ab_w_in[j]
            z, zg = _inproj(h, _row(norm_mix[i]), w_in[:, :Z_MAIN].astype(BF16),
                            _pad_lanes(w_in[:, Z_MAIN:]).astype(BF16), cos, sin, seq, rope_chunks=2)
            lam_init = 0.8 - 0.6 * math.exp(-0.3 * i)
            lam_p = jnp.stack([a_lam_q1[j], a_lam_k1[j], a_lam_q2[j], a_lam_k2[j]]).astype(F32)
            out_1 = _attn_a(z, lam_p, a_subln[j].reshape(HEAD_DIM, 1).astype(F32), batch, seq, lam_init)
            gate_bias = _pad_lanes(jnp.concatenate([b_igate_bias[j], b_fgate_bias[j]]).reshape(1, -1))
            out_2 = _mlstm(z, zg, b_conv[j], gate_bias, _row(b_norm[j]), batch, seq)
            w_out = ab_w_out[j]
        else:
            w_in = cd_w_in[j]
            c_main, d_main = 4 * MIX_W, 3 * MIX_W
            w_main = jnp.concatenate([w_in[:, :c_main], w_in[:, c_main + 8:c_main + 8 + d_main]], axis=1)
            w_small = jnp.concatenate([w_in[:, c_main:c_main + 8], w_in[:, c_main + 8 + d_main:]], axis=1)
            z, zg = _inproj(h, _row(norm_mix[i]), w_main.astype(BF16), _pad_lanes(w_small).astype(BF16),
                            ones_tab, ones_tab, seq, rope_chunks=0)
            params = jnp.stack([_pad_lanes(c_a_log[j].reshape(1, -1))[0], _pad_lanes(c_dt_bias[j].reshape(1, -1))[0]])
            out_1 = _deltanet(z, zg, c_conv[j], params, _row(c_norm[j]), batch, seq)
            fd = zg[:, 8:8 + N_HEADS].reshape(batch, seq, N_HEADS).transpose(0, 2, 1)
            bias_rows = jnp.repeat(jnp.tile(d_fgate_bias[j], batch), seq // LANES).reshape(-1, 1)
            cum = _logf_cumsum(fd.reshape(-1, LANES), jnp.broadcast_to(bias_rows, (bias_rows.shape[0], LANES)),
                               seq // LANES)
            kx, qx = _forget_bias_operands(cum.reshape(batch * N_HEADS, seq))
            out_2 = _attn_d(z, kx, qx, batch, seq)
            w_out = cd_w_out[j]

        w_r = _pad_lanes(jnp.concatenate([moe_w_router[i], moe_w_group[i]], axis=1))
        w_r_hi_f = _bf16_part(w_r)
        w_r_hi = w_r_hi_f.astype(BF16)
        w_r_lo = (w_r - w_r_hi_f).astype(BF16)
        b_r = _pad_lanes(jnp.concatenate([moe_b_router[i], moe_b_group[i]]).reshape(1, -1))
        h, xn, comb = _outproj_router(h, out_1, out_2, w_out.astype(BF16), _row(norm_ffn[i]), w_r_hi, w_r_lo, b_r)
        y = _moe(xn, comb, moe_w_gate[i].astype(BF16), moe_w_up[i].astype(BF16), moe_w_down[i].astype(BF16))
        h = _ple(h, y, p[i].reshape(t, PLE_DIM), ple_w_gate[i].astype(BF16), ple_w_proj[i].astype(BF16),
                 _row(norm_final), final_norm=(i == depth - 1))
    return h.reshape(batch, seq, D_MODEL)
```

```python
import functools
import math

import jax
import jax.numpy as jnp
from jax import lax
from jax.experimental import pallas as pl
from jax.experimental.pallas import tpu as pltpu

F32 = jnp.float32
BF16 = jnp.bfloat16
HIGHEST = lax.Precision.HIGHEST

D_MODEL = 1024
HEAD_DIM = 128
N_HEADS = 4
MIX_W = N_HEADS * HEAD_DIM
A_HALF = HEAD_DIM // 2
CHUNK = 64
CONV_W = 4
RMS_EPS = 1e-6
PLE_DIM = 256
N_GROUPS = 4
EXPERTS_PER_GROUP = 4
N_EXPERTS = 16
EXPERT_FF = D_MODEL // 2
ROPE_THETA = 10000.0
Z_MAIN = 7 * MIX_W
LANES = 128

VMEM_LIMIT = 56 * 1024 * 1024

NEG_INF = float("-inf")


def _cparams(sem):
    return pltpu.CompilerParams(dimension_semantics=sem, vmem_limit_bytes=VMEM_LIMIT)


def _dot(a, b):
    return jnp.dot(a, b, preferred_element_type=F32)


def _dot_nt(a, b):
    return lax.dot_general(a, b, (((1,), (1,)), ((), ())), preferred_element_type=F32)


def _dot_exact(a, b):
    return jnp.dot(a, b, preferred_element_type=F32, precision=HIGHEST)


def _bf16_part(x):
    bits = lax.bitcast_convert_type(x, jnp.int32) & jnp.int32(-65536)
    return lax.bitcast_convert_type(bits, F32)


def _sigmoid(x):
    return 1.0 / (1.0 + jnp.exp(-x))


def _silu(x):
    return x * _sigmoid(x)


def _log_sigmoid(x):
    return jnp.minimum(x, 0.0) - jnp.log(1.0 + jnp.exp(-jnp.abs(x)))


def _softplus(x):
    return jnp.maximum(x, 0.0) + jnp.log(1.0 + jnp.exp(-jnp.abs(x)))


def _rms(x, gain):
    return x * lax.rsqrt(jnp.mean(x * x, axis=-1, keepdims=True) + RMS_EPS) * gain


IN_TM = 512
IN_TN = 512


def _inproj_kernel(h_ref, g_ref, w_ref, wg_ref, cos_ref, sin_ref, z_ref, zg_ref, *, rope_chunks):
    hn = _rms(h_ref[...], g_ref[...])
    hb = hn.astype(BF16)
    zg_ref[...] = _dot(hb, wg_ref[...])
    if rope_chunks:
        lane = lax.broadcasted_iota(jnp.int32, (IN_TM, IN_TN), 1)
        first_half = (lane & 32) == 0
        cos = jnp.concatenate([cos_ref[...]] * (IN_TN // LANES), axis=1)
        sin = jnp.concatenate([sin_ref[...]] * (IN_TN // LANES), axis=1)
    for c in range(Z_MAIN // IN_TN):
        zc = _dot(hb, w_ref[:, c * IN_TN:(c + 1) * IN_TN])
        if c < rope_chunks:
            partner = jnp.where(first_half, pltpu.roll(zc, IN_TN - 32, axis=1), pltpu.roll(zc, 32, axis=1))
            zc = zc * cos + partner * sin
        z_ref[:, c * IN_TN:(c + 1) * IN_TN] = zc.astype(BF16)


def _inproj(h, gain, w_main, w_gate, cos, sin, seq, rope_chunks):
    t = h.shape[0]
    per_seq = seq // IN_TM
    return pl.pallas_call(
        functools.partial(_inproj_kernel, rope_chunks=rope_chunks),
        grid=(t // IN_TM,),
        in_specs=[
            pl.BlockSpec((IN_TM, D_MODEL), lambda i: (i, 0)),
            pl.BlockSpec((1, D_MODEL), lambda i: (0, 0)),
            pl.BlockSpec((D_MODEL, Z_MAIN), lambda i: (0, 0)),
            pl.BlockSpec((D_MODEL, LANES), lambda i: (0, 0)),
            pl.BlockSpec((IN_TM, LANES), lambda i: (i % per_seq, 0)),
            pl.BlockSpec((IN_TM, LANES), lambda i: (i % per_seq, 0)),
        ],
        out_specs=[
            pl.BlockSpec((IN_TM, Z_MAIN), lambda i: (i, 0)),
            pl.BlockSpec((IN_TM, LANES), lambda i: (i, 0)),
        ],
        out_shape=[jax.ShapeDtypeStruct((t, Z_MAIN), BF16), jax.ShapeDtypeStruct((t, LANES), F32)],
        compiler_params=_cparams(("parallel",)),
        name="inproj",
    )(h, gain, w_main, w_gate, cos, sin)


def _rope_tables(seq):
    inv = ROPE_THETA ** (-jnp.arange(0, A_HALF, 2, dtype=F32) / A_HALF)
    ang = jnp.arange(seq, dtype=F32)[:, None] * inv[None, :]
    cos, sin = jnp.cos(ang), jnp.sin(ang)
    return jnp.tile(cos, (1, 4)), jnp.tile(jnp.concatenate([-sin, sin], axis=1), (1, 2))


ATT_W = 512
ATT_TK = 512
LOG2E = 1.4426950408889634


def _attn_init(v_ref, vt_ref, m_ref, l_ref, acc_ref):
    @pl.when(pl.program_id(2) == 0)
    def _():
        def body(j, carry):
            off = pl.multiple_of(j * ATT_TK, ATT_TK)
            vt_ref[j] = v_ref[pl.ds(off, ATT_TK), :].astype(F32).T.astype(BF16)
            return carry
        lax.fori_loop(0, vt_ref.shape[0], body, 0)

    m_ref[...] = jnp.full(m_ref.shape, NEG_INF, F32)
    l_ref[...] = jnp.zeros(l_ref.shape, F32)
    acc_ref[...] = jnp.zeros(acc_ref.shape, F32)


def _attn_logits(k_ref, q_t, s_ref, blk, mask):
    off = pl.multiple_of(blk * ATT_TK, ATT_TK)
    s = _dot(k_ref[pl.ds(off, ATT_TK), :], q_t)
    if mask is not None:
        s = jnp.where(mask, s, NEG_INF)
    s_ref[...] = s


def _attn_consume(s_ref, vt_ref, blk, m_ref, l_ref, acc_ref):
    s = s_ref[...]
    m_prev = m_ref[...]
    m_new = jnp.maximum(m_prev, jnp.max(s, axis=0, keepdims=True))
    alpha = jnp.exp2(m_prev - m_new)
    p = jnp.exp2(s - m_new)
    l_ref[...] = alpha * l_ref[...] + jnp.sum(p, axis=0, keepdims=True)
    acc_ref[...] = alpha * acc_ref[...] + _dot(vt_ref[blk], p.astype(BF16))
    m_ref[...] = m_new


def _attn_pipeline(k_ref, vt_ref, q_t, s0_ref, s1_ref, m_ref, l_ref, acc_ref, n_full, diag_blk, diag_mask):
    def consume(s_ref, blk):
        _attn_consume(s_ref, vt_ref, blk, m_ref, l_ref, acc_ref)

    _attn_logits(k_ref, q_t, s0_ref, diag_blk, diag_mask)

    def body(t, carry):
        _attn_logits(k_ref, q_t, s1_ref, 2 * t, None)
        consume(s0_ref, jnp.where(t == 0, diag_blk, 2 * t - 1))
        _attn_logits(k_ref, q_t, s0_ref, 2 * t + 1, None)
        consume(s1_ref, 2 * t)
        return carry

    npairs = n_full // 2
    lax.fori_loop(0, npairs, body, 0)
    in_s0 = jnp.where(npairs == 0, diag_blk, 2 * npairs - 1)

    @pl.when(n_full % 2 == 1)
    def _():
        _attn_logits(k_ref, q_t, s1_ref, n_full - 1, None)
        consume(s0_ref, in_s0)
        consume(s1_ref, n_full - 1)

    @pl.when(n_full % 2 == 0)
    def _():
        consume(s0_ref, in_s0)


def _attn_a_kernel(q_ref, k_ref, v_ref, lam_ref, gain_ref, o_ref, vt_ref, s0_ref, s1_ref, m_ref, l_ref, acc_ref,
                   *, lam_init):
    _attn_init(v_ref, vt_ref, m_ref, l_ref, acc_ref)
    i = pl.program_id(2)
    tq = ATT_W
    q = (q_ref[...].astype(F32) * (A_HALF ** -0.5 * LOG2E)).T
    dim = lax.broadcasted_iota(jnp.int32, q.shape, 0)
    q_t = jnp.concatenate([jnp.where(dim < A_HALF, q, 0.0), jnp.where(dim >= A_HALF, q, 0.0)], axis=1).astype(BF16)
    key = lax.broadcasted_iota(jnp.int32, (ATT_TK, 2 * tq), 0)
    qry = lax.broadcasted_iota(jnp.int32, (ATT_TK, 2 * tq), 1) & (tq - 1)
    _attn_pipeline(k_ref, vt_ref, q_t, s0_ref, s1_ref, m_ref, l_ref, acc_ref, i, i, (key >> 6) <= (qry >> 6))

    lam_p = lam_ref[...]
    lam = (jnp.exp(jnp.sum(lam_p[0:1] * lam_p[1:2], axis=-1, keepdims=True))
           - jnp.exp(jnp.sum(lam_p[2:3] * lam_p[3:4], axis=-1, keepdims=True)) + lam_init)
    o_all = acc_ref[...] * (1.0 / l_ref[...])
    out = o_all[:, 0:tq] - lam * o_all[:, tq:2 * tq]
    out = out * lax.rsqrt(jnp.mean(out * out, axis=0, keepdims=True) + RMS_EPS) * (gain_ref[...] * (1.0 - lam_init))
    o_ref[...] = out.T.astype(BF16)


def _attn_scratch(seq, cols):
    return [pltpu.VMEM((seq // ATT_TK, HEAD_DIM, ATT_TK), BF16), pltpu.VMEM((ATT_TK, cols), F32),
            pltpu.VMEM((ATT_TK, cols), F32), pltpu.VMEM((1, cols), F32), pltpu.VMEM((1, cols), F32),
            pltpu.VMEM((HEAD_DIM, cols), F32)]


def _attn_a(z, lam_p, gain_col, batch, seq, lam_init):
    t = batch * seq
    nq = seq // ATT_W
    return pl.pallas_call(
        functools.partial(_attn_a_kernel, lam_init=lam_init),
        grid=(batch, N_HEADS, nq),
        in_specs=[
            pl.BlockSpec((ATT_W, HEAD_DIM), lambda b, h, i: (b * nq + i, h)),
            pl.BlockSpec((seq, HEAD_DIM), lambda b, h, i: (b, N_HEADS + h)),
            pl.BlockSpec((seq, HEAD_DIM), lambda b, h, i: (b, 2 * N_HEADS + h)),
            pl.BlockSpec((4, A_HALF), lambda b, h, i: (0, 0)),
            pl.BlockSpec((HEAD_DIM, 1), lambda b, h, i: (0, 0)),
        ],
        out_specs=pl.BlockSpec((ATT_W, HEAD_DIM), lambda b, h, i: (b * nq + i, h)),
        out_shape=jax.ShapeDtypeStruct((t, MIX_W), BF16),
        scratch_shapes=_attn_scratch(seq, 2 * ATT_W),
        compiler_params=_cparams(("parallel", "parallel", "arbitrary")),
        name="attn_a",
    )(z, z, z, lam_p, gain_col)


D_AUG = 2 * HEAD_DIM
BIAS_ROWS = 16


def _attn_d_kernel(q_ref, k_ref, v_ref, kx_ref, qx_ref, o_ref, kaug_ref, vt_ref, s0_ref, s1_ref, m_ref, l_ref,
                   acc_ref):
    _attn_init(v_ref, vt_ref, m_ref, l_ref, acc_ref)
    i = pl.program_id(2)

    @pl.when(i == 0)
    def _():
        def body(j, carry):
            off = pl.multiple_of(j * ATT_TK, ATT_TK)
            kaug_ref[pl.ds(off, ATT_TK), 0:HEAD_DIM] = k_ref[pl.ds(off, ATT_TK), :]
            kaug_ref[pl.ds(off, ATT_TK), HEAD_DIM:D_AUG] = kx_ref[0, pl.ds(off, ATT_TK), :]
            return carry
        lax.fori_loop(0, vt_ref.shape[0], body, 0)

    q = (q_ref[...].astype(F32) * (HEAD_DIM ** -0.5 * LOG2E)).T.astype(BF16)
    q_t = jnp.concatenate([q, qx_ref[0], jnp.zeros((D_AUG - HEAD_DIM - BIAS_ROWS, ATT_W), BF16)], axis=0)
    key = lax.broadcasted_iota(jnp.int32, (ATT_TK, ATT_W), 0)
    qry = lax.broadcasted_iota(jnp.int32, (ATT_TK, ATT_W), 1)
    _attn_pipeline(kaug_ref, vt_ref, q_t, s0_ref, s1_ref, m_ref, l_ref, acc_ref, i, i, key <= qry)
    o_ref[...] = (acc_ref[...] * (1.0 / l_ref[...])).T.astype(BF16)


def _attn_d(z, kx, qx, batch, seq):
    t = batch * seq
    nq = seq // ATT_W
    base = 4 * N_HEADS
    return pl.pallas_call(
        _attn_d_kernel,
        grid=(batch, N_HEADS, nq),
        in_specs=[
            pl.BlockSpec((ATT_W, HEAD_DIM), lambda b, h, i: (b * nq + i, base + h)),
            pl.BlockSpec((seq, HEAD_DIM), lambda b, h, i: (b, base + N_HEADS + h)),
            pl.BlockSpec((seq, HEAD_DIM), lambda b, h, i: (b, base + 2 * N_HEADS + h)),
            pl.BlockSpec((1, seq, HEAD_DIM), lambda b, h, i: (b * N_HEADS + h, 0, 0)),
            pl.BlockSpec((1, BIAS_ROWS, ATT_W), lambda b, h, i: (b * N_HEADS + h, 0, i)),
        ],
        out_specs=pl.BlockSpec((ATT_W, HEAD_DIM), lambda b, h, i: (b * nq + i, h)),
        out_shape=jax.ShapeDtypeStruct((t, MIX_W), BF16),
        scratch_shapes=[pltpu.VMEM((seq, D_AUG), BF16)] + _attn_scratch(seq, ATT_W),
        compiler_params=_cparams(("parallel", "parallel", "arbitrary")),
        name="attn_d",
    )(z, z, z, kx, qx)


def _forget_bias_operands(cum):
    c = cum * LOG2E
    hi_f = _bf16_part(c)
    mid_f = _bf16_part(c - hi_f)
    hi, mid, lo = hi_f.astype(BF16), mid_f.astype(BF16), (c - hi_f - mid_f).astype(BF16)
    one = jnp.ones_like(hi)
    kx = jnp.stack([hi, mid, lo, one, one, one], axis=-1)
    kx = jnp.pad(kx, ((0, 0), (0, 0), (0, HEAD_DIM - 6)))
    qx = jnp.stack([-one, -one, -one, hi, mid, lo], axis=1)
    qx = jnp.pad(qx, ((0, 0), (0, BIAS_ROWS - 6), (0, 0)))
    return kx, qx


def _logf_cumsum_kernel(x_ref, b_ref, o_ref, *, rows_per_seq):
    r = x_ref.shape[0]
    lf = _log_sigmoid(x_ref[...] + b_ref[...])
    a = lax.broadcasted_iota(jnp.int32, (LANES, LANES), 0)
    b = lax.broadcasted_iota(jnp.int32, (LANES, LANES), 1)
    within = _dot_exact(lf, (a <= b).astype(F32))
    totals = _dot_exact(lf, jnp.ones((LANES, LANES), F32))
    ra = lax.broadcasted_iota(jnp.int32, (r, r), 0)
    rb = lax.broadcasted_iota(jnp.int32, (r, r), 1)
    earlier = ((rb < ra) & ((rb // rows_per_seq) == (ra // rows_per_seq))).astype(F32)
    o_ref[...] = within + _dot_exact(earlier, totals)


def _logf_cumsum(x, bias_rows, rows_per_seq):
    r = x.shape[0]
    return pl.pallas_call(
        functools.partial(_logf_cumsum_kernel, rows_per_seq=rows_per_seq),
        out_shape=jax.ShapeDtypeStruct((r, LANES), F32),
        compiler_params=pltpu.CompilerParams(vmem_limit_bytes=VMEM_LIMIT),
        name="logf_cumsum",
    )(x, bias_rows)


def _causal_conv_silu(x, prev, w):
    row = lax.broadcasted_iota(jnp.int32, x.shape, 0)
    acc = x * w[CONV_W - 1:CONV_W, :]
    for s in range(1, CONV_W):
        xs = jnp.where(row >= s, pltpu.roll(x, s, axis=0), pltpu.roll(prev, s, axis=0))
        acc = acc + xs * w[CONV_W - 1 - s:CONV_W - s, :]
    return _silu(acc)


def _chunk_masks():
    r = lax.broadcasted_iota(jnp.int32, (CHUNK, CHUNK), 0)
    c = lax.broadcasted_iota(jnp.int32, (CHUNK, CHUNK), 1)
    return r, c


REC_NB = 2


def _rec_specs(nb, cols):
    main = [pl.BlockSpec((nb, CHUNK, MIX_W), lambda b, c, j=j: (b, c, j)) for j in cols]
    return main + [pl.BlockSpec((nb, CHUNK, LANES), lambda b, c: (b, c, 0))]


def _mlstm_kernel(q_ref, k_ref, v_ref, og_ref, g_ref, conv_ref, bias_ref, norm_ref, out_ref,
                  pq_ref, pk_ref, c_ref, n_ref, m_ref):
    @pl.when(pl.program_id(1) == 0)
    def _():
        pq_ref[...] = jnp.zeros(pq_ref.shape, F32)
        pk_ref[...] = jnp.zeros(pk_ref.shape, F32)
        c_ref[...] = jnp.zeros(c_ref.shape, F32)
        n_ref[...] = jnp.zeros(n_ref.shape, F32)
        m_ref[...] = jnp.zeros(m_ref.shape, F32)

    lane = lax.broadcasted_iota(jnp.int32, (CHUNK, LANES), 1)
    r, c = _chunk_masks()
    causal = c <= r
    tri = causal.astype(F32)
    units = []
    for bi in range(q_ref.shape[0]):
        xq = q_ref[bi].astype(F32)
        xk = k_ref[bi].astype(F32)
        q_all = _causal_conv_silu(xq, pq_ref[bi], conv_ref[:, 0:MIX_W])
        k_all = _causal_conv_silu(xk, pk_ref[bi], conv_ref[:, MIX_W:2 * MIX_W]) * (HEAD_DIM ** -0.5)
        pq_ref[bi] = xq
        pk_ref[bi] = xk
        pre = g_ref[bi] + bias_ref[...]
        e = jnp.where(lane < N_HEADS, pre, _log_sigmoid(pre))
        cum = _dot_exact(tri, e)
        e_t = e.T
        cum_t = cum.T
        for h in range(N_HEADS):
            sl = slice(h * HEAD_DIM, (h + 1) * HEAD_DIM)
            u = dict(bi=bi, h=h, sl=sl, q=q_all[:, sl], k=k_all[:, sl], v=v_ref[bi, :, sl].astype(F32))
            u["qb"] = u["q"].astype(BF16)
            u["kb"] = u["k"].astype(BF16)
            i_col = e[:, h:h + 1]
            i_row = e_t[h:h + 1, :]
            b_col = cum[:, N_HEADS + h:N_HEADS + h + 1]
            b_row = cum_t[N_HEADS + h:N_HEADS + h + 1, :]
            b_last = b_col[CHUNK - 1:CHUNK, :]
            m_st = m_ref[bi, h:h + 1, 0:1]
            u["c_st"] = c_ref[bi, h]
            u["n_st"] = n_ref[bi, h:h + 1, :]
            d_log = jnp.where(causal, b_col - b_row + i_row, NEG_INF)
            inter_log = b_col + m_st
            u["m_t"] = jnp.maximum(inter_log, jnp.max(d_log, axis=-1, keepdims=True))
            u["d_w"] = jnp.exp(d_log - u["m_t"])
            u["inter_w"] = jnp.exp(inter_log - u["m_t"])
            w_log = b_last - b_col + i_col
            u["m_new"] = jnp.maximum(b_last + m_st, jnp.max(w_log, axis=0, keepdims=True))
            u["sw"] = jnp.exp(w_log - u["m_new"])
            u["decay"] = jnp.exp(b_last + m_st - u["m_new"])
            units.append(u)

    for u in units:
        u["qk"] = _dot_nt(u["qb"], u["kb"])
    for u in units:
        u["qc"] = _dot_nt(u["qb"], u["c_st"].astype(BF16))
    for u in units:
        u["upd"] = _dot((u["v"] * u["sw"]).T.astype(BF16), u["kb"])
    for u in units:
        u["a"] = u["qk"] * u["d_w"]
        u["av"] = _dot(u["a"].astype(BF16), u["v"].astype(BF16))
    for u in units:
        bi, h, sl = u["bi"], u["h"], u["sl"]
        num = u["av"] + u["inter_w"] * u["qc"]
        den = (jnp.sum(u["a"], axis=-1, keepdims=True)
               + u["inter_w"] * jnp.sum(u["q"] * u["n_st"], axis=-1, keepdims=True))
        hh = num / jnp.maximum(jnp.abs(den), jnp.exp(-u["m_t"]))
        c_ref[bi, h] = u["decay"] * u["c_st"] + u["upd"]
        n_ref[bi, h:h + 1, :] = u["decay"] * u["n_st"] + jnp.sum(u["k"] * u["sw"], axis=0, keepdims=True)
        m_ref[bi, h:h + 1, :] = jnp.broadcast_to(u["m_new"], (1, LANES))
        gate = _sigmoid(og_ref[bi, :, sl].astype(F32))
        out_ref[bi, :, sl] = (_rms(hh, norm_ref[...]) * gate).astype(BF16)


def _mlstm(z, zg, conv_w, gate_bias, norm, batch, seq):
    nb = REC_NB if batch % REC_NB == 0 else 1
    full = lambda s: pl.BlockSpec(s, lambda b, c: (0, 0))
    out = pl.pallas_call(
        _mlstm_kernel,
        grid=(batch // nb, seq // CHUNK),
        in_specs=_rec_specs(nb, (3, 4, 5, 6)) + [full((CONV_W, 2 * MIX_W)), full((1, LANES)), full((1, HEAD_DIM))],
        out_specs=pl.BlockSpec((nb, CHUNK, MIX_W), lambda b, c: (b, c, 0)),
        out_shape=jax.ShapeDtypeStruct((batch, seq, MIX_W), BF16),
        scratch_shapes=[pltpu.VMEM((nb, CHUNK, MIX_W), F32), pltpu.VMEM((nb, CHUNK, MIX_W), F32),
                        pltpu.VMEM((nb, N_HEADS, HEAD_DIM, HEAD_DIM), F32), pltpu.VMEM((nb, 8, HEAD_DIM), F32),
                        pltpu.VMEM((nb, 8, LANES), F32)],
        compiler_params=_cparams(("parallel", "arbitrary")),
        name="mlstm",
    )(*([z.reshape(batch, seq, Z_MAIN)] * 4), zg.reshape(batch, seq, LANES), conv_w, gate_bias, norm)
    return out.reshape(batch * seq, MIX_W)


def _bdot(a, b):
    return _dot(a.astype(BF16), b.astype(BF16))


def _unit_lower_inverses(l_stricts, r, c):
    same16 = (r >> 4) == (c >> 4)
    same32 = (r >> 5) == (c >> 5)
    eye = (r == c).astype(F32)
    ps = [jnp.where(same16, -l, 0.0) for l in l_stricts]
    invs = [eye + p for p in ps]
    for _ in range(3):
        ps = [_bdot(p, p) for p in ps]
        invs = [inv + _bdot(inv, p) for inv, p in zip(invs, ps)]
    for off_mask in (same32 & jnp.logical_not(same16), jnp.logical_not(same32)):
        ts = [_bdot(jnp.where(off_mask, l, 0.0), inv) for l, inv in zip(l_stricts, invs)]
        invs = [inv - _bdot(inv, t) for inv, t in zip(invs, ts)]
    return invs


def _deltanet_kernel(q_ref, k_ref, v_ref, og_ref, g_ref, conv_ref, par_ref, norm_ref, out_ref,
                     pq_ref, pk_ref, pv_ref, s_ref):
    @pl.when(pl.program_id(1) == 0)
    def _():
        pq_ref[...] = jnp.zeros(pq_ref.shape, F32)
        pk_ref[...] = jnp.zeros(pk_ref.shape, F32)
        pv_ref[...] = jnp.zeros(pv_ref.shape, F32)
        s_ref[...] = jnp.zeros(s_ref.shape, F32)

    lane = lax.broadcasted_iota(jnp.int32, (CHUNK, LANES), 1)
    r, c = _chunk_masks()
    tri = (c <= r).astype(F32)
    units = []
    for bi in range(q_ref.shape[0]):
        xq = q_ref[bi].astype(F32)
        xk = k_ref[bi].astype(F32)
        xv = v_ref[bi].astype(F32)
        q_all = _causal_conv_silu(xq, pq_ref[bi], conv_ref[:, 0:MIX_W])
        k_all = _causal_conv_silu(xk, pk_ref[bi], conv_ref[:, MIX_W:2 * MIX_W])
        v_all = _causal_conv_silu(xv, pv_ref[bi], conv_ref[:, 2 * MIX_W:3 * MIX_W])
        pq_ref[bi] = xq
        pk_ref[bi] = xk
        pv_ref[bi] = xv
        pre = g_ref[bi]
        e = jnp.where(lane < N_HEADS, -jnp.exp(par_ref[0:1, :]) * _softplus(pre + par_ref[1:2, :]), _sigmoid(pre))
        cum = _dot_exact(tri, e)
        cum_t = cum.T
        for h in range(N_HEADS):
            sl = slice(h * HEAD_DIM, (h + 1) * HEAD_DIM)
            q = q_all[:, sl]
            k = k_all[:, sl]
            q = q * lax.rsqrt(jnp.sum(q * q, axis=-1, keepdims=True) + RMS_EPS) * (HEAD_DIM ** -0.5)
            k = k * lax.rsqrt(jnp.sum(k * k, axis=-1, keepdims=True) + RMS_EPS)
            beta = e[:, N_HEADS + h:N_HEADS + h + 1]
            b_col = cum[:, h:h + 1]
            b_row = cum_t[h:h + 1, :]
            b_last = b_col[CHUNK - 1:CHUNK, :]
            e_col = jnp.exp(b_col)
            kbeta = k * beta
            units.append(dict(
                bi=bi, h=h, sl=sl, k=k, kbeta=kbeta, q_dec=q * e_col,
                gam=jnp.exp(jnp.where(c <= r, b_col - b_row, NEG_INF)),
                rhs=jnp.concatenate([v_all[:, sl] * beta, kbeta * e_col], axis=1),
                k_dec=k * jnp.exp(b_last - b_col), chunk_decay=jnp.exp(b_last),
                kq=jnp.concatenate([kbeta, q], axis=0)))

    for u in units:
        both = _dot_nt(u["kq"].astype(BF16), u["k"].astype(BF16)) * jnp.concatenate([u["gam"], u["gam"]], axis=0)
        u["l"] = jnp.where(c < r, both[0:CHUNK], 0.0)
        u["qk"] = both[CHUNK:2 * CHUNK]
    invs = _unit_lower_inverses([u["l"] for u in units], r, c)
    for u, inv in zip(units, invs):
        sol = _bdot(inv, u["rhs"])
        u["u"] = sol[:, 0:HEAD_DIM]
        u["w"] = sol[:, HEAD_DIM:2 * HEAD_DIM]
    for u in units:
        u["st"] = s_ref[u["bi"], u["h"]]
        u["ws"] = _bdot(jnp.concatenate([u["w"], u["q_dec"]], axis=0), u["st"])
    for u in units:
        u["v_new"] = u["u"] - u["ws"][0:CHUNK]
        u["o"] = u["ws"][CHUNK:2 * CHUNK] + _bdot(u["qk"], u["v_new"])
    for u in units:
        s_ref[u["bi"], u["h"]] = u["chunk_decay"] * u["st"] + _bdot(u["k_dec"].T, u["v_new"])
    for u in units:
        bi, sl = u["bi"], u["sl"]
        gate = _silu(og_ref[bi, :, sl].astype(F32))
        out_ref[bi, :, sl] = (_rms(u["o"], norm_ref[...]) * gate).astype(BF16)


def _deltanet(z, zg, conv_w, params, norm, batch, seq):
    nb = REC_NB if batch % REC_NB == 0 else 1
    full = lambda s: pl.BlockSpec(s, lambda b, c: (0, 0))
    out = pl.pallas_call(
        _deltanet_kernel,
        grid=(batch // nb, seq // CHUNK),
        in_specs=_rec_specs(nb, (0, 1, 2, 3)) + [full((CONV_W, 3 * MIX_W)), full((2, LANES)), full((1, HEAD_DIM))],
        out_specs=pl.BlockSpec((nb, CHUNK, MIX_W), lambda b, c: (b, c, 0)),
        out_shape=jax.ShapeDtypeStruct((batch, seq, MIX_W), BF16),
        scratch_shapes=[pltpu.VMEM((nb, CHUNK, MIX_W), F32)] * 3
        + [pltpu.VMEM((nb, N_HEADS, HEAD_DIM, HEAD_DIM), F32)],
        compiler_params=_cparams(("parallel", "arbitrary")),
        name="deltanet",
    )(*([z.reshape(batch, seq, Z_MAIN)] * 4), zg.reshape(batch, seq, LANES), conv_w, params, norm)
    return out.reshape(batch * seq, MIX_W)


OUT_TM = 512
G_LANE0 = N_EXPERTS
ROUTE_W_LANE = 4
ROW_TILES = D_MODEL // LANES


def _outproj_router_kernel(h_ref, a_ref, b_ref, wo_ref, gn_ref, whi_ref, wlo_ref, br_ref,
                           hout_ref, xn_ref, comb_ref):
    h_new = h_ref[...] + _dot(a_ref[...], wo_ref[0:MIX_W, :]) + _dot(b_ref[...], wo_ref[MIX_W:2 * MIX_W, :])
    hout_ref[...] = h_new
    hn = _rms(h_new, gn_ref[...])
    for c in range(ROW_TILES):
        xn_ref[:, c, :] = hn[:, c * LANES:(c + 1) * LANES]
    hi_f = _bf16_part(hn)
    hi = hi_f.astype(BF16)
    lo = (hn - hi_f).astype(BF16)
    logits = (_dot(hi, whi_ref[...]) + _dot(lo, whi_ref[...]) + _dot(hi, wlo_ref[...]) + _dot(lo, wlo_ref[...])
              + br_ref[...])

    lane_i = lax.broadcasted_iota(jnp.int32, logits.shape, 1)
    lane = lane_i.astype(F32)
    big = 1e9
    is_group = (lane_i >= G_LANE0) & (lane_i < G_LANE0 + N_GROUPS)
    gl = jnp.where(is_group, logits, NEG_INF)
    gmax = jnp.max(gl, axis=-1, keepdims=True)
    g_sel = jnp.min(jnp.where(gl == gmax, lane, big), axis=-1, keepdims=True) - G_LANE0
    p_top = 1.0 / jnp.sum(jnp.exp(gl - gmax), axis=-1, keepdims=True)
    group_of_lane = (lane_i >> 2).astype(F32)
    in_group = (lane_i < N_EXPERTS) & (group_of_lane == g_sel)
    el = jnp.where(in_group, logits, NEG_INF)
    v1 = jnp.max(el, axis=-1, keepdims=True)
    i1 = jnp.min(jnp.where(el == v1, lane, big), axis=-1, keepdims=True)
    el2 = jnp.where(lane == i1, NEG_INF, el)
    v2 = jnp.max(el2, axis=-1, keepdims=True)
    i2 = jnp.min(jnp.where(el2 == v2, lane, big), axis=-1, keepdims=True)
    e21 = jnp.exp(v2 - v1)
    w1 = p_top / (1.0 + e21)
    local = lane - ROUTE_W_LANE + EXPERTS_PER_GROUP * g_sel
    comb_ref[...] = (jnp.where(lane_i == 0, g_sel, 0.0) + jnp.where(local == i1, w1, 0.0)
                     + jnp.where(local == i2, w1 * e21, 0.0))


def _outproj_router(h, oa, ob, w_out, gain, wr_hi, wr_lo, b_r):
    t = h.shape[0]
    row = lambda w: pl.BlockSpec((OUT_TM, w), lambda i: (i, 0))
    full = lambda s: pl.BlockSpec(s, lambda i: (0, 0))
    return pl.pallas_call(
        _outproj_router_kernel,
        grid=(t // OUT_TM,),
        in_specs=[row(D_MODEL), row(MIX_W), row(MIX_W), full((D_MODEL, D_MODEL)), full((1, D_MODEL)),
                  full((D_MODEL, LANES)), full((D_MODEL, LANES)), full((1, LANES))],
        out_specs=[row(D_MODEL), pl.BlockSpec((OUT_TM, ROW_TILES, LANES), lambda i: (i, 0, 0)), row(LANES)],
        out_shape=[jax.ShapeDtypeStruct((t, D_MODEL), F32), jax.ShapeDtypeStruct((t, ROW_TILES, LANES), F32),
                   jax.ShapeDtypeStruct((t, LANES), F32)],
        compiler_params=_cparams(("parallel",)),
        name="outproj_router",
    )(h, oa, ob, w_out, gain, wr_hi, wr_lo, b_r)


MOE_TM = 512


def _moe_plan(route, t):
    n_rows = t + N_GROUPS * MOE_TM
    n_tiles = n_rows // MOE_TM
    group = route[:, 0].astype(jnp.int32)
    onehot = (group[:, None] == jnp.arange(N_GROUPS, dtype=jnp.int32)[None, :]).astype(jnp.int32)
    csum = jnp.cumsum(onehot, axis=0)
    rank = jnp.sum(csum * onehot, axis=1) - 1
    padded = ((csum[-1] + MOE_TM - 1) // MOE_TM) * MOE_TM
    ends = jnp.cumsum(padded)
    pos = jnp.sum((ends - padded)[None, :] * onehot, axis=1) + rank
    real = jnp.full((n_rows,), -1, jnp.int32).at[pos].set(jnp.arange(t, dtype=jnp.int32), unique_indices=True)
    is_pad = real < 0
    dst = jnp.where(is_pad, t + jnp.cumsum(is_pad.astype(jnp.int32)) - 1, real)
    w_sorted = jnp.take(route[:, ROUTE_W_LANE:ROUTE_W_LANE + EXPERTS_PER_GROUP], jnp.where(is_pad, 0, real), axis=0)
    starts = jnp.arange(n_tiles, dtype=jnp.int32) * MOE_TM
    tile_group = jnp.minimum(jnp.sum((starts[:, None] >= ends[None, :]).astype(jnp.int32), axis=1), N_GROUPS - 1)
    return tile_group, dst.reshape(n_tiles, MOE_TM), (ends[-1:] // MOE_TM), w_sorted


def _moe_kernel(tg_ref, dst_ref, nv_ref, x_hbm, ws_ref, wg_ref, wu_ref, wd_ref, out_hbm,
                xbuf, ybuf, sem_g, sem_s, *, n_tok):
    t = pl.program_id(0)
    nt = pl.num_programs(0)
    slot = t % 2

    def row_gather(tile, buf, r):
        d = dst_ref[tile, r]
        tok = jnp.where(d < n_tok, d, 0)
        return pltpu.make_async_copy(x_hbm.at[tok], xbuf.at[buf, r], sem_g.at[buf])

    def row_scatter(tile, buf, r):
        return pltpu.make_async_copy(ybuf.at[buf, r], out_hbm.at[dst_ref[tile, r]], sem_s.at[buf])

    def start_all(make, tile, buf):
        def body(r, carry):
            make(tile, buf, r).start()
            return carry
        lax.fori_loop(0, MOE_TM, body, 0, unroll=8)

    def wait_gather(buf):
        pltpu.make_async_copy(x_hbm.at[pl.ds(0, MOE_TM)], xbuf.at[buf], sem_g.at[buf]).wait()

    def wait_scatter(buf):
        pltpu.make_async_copy(ybuf.at[buf], out_hbm.at[pl.ds(0, MOE_TM)], sem_s.at[buf]).wait()

    @pl.when(t == 0)
    def _():
        start_all(row_gather, 0, 0)

    @pl.when(t + 1 < nt)
    def _():
        start_all(row_gather, t + 1, 1 - slot)

    wait_gather(slot)

    @pl.when(t < nv_ref[0])
    def _():
        x = jnp.concatenate([xbuf[slot, :, c, :] for c in range(ROW_TILES)], axis=1).astype(BF16)
        ws = ws_ref[...]
        y = jnp.zeros((MOE_TM, D_MODEL), F32)
        for e in range(EXPERTS_PER_GROUP):
            he = _silu(_dot(x, wg_ref[e])) * _dot(x, wu_ref[e])
            y = y + _dot((he * ws[:, e:e + 1]).astype(BF16), wd_ref[e])
        for c in range(ROW_TILES):
            ybuf[slot, :, c, :] = y[:, c * LANES:(c + 1) * LANES]

    @pl.when(t >= nv_ref[0])
    def _():
        ybuf[slot] = jnp.zeros(ybuf.shape[1:], F32)

    start_all(row_scatter, t, slot)

    @pl.when(t >= 1)
    def _():
        wait_scatter(1 - slot)

    @pl.when(t == nt - 1)
    def _():
        wait_scatter(slot)


def _moe(xn, route, wg, wu, wd):
    t = xn.shape[0]
    tile_group, dst, n_valid, w_sorted = _moe_plan(route, t)
    n_tiles = dst.shape[0]
    buf = pltpu.VMEM((2, MOE_TM, ROW_TILES, LANES), F32)
    grid_spec = pltpu.PrefetchScalarGridSpec(
        num_scalar_prefetch=3,
        grid=(n_tiles,),
        in_specs=[
            pl.BlockSpec(memory_space=pl.ANY),
            pl.BlockSpec((MOE_TM, EXPERTS_PER_GROUP), lambda i, tg, dst, nv: (i, 0)),
            pl.BlockSpec((EXPERTS_PER_GROUP, D_MODEL, EXPERT_FF), lambda i, tg, dst, nv: (tg[i], 0, 0)),
            pl.BlockSpec((EXPERTS_PER_GROUP, D_MODEL, EXPERT_FF), lambda i, tg, dst, nv: (tg[i], 0, 0)),
            pl.BlockSpec((EXPERTS_PER_GROUP, EXPERT_FF, D_MODEL), lambda i, tg, dst, nv: (tg[i], 0, 0)),
        ],
        out_specs=pl.BlockSpec(memory_space=pl.ANY),
        scratch_shapes=[buf, buf, pltpu.SemaphoreType.DMA((2,)), pltpu.SemaphoreType.DMA((2,))],
    )
    return pl.pallas_call(
        functools.partial(_moe_kernel, n_tok=t),
        grid_spec=grid_spec,
        out_shape=jax.ShapeDtypeStruct((dst.size, ROW_TILES, LANES), F32),
        compiler_params=_cparams(("arbitrary",)),
        name="moe",
    )(tile_group, dst, n_valid, xn, w_sorted, wg, wu, wd)


PLE_TM = 512


def _ple_kernel(h_ref, y_ref, p_ref, wg_ref, wp_ref, gn_ref, out_ref, *, final_norm):
    h = h_ref[...] + jnp.concatenate([y_ref[:, c, :] for c in range(ROW_TILES)], axis=1)
    gate = _sigmoid(_dot(h.astype(BF16), wg_ref[...]))
    out = h + gate * _dot(p_ref[...].astype(BF16), wp_ref[...])
    if final_norm:
        out = _rms(out, gn_ref[...])
    out_ref[...] = out


def _ple(h, y, p, w_gate, w_proj, gain, final_norm):
    t = h.shape[0]
    return pl.pallas_call(
        functools.partial(_ple_kernel, final_norm=final_norm),
        grid=(t // PLE_TM,),
        in_specs=[
            pl.BlockSpec((PLE_TM, D_MODEL), lambda i: (i, 0)),
            pl.BlockSpec((PLE_TM, ROW_TILES, LANES), lambda i: (i, 0, 0)),
            pl.BlockSpec((PLE_TM, PLE_DIM), lambda i: (i, 0)),
            pl.BlockSpec((D_MODEL, D_MODEL), lambda i: (0, 0)),
            pl.BlockSpec((PLE_DIM, D_MODEL), lambda i: (0, 0)),
            pl.BlockSpec((1, D_MODEL), lambda i: (0, 0)),
        ],
        out_specs=pl.BlockSpec((PLE_TM, D_MODEL), lambda i: (i, 0)),
        out_shape=jax.ShapeDtypeStruct((t, D_MODEL), F32),
        compiler_params=_cparams(("parallel",)),
        name="ple",
    )(h, y, p, w_gate, w_proj, gain)


def _pad_lanes(x):
    return jnp.pad(x, ((0, 0), (0, LANES - x.shape[-1])))


def _row(x):
    return x.reshape(1, -1).astype(F32)


def kernel(x, p, norm_mix, norm_ffn, norm_final, ab_w_in, ab_w_out, a_lam_q1, a_lam_k1, a_lam_q2, a_lam_k2,
           a_subln, b_conv, b_igate_bias, b_fgate_bias, b_norm, cd_w_in, cd_w_out, c_conv, c_a_log, c_dt_bias,
           c_norm, d_fgate_bias, moe_w_group, moe_b_group, moe_w_router, moe_b_router, moe_w_gate, moe_w_up,
           moe_w_down, ple_w_gate, ple_w_proj):
    batch, seq, _ = x.shape
    depth = p.shape[0]
    t = batch * seq
    assert seq % IN_TM == 0 and seq % ATT_TK == 0 and t % MOE_TM == 0
    h = x.reshape(t, D_MODEL)
    cos, sin = _rope_tables(seq)
    ones_tab = jnp.ones((seq, LANES), F32)

    for i in range(depth):
        j = i // 2
        if i % 2 == 0:
            w_in = ab_w_in[j]
            z, zg = _inproj(h, _row(norm_mix[i]), w_in[:, :Z_MAIN].astype(BF16),
                            _pad_lanes(w_in[:, Z_MAIN:]).astype(BF16), cos, sin, seq, rope_chunks=2)
            lam_init = 0.8 - 0.6 * math.exp(-0.3 * i)
            lam_p = jnp.stack([a_lam_q1[j], a_lam_k1[j], a_lam_q2[j], a_lam_k2[j]]).astype(F32)
            out_1 = _attn_a(z, lam_p, a_subln[j].reshape(HEAD_DIM, 1).astype(F32), batch, seq, lam_init)
            gate_bias = _pad_lanes(jnp.concatenate([b_igate_bias[j], b_fgate_bias[j]]).reshape(1, -1))
            out_2 = _mlstm(z, zg, b_conv[j], gate_bias, _row(b_norm[j]), batch, seq)
            w_out = ab_w_out[j]
        else:
            w_in = cd_w_in[j]
            c_main, d_main = 4 * MIX_W, 3 * MIX_W
            w_main = jnp.concatenate([w_in[:, :c_main], w_in[:, c_main + 8:c_main + 8 + d_main]], axis=1)
            w_small = jnp.concatenate([w_in[:, c_main:c_main + 8], w_in[:, c_main + 8 + d_main:]], axis=1)
            z, zg = _inproj(h, _row(norm_mix[i]), w_main.astype(BF16), _pad_lanes(w_small).astype(BF16),
                            ones_tab, ones_tab, seq, rope_chunks=0)
            params = jnp.stack([_pad_lanes(c_a_log[j].reshape(1, -1))[0], _pad_lanes(c_dt_bias[j].reshape(1, -1))[0]])
            out_1 = _deltanet(z, zg, c_conv[j], params, _row(c_norm[j]), batch, seq)
            fd = zg[:, 8:8 + N_HEADS].reshape(batch, seq, N_HEADS).transpose(0, 2, 1)
            bias_rows = jnp.repeat(jnp.tile(d_fgate_bias[j], batch), seq // LANES).reshape(-1, 1)
            cum = _logf_cumsum(fd.reshape(-1, LANES), jnp.broadcast_to(bias_rows, (bias_rows.shape[0], LANES)),
                               seq // LANES)
            kx, qx = _forget_bias_operands(cum.reshape(batch * N_HEADS, seq))
            out_2 = _attn_d(z, kx, qx, batch, seq)
            w_out = cd_w_out[j]

        w_r = _pad_lanes(jnp.concatenate([moe_w_router[i], moe_w_group[i]], axis=1))
        w_r_hi_f = _bf16_part(w_r)
        w_r_hi = w_r_hi_f.astype(BF16)
        w_r_lo = (w_r - w_r_hi_f).astype(BF16)
        b_r = _pad_lanes(jnp.concatenate([moe_b_router[i], moe_b_group[i]]).reshape(1, -1))
        h, xn, comb = _outproj_router(h, out_1, out_2, w_out.astype(BF16), _row(norm_ffn[i]), w_r_hi, w_r_lo, b_r)
        y = _moe(xn, comb, moe_w_gate[i].astype(BF16), moe_w_up[i].astype(BF16), moe_w_down[i].astype(BF16))
        h = _ple(h, y, p[i].reshape(t, PLE_DIM), ple_w_gate[i].astype(BF16), ple_w_proj[i].astype(BF16),
                 _row(norm_final), final_norm=(i == depth - 1))
    return h.reshape(batch, seq, D_MODEL)
```

```python
import functools
import math

import jax
import jax.numpy as jnp
from jax import lax
from jax.experimental import pallas as pl
from jax.experimental.pallas import tpu as pltpu

F32 = jnp.float32
BF16 = jnp.bfloat16
HIGHEST = lax.Precision.HIGHEST

D_MODEL = 1024
HEAD_DIM = 128
N_HEADS = 4
MIX_W = N_HEADS * HEAD_DIM
A_HALF = HEAD_DIM // 2
CHUNK = 64
CONV_W = 4
RMS_EPS = 1e-6
PLE_DIM = 256
N_GROUPS = 4
EXPERTS_PER_GROUP = 4
N_EXPERTS = 16
EXPERT_FF = D_MODEL // 2
ROPE_THETA = 10000.0
Z_MAIN = 7 * MIX_W
LANES = 128

VMEM_LIMIT = 56 * 1024 * 1024

NEG_INF = float("-inf")


def _cparams(sem):
    return pltpu.CompilerParams(dimension_semantics=sem, vmem_limit_bytes=VMEM_LIMIT)


def _dot(a, b):
    return jnp.dot(a, b, preferred_element_type=F32)


def _dot_nt(a, b):
    return lax.dot_general(a, b, (((1,), (1,)), ((), ())), preferred_element_type=F32)


def _dot_exact(a, b):
    return jnp.dot(a, b, preferred_element_type=F32, precision=HIGHEST)


def _bf16_part(x):
    bits = lax.bitcast_convert_type(x, jnp.int32) & jnp.int32(-65536)
    return lax.bitcast_convert_type(bits, F32)


def _sigmoid(x):
    return 1.0 / (1.0 + jnp.exp(-x))


def _silu(x):
    return x * _sigmoid(x)


def _log_sigmoid(x):
    return jnp.minimum(x, 0.0) - jnp.log(1.0 + jnp.exp(-jnp.abs(x)))


def _softplus(x):
    return jnp.maximum(x, 0.0) + jnp.log(1.0 + jnp.exp(-jnp.abs(x)))


def _rms(x, gain):
    return x * lax.rsqrt(jnp.mean(x * x, axis=-1, keepdims=True) + RMS_EPS) * gain


IN_TM = 512
IN_TN = 512


def _inproj_kernel(h_ref, g_ref, w_ref, wg_ref, cos_ref, sin_ref, z_ref, zg_ref, *, rope_chunks):
    hn = _rms(h_ref[...], g_ref[...])
    hb = hn.astype(BF16)
    zg_ref[...] = _dot(hb, wg_ref[...])
    if rope_chunks:
        lane = lax.broadcasted_iota(jnp.int32, (IN_TM, IN_TN), 1)
        first_half = (lane & 32) == 0
        cos = jnp.concatenate([cos_ref[...]] * (IN_TN // LANES), axis=1)
        sin = jnp.concatenate([sin_ref[...]] * (IN_TN // LANES), axis=1)
    for c in range(Z_MAIN // IN_TN):
        zc = _dot(hb, w_ref[:, c * IN_TN:(c + 1) * IN_TN])
        if c < rope_chunks:
            partner = jnp.where(first_half, pltpu.roll(zc, IN_TN - 32, axis=1), pltpu.roll(zc, 32, axis=1))
            zc = zc * cos + partner * sin
        z_ref[:, c * IN_TN:(c + 1) * IN_TN] = zc.astype(BF16)


def _inproj(h, gain, w_main, w_gate, cos, sin, seq, rope_chunks):
    t = h.shape[0]
    per_seq = seq // IN_TM
    return pl.pallas_call(
        functools.partial(_inproj_kernel, rope_chunks=rope_chunks),
        grid=(t // IN_TM,),
        in_specs=[
            pl.BlockSpec((IN_TM, D_MODEL), lambda i: (i, 0)),
            pl.BlockSpec((1, D_MODEL), lambda i: (0, 0)),
            pl.BlockSpec((D_MODEL, Z_MAIN), lambda i: (0, 0)),
            pl.BlockSpec((D_MODEL, LANES), lambda i: (0, 0)),
            pl.BlockSpec((IN_TM, LANES), lambda i: (i % per_seq, 0)),
            pl.BlockSpec((IN_TM, LANES), lambda i: (i % per_seq, 0)),
        ],
        out_specs=[
            pl.BlockSpec((IN_TM, Z_MAIN), lambda i: (i, 0)),
            pl.BlockSpec((IN_TM, LANES), lambda i: (i, 0)),
        ],
        out_shape=[jax.ShapeDtypeStruct((t, Z_MAIN), BF16), jax.ShapeDtypeStruct((t, LANES), F32)],
        compiler_params=_cparams(("parallel",)),
        name="inproj",
    )(h, gain, w_main, w_gate, cos, sin)


def _rope_tables(seq):
    inv = ROPE_THETA ** (-jnp.arange(0, A_HALF, 2, dtype=F32) / A_HALF)
    ang = jnp.arange(seq, dtype=F32)[:, None] * inv[None, :]
    cos, sin = jnp.cos(ang), jnp.sin(ang)
    return jnp.tile(cos, (1, 4)), jnp.tile(jnp.concatenate([-sin, sin], axis=1), (1, 2))


ATT_W = 512
ATT_TK = 512
LOG2E = 1.4426950408889634


def _attn_init(v_ref, vt_ref, m_ref, l_ref, acc_ref):
    @pl.when(pl.program_id(2) == 0)
    def _():
        def body(j, carry):
            off = pl.multiple_of(j * ATT_TK, ATT_TK)
            vt_ref[j] = v_ref[pl.ds(off, ATT_TK), :].astype(F32).T.astype(BF16)
            return carry
        lax.fori_loop(0, vt_ref.shape[0], body, 0)

    m_ref[...] = jnp.full(m_ref.shape, NEG_INF, F32)
    l_ref[...] = jnp.zeros(l_ref.shape, F32)
    acc_ref[...] = jnp.zeros(acc_ref.shape, F32)


def _attn_logits(k_ref, q_t, s_ref, blk, mask):
    off = pl.multiple_of(blk * ATT_TK, ATT_TK)
    s = _dot(k_ref[pl.ds(off, ATT_TK), :], q_t)
    if mask is not None:
        s = jnp.where(mask, s, NEG_INF)
    s_ref[...] = s


def _attn_consume(s_ref, vt_ref, blk, m_ref, l_ref, acc_ref):
    s = s_ref[...]
    m_prev = m_ref[...]
    m_new = jnp.maximum(m_prev, jnp.max(s, axis=0, keepdims=True))
    alpha = jnp.exp2(m_prev - m_new)
    p = jnp.exp2(s - m_new)
    l_ref[...] = alpha * l_ref[...] + jnp.sum(p, axis=0, keepdims=True)
    acc_ref[...] = alpha * acc_ref[...] + _dot(vt_ref[blk], p.astype(BF16))
    m_ref[...] = m_new


def _attn_pipeline(k_ref, vt_ref, q_t, s0_ref, s1_ref, m_ref, l_ref, acc_ref, n_full, diag_blk, diag_mask):
    def consume(s_ref, blk):
        _attn_consume(s_ref, vt_ref, blk, m_ref, l_ref, acc_ref)

    _attn_logits(k_ref, q_t, s0_ref, diag_blk, diag_mask)

    def body(t, carry):
        _attn_logits(k_ref, q_t, s1_ref, 2 * t, None)
        consume(s0_ref, jnp.where(t == 0, diag_blk, 2 * t - 1))
        _attn_logits(k_ref, q_t, s0_ref, 2 * t + 1, None)
        consume(s1_ref, 2 * t)
        return carry

    npairs = n_full // 2
    lax.fori_loop(0, npairs, body, 0)
    in_s0 = jnp.where(npairs == 0, diag_blk, 2 * npairs - 1)

    @pl.when(n_full % 2 == 1)
    def _():
        _attn_logits(k_ref, q_t, s1_ref, n_full - 1, None)
        consume(s0_ref, in_s0)
        consume(s1_ref, n_full - 1)

    @pl.when(n_full % 2 == 0)
    def _():
        consume(s0_ref, in_s0)


def _attn_a_kernel(q_ref, k_ref, v_ref, lam_ref, gain_ref, o_ref, vt_ref, s0_ref, s1_ref, m_ref, l_ref, acc_ref,
                   *, lam_init):
    _attn_init(v_ref, vt_ref, m_ref, l_ref, acc_ref)
    i = pl.program_id(2)
    tq = ATT_W
    q = (q_ref[...].astype(F32) * (A_HALF ** -0.5 * LOG2E)).T
    dim = lax.broadcasted_iota(jnp.int32, q.shape, 0)
    q_t = jnp.concatenate([jnp.where(dim < A_HALF, q, 0.0), jnp.where(dim >= A_HALF, q, 0.0)], axis=1).astype(BF16)
    key = lax.broadcasted_iota(jnp.int32, (ATT_TK, 2 * tq), 0)
    qry = lax.broadcasted_iota(jnp.int32, (ATT_TK, 2 * tq), 1) & (tq - 1)
    _attn_pipeline(k_ref, vt_ref, q_t, s0_ref, s1_ref, m_ref, l_ref, acc_ref, i, i, (key >> 6) <= (qry >> 6))

    lam_p = lam_ref[...]
    lam = (jnp.exp(jnp.sum(lam_p[0:1] * lam_p[1:2], axis=-1, keepdims=True))
           - jnp.exp(jnp.sum(lam_p[2:3] * lam_p[3:4], axis=-1, keepdims=True)) + lam_init)
    o_all = acc_ref[...] * (1.0 / l_ref[...])
    out = o_all[:, 0:tq] - lam * o_all[:, tq:2 * tq]
    out = out * lax.rsqrt(jnp.mean(out * out, axis=0, keepdims=True) + RMS_EPS) * (gain_ref[...] * (1.0 - lam_init))
    o_ref[...] = out.T.astype(BF16)


def _attn_scratch(seq, cols):
    return [pltpu.VMEM((seq // ATT_TK, HEAD_DIM, ATT_TK), BF16), pltpu.VMEM((ATT_TK, cols), F32),
            pltpu.VMEM((ATT_TK, cols), F32), pltpu.VMEM((1, cols), F32), pltpu.VMEM((1, cols), F32),
            pltpu.VMEM((HEAD_DIM, cols), F32)]


def _attn_a(z, lam_p, gain_col, batch, seq, lam_init):
    t = batch * seq
    nq = seq // ATT_W
    return pl.pallas_call(
        functools.partial(_attn_a_kernel, lam_init=lam_init),
        grid=(batch, N_HEADS, nq),
        in_specs=[
            pl.BlockSpec((ATT_W, HEAD_DIM), lambda b, h, i: (b * nq + i, h)),
            pl.BlockSpec((seq, HEAD_DIM), lambda b, h, i: (b, N_HEADS + h)),
            pl.BlockSpec((seq, HEAD_DIM), lambda b, h, i: (b, 2 * N_HEADS + h)),
            pl.BlockSpec((4, A_HALF), lambda b, h, i: (0, 0)),
            pl.BlockSpec((HEAD_DIM, 1), lambda b, h, i: (0, 0)),
        ],
        out_specs=pl.BlockSpec((ATT_W, HEAD_DIM), lambda b, h, i: (b * nq + i, h)),
        out_shape=jax.ShapeDtypeStruct((t, MIX_W), BF16),
        scratch_shapes=_attn_scratch(seq, 2 * ATT_W),
        compiler_params=_cparams(("parallel", "parallel", "arbitrary")),
        name="attn_a",
    )(z, z, z, lam_p, gain_col)


D_AUG = 2 * HEAD_DIM
BIAS_ROWS = 16


def _attn_d_kernel(q_ref, k_ref, v_ref, kx_ref, qx_ref, o_ref, kaug_ref, vt_ref, s0_ref, s1_ref, m_ref, l_ref,
                   acc_ref):
    _attn_init(v_ref, vt_ref, m_ref, l_ref, acc_ref)
    i = pl.program_id(2)

    @pl.when(i == 0)
    def _():
        def body(j, carry):
            off = pl.multiple_of(j * ATT_TK, ATT_TK)
            kaug_ref[pl.ds(off, ATT_TK), 0:HEAD_DIM] = k_ref[pl.ds(off, ATT_TK), :]
            kaug_ref[pl.ds(off, ATT_TK), HEAD_DIM:D_AUG] = kx_ref[0, pl.ds(off, ATT_TK), :]
            return carry
        lax.fori_loop(0, vt_ref.shape[0], body, 0)

    q = (q_ref[...].astype(F32) * (HEAD_DIM ** -0.5 * LOG2E)).T.astype(BF16)
    q_t = jnp.concatenate([q, qx_ref[0], jnp.zeros((D_AUG - HEAD_DIM - BIAS_ROWS, ATT_W), BF16)], axis=0)
    key = lax.broadcasted_iota(jnp.int32, (ATT_TK, ATT_W), 0)
    qry = lax.broadcasted_iota(jnp.int32, (ATT_TK, ATT_W), 1)
    _attn_pipeline(kaug_ref, vt_ref, q_t, s0_ref, s1_ref, m_ref, l_ref, acc_ref, i, i, key <= qry)
    o_ref[...] = (acc_ref[...] * (1.0 / l_ref[...])).T.astype(BF16)


def _attn_d(z, kx, qx, batch, seq):
    t = batch * seq
    nq = seq // ATT_W
    base = 4 * N_HEADS
    return pl.pallas_call(
        _attn_d_kernel,
        grid=(batch, N_HEADS, nq),
        in_specs=[
            pl.BlockSpec((ATT_W, HEAD_DIM), lambda b, h, i: (b * nq + i, base + h)),
            pl.BlockSpec((seq, HEAD_DIM), lambda b, h, i: (b, base + N_HEADS + h)),
            pl.BlockSpec((seq, HEAD_DIM), lambda b, h, i: (b, base + 2 * N_HEADS + h)),
            pl.BlockSpec((1, seq, HEAD_DIM), lambda b, h, i: (b * N_HEADS + h, 0, 0)),
            pl.BlockSpec((1, BIAS_ROWS, ATT_W), lambda b, h, i: (b * N_HEADS + h, 0, i)),
        ],
        out_specs=pl.BlockSpec((ATT_W, HEAD_DIM), lambda b, h, i: (b * nq + i, h)),
        out_shape=jax.ShapeDtypeStruct((t, MIX_W), BF16),
        scratch_shapes=[pltpu.VMEM((seq, D_AUG), BF16)] + _attn_scratch(seq, ATT_W),
        compiler_params=_cparams(("parallel", "parallel", "arbitrary")),
        name="attn_d",
    )(z, z, z, kx, qx)


def _forget_bias_operands(cum):
    c = cum * LOG2E
    hi_f = _bf16_part(c)
    mid_f = _bf16_part(c - hi_f)
    hi, mid, lo = hi_f.astype(BF16), mid_f.astype(BF16), (c - hi_f - mid_f).astype(BF16)
    one = jnp.ones_like(hi)
    kx = jnp.stack([hi, mid, lo, one, one, one], axis=-1)
    kx = jnp.pad(kx, ((0, 0), (0, 0), (0, HEAD_DIM - 6)))
    qx = jnp.stack([-one, -one, -one, hi, mid, lo], axis=1)
    qx = jnp.pad(qx, ((0, 0), (0, BIAS_ROWS - 6), (0, 0)))
    return kx, qx


def _logf_cumsum_kernel(x_ref, b_ref, o_ref, *, rows_per_seq):
    r = x_ref.shape[0]
    lf = _log_sigmoid(x_ref[...] + b_ref[...])
    a = lax.broadcasted_iota(jnp.int32, (LANES, LANES), 0)
    b = lax.broadcasted_iota(jnp.int32, (LANES, LANES), 1)
    within = _dot_exact(lf, (a <= b).astype(F32))
    totals = _dot_exact(lf, jnp.ones((LANES, LANES), F32))
    ra = lax.broadcasted_iota(jnp.int32, (r, r), 0)
    rb = lax.broadcasted_iota(jnp.int32, (r, r), 1)
    earlier = ((rb < ra) & ((rb // rows_per_seq) == (ra // rows_per_seq))).astype(F32)
    o_ref[...] = within + _dot_exact(earlier, totals)


def _logf_cumsum(x, bias_rows, rows_per_seq):
    r = x.shape[0]
    return pl.pallas_call(
        functools.partial(_logf_cumsum_kernel, rows_per_seq=rows_per_seq),
        out_shape=jax.ShapeDtypeStruct((r, LANES), F32),
        compiler_params=pltpu.CompilerParams(vmem_limit_bytes=VMEM_LIMIT),
        name="logf_cumsum",
    )(x, bias_rows)


CONV_TAIL = 8


def _causal_conv_silu(x, prev_tail, w):
    row = lax.broadcasted_iota(jnp.int32, prev_tail.shape, 0)
    acc = x * w[CONV_W - 1:CONV_W, :]
    for s in range(1, CONV_W):
        xs = pltpu.roll(x, s, axis=0)
        head = jnp.where(row >= s, xs[0:CONV_TAIL], pltpu.roll(prev_tail, s, axis=0))
        xs = jnp.concatenate([head, xs[CONV_TAIL:]], axis=0)
        acc = acc + xs * w[CONV_W - 1 - s:CONV_W - s, :]
    return _silu(acc)


def _chunk_masks():
    r = lax.broadcasted_iota(jnp.int32, (CHUNK, CHUNK), 0)
    c = lax.broadcasted_iota(jnp.int32, (CHUNK, CHUNK), 1)
    return r, c


REC_NB_MLSTM = 2
REC_NB_DELTANET = 4


def _rec_batch_block(batch, nb):
    return nb if batch % nb == 0 else 1


def _rec_specs(nb, cols):
    main = [pl.BlockSpec((nb, CHUNK, MIX_W), lambda b, c, j=j: (b, c, j)) for j in cols]
    return main + [pl.BlockSpec((nb, CHUNK, LANES), lambda b, c: (b, c, 0))]


def _mlstm_kernel(q_ref, k_ref, v_ref, og_ref, g_ref, conv_ref, bias_ref, norm_ref, out_ref,
                  pq_ref, pk_ref, c_ref, n_ref, m_ref):
    @pl.when(pl.program_id(1) == 0)
    def _():
        pq_ref[...] = jnp.zeros(pq_ref.shape, F32)
        pk_ref[...] = jnp.zeros(pk_ref.shape, F32)
        c_ref[...] = jnp.zeros(c_ref.shape, F32)
        n_ref[...] = jnp.zeros(n_ref.shape, F32)
        m_ref[...] = jnp.zeros(m_ref.shape, F32)

    lane = lax.broadcasted_iota(jnp.int32, (CHUNK, LANES), 1)
    r, c = _chunk_masks()
    causal = c <= r
    tri = causal.astype(F32)
    units = []
    for bi in range(q_ref.shape[0]):
        xq = q_ref[bi].astype(F32)
        xk = k_ref[bi].astype(F32)
        q_all = _causal_conv_silu(xq, pq_ref[bi], conv_ref[:, 0:MIX_W])
        k_all = _causal_conv_silu(xk, pk_ref[bi], conv_ref[:, MIX_W:2 * MIX_W]) * (HEAD_DIM ** -0.5)
        pq_ref[bi] = xq[CHUNK - CONV_TAIL:CHUNK]
        pk_ref[bi] = xk[CHUNK - CONV_TAIL:CHUNK]
        pre = g_ref[bi] + bias_ref[...]
        e = jnp.where(lane < N_HEADS, pre, _log_sigmoid(pre))
        cum = _dot_exact(tri, e)
        e_t = e.T
        cum_t = cum.T
        for h in range(N_HEADS):
            sl = slice(h * HEAD_DIM, (h + 1) * HEAD_DIM)
            u = dict(bi=bi, h=h, sl=sl, q=q_all[:, sl], k=k_all[:, sl], v=v_ref[bi, :, sl].astype(F32))
            u["qb"] = u["q"].astype(BF16)
            u["kb"] = u["k"].astype(BF16)
            i_col = e[:, h:h + 1]
            i_row = e_t[h:h + 1, :]
            b_col = cum[:, N_HEADS + h:N_HEADS + h + 1]
            b_row = cum_t[N_HEADS + h:N_HEADS + h + 1, :]
            b_last = b_col[CHUNK - 1:CHUNK, :]
            m_st = m_ref[bi, h:h + 1, 0:1]
            u["c_st"] = c_ref[bi, h]
            u["n_st"] = n_ref[bi, h:h + 1, :]
            d_log = jnp.where(causal, b_col - b_row + i_row, NEG_INF)
            inter_log = b_col + m_st
            u["m_t"] = jnp.maximum(inter_log, jnp.max(d_log, axis=-1, keepdims=True))
            u["d_w"] = jnp.exp(d_log - u["m_t"])
            u["inter_w"] = jnp.exp(inter_log - u["m_t"])
            w_log = b_last - b_col + i_col
            u["m_new"] = jnp.maximum(b_last + m_st, jnp.max(w_log, axis=0, keepdims=True))
            u["sw"] = jnp.exp(w_log - u["m_new"])
            u["decay"] = jnp.exp(b_last + m_st - u["m_new"])
            units.append(u)

    for u in units:
        u["qk"] = _dot_nt(u["qb"], u["kb"])
    for u in units:
        u["qc"] = _dot_nt(u["qb"], u["c_st"].astype(BF16))
    for u in units:
        u["upd"] = _dot((u["v"] * u["sw"]).T.astype(BF16), u["kb"])
    for u in units:
        u["a"] = u["qk"] * u["d_w"]
        u["av"] = _dot(u["a"].astype(BF16), u["v"].astype(BF16))
    for u in units:
        bi, h, sl = u["bi"], u["h"], u["sl"]
        num = u["av"] + u["inter_w"] * u["qc"]
        den = (jnp.sum(u["a"], axis=-1, keepdims=True)
               + u["inter_w"] * jnp.sum(u["q"] * u["n_st"], axis=-1, keepdims=True))
        hh = num / jnp.maximum(jnp.abs(den), jnp.exp(-u["m_t"]))
        c_ref[bi, h] = u["decay"] * u["c_st"] + u["upd"]
        n_ref[bi, h:h + 1, :] = u["decay"] * u["n_st"] + jnp.sum(u["k"] * u["sw"], axis=0, keepdims=True)
        m_ref[bi, h:h + 1, :] = jnp.broadcast_to(u["m_new"], (1, LANES))
        gate = _sigmoid(og_ref[bi, :, sl].astype(F32))
        out_ref[bi, :, sl] = (_rms(hh, norm_ref[...]) * gate).astype(BF16)


def _mlstm(z, zg, conv_w, gate_bias, norm, batch, seq):
    nb = _rec_batch_block(batch, REC_NB_MLSTM)
    full = lambda s: pl.BlockSpec(s, lambda b, c: (0, 0))
    out = pl.pallas_call(
        _mlstm_kernel,
        grid=(batch // nb, seq // CHUNK),
        in_specs=_rec_specs(nb, (3, 4, 5, 6)) + [full((CONV_W, 2 * MIX_W)), full((1, LANES)), full((1, HEAD_DIM))],
        out_specs=pl.BlockSpec((nb, CHUNK, MIX_W), lambda b, c: (b, c, 0)),
        out_shape=jax.ShapeDtypeStruct((batch, seq, MIX_W), BF16),
        scratch_shapes=[pltpu.VMEM((nb, CONV_TAIL, MIX_W), F32), pltpu.VMEM((nb, CONV_TAIL, MIX_W), F32),
                        pltpu.VMEM((nb, N_HEADS, HEAD_DIM, HEAD_DIM), F32), pltpu.VMEM((nb, 8, HEAD_DIM), F32),
                        pltpu.VMEM((nb, 8, LANES), F32)],
        compiler_params=_cparams(("parallel", "arbitrary")),
        name="mlstm",
    )(*([z.reshape(batch, seq, Z_MAIN)] * 4), zg.reshape(batch, seq, LANES), conv_w, gate_bias, norm)
    return out.reshape(batch * seq, MIX_W)


def _bdot(a, b):
    return _dot(a.astype(BF16), b.astype(BF16))


def _unit_lower_inverses(l_stricts, r, c):
    same16 = (r >> 4) == (c >> 4)
    same32 = (r >> 5) == (c >> 5)
    eye = (r == c).astype(F32)
    ps = [jnp.where(same16, -l, 0.0) for l in l_stricts]
    invs = [eye + p for p in ps]
    for _ in range(3):
        ps = [_bdot(p, p) for p in ps]
        invs = [inv + _bdot(inv, p) for inv, p in zip(invs, ps)]
    for off_mask in (same32 & jnp.logical_not(same16), jnp.logical_not(same32)):
        ts = [_bdot(jnp.where(off_mask, l, 0.0), inv) for l, inv in zip(l_stricts, invs)]
        invs = [inv - _bdot(inv, t) for inv, t in zip(invs, ts)]
    return invs


def _deltanet_kernel(q_ref, k_ref, v_ref, og_ref, g_ref, conv_ref, par_ref, norm_ref, out_ref,
                     pq_ref, pk_ref, pv_ref, s_ref):
    @pl.when(pl.program_id(1) == 0)
    def _():
        pq_ref[...] = jnp.zeros(pq_ref.shape, F32)
        pk_ref[...] = jnp.zeros(pk_ref.shape, F32)
        pv_ref[...] = jnp.zeros(pv_ref.shape, F32)
        s_ref[...] = jnp.zeros(s_ref.shape, F32)

    lane = lax.broadcasted_iota(jnp.int32, (CHUNK, LANES), 1)
    r, c = _chunk_masks()
    tri = (c <= r).astype(F32)
    units = []
    for bi in range(q_ref.shape[0]):
        xq = q_ref[bi].astype(F32)
        xk = k_ref[bi].astype(F32)
        xv = v_ref[bi].astype(F32)
        q_all = _causal_conv_silu(xq, pq_ref[bi], conv_ref[:, 0:MIX_W])
        k_all = _causal_conv_silu(xk, pk_ref[bi], conv_ref[:, MIX_W:2 * MIX_W])
        v_all = _causal_conv_silu(xv, pv_ref[bi], conv_ref[:, 2 * MIX_W:3 * MIX_W])
        pq_ref[bi] = xq[CHUNK - CONV_TAIL:CHUNK]
        pk_ref[bi] = xk[CHUNK - CONV_TAIL:CHUNK]
        pv_ref[bi] = xv[CHUNK - CONV_TAIL:CHUNK]
        pre = g_ref[bi]
        e = jnp.where(lane < N_HEADS, -jnp.exp(par_ref[0:1, :]) * _softplus(pre + par_ref[1:2, :]), _sigmoid(pre))
        cum = _dot_exact(tri, e)
        cum_t = cum.T
        for h in range(N_HEADS):
            sl = slice(h * HEAD_DIM, (h + 1) * HEAD_DIM)
            q = q_all[:, sl]
            k = k_all[:, sl]
            q = q * lax.rsqrt(jnp.sum(q * q, axis=-1, keepdims=True) + RMS_EPS) * (HEAD_DIM ** -0.5)
            k = k * lax.rsqrt(jnp.sum(k * k, axis=-1, keepdims=True) + RMS_EPS)
            beta = e[:, N_HEADS + h:N_HEADS + h + 1]
            b_col = cum[:, h:h + 1]
            b_row = cum_t[h:h + 1, :]
            b_last = b_col[CHUNK - 1:CHUNK, :]
            e_col = jnp.exp(b_col)
            kbeta = k * beta
            units.append(dict(
                bi=bi, h=h, sl=sl, k=k, kbeta=kbeta, q_dec=q * e_col,
                gam=jnp.exp(jnp.where(c <= r, b_col - b_row, NEG_INF)),
                rhs=jnp.concatenate([v_all[:, sl] * beta, kbeta * e_col], axis=1),
                k_dec=k * jnp.exp(b_last - b_col), chunk_decay=jnp.exp(b_last),
                kq=jnp.concatenate([kbeta, q], axis=0)))

    for u in units:
        both = _dot_nt(u["kq"].astype(BF16), u["k"].astype(BF16)) * jnp.concatenate([u["gam"], u["gam"]], axis=0)
        u["l"] = jnp.where(c < r, both[0:CHUNK], 0.0)
        u["qk"] = both[CHUNK:2 * CHUNK]
    invs = _unit_lower_inverses([u["l"] for u in units], r, c)
    for u, inv in zip(units, invs):
        sol = _bdot(inv, u["rhs"])
        u["u"] = sol[:, 0:HEAD_DIM]
        u["w"] = sol[:, HEAD_DIM:2 * HEAD_DIM]
    for u in units:
        u["st"] = s_ref[u["bi"], u["h"]]
        u["ws"] = _bdot(jnp.concatenate([u["w"], u["q_dec"]], axis=0), u["st"])
    for u in units:
        u["v_new"] = u["u"] - u["ws"][0:CHUNK]
        u["o"] = u["ws"][CHUNK:2 * CHUNK] + _bdot(u["qk"], u["v_new"])
    for u in units:
        s_ref[u["bi"], u["h"]] = u["chunk_decay"] * u["st"] + _bdot(u["k_dec"].T, u["v_new"])
    for u in units:
        bi, sl = u["bi"], u["sl"]
        gate = _silu(og_ref[bi, :, sl].astype(F32))
        out_ref[bi, :, sl] = (_rms(u["o"], norm_ref[...]) * gate).astype(BF16)


def _deltanet(z, zg, conv_w, params, norm, batch, seq):
    nb = _rec_batch_block(batch, REC_NB_DELTANET)
    full = lambda s: pl.BlockSpec(s, lambda b, c: (0, 0))
    out = pl.pallas_call(
        _deltanet_kernel,
        grid=(batch // nb, seq // CHUNK),
        in_specs=_rec_specs(nb, (0, 1, 2, 3)) + [full((CONV_W, 3 * MIX_W)), full((2, LANES)), full((1, HEAD_DIM))],
        out_specs=pl.BlockSpec((nb, CHUNK, MIX_W), lambda b, c: (b, c, 0)),
        out_shape=jax.ShapeDtypeStruct((batch, seq, MIX_W), BF16),
        scratch_shapes=[pltpu.VMEM((nb, CONV_TAIL, MIX_W), F32)] * 3
        + [pltpu.VMEM((nb, N_HEADS, HEAD_DIM, HEAD_DIM), F32)],
        compiler_params=_cparams(("parallel", "arbitrary")),
        name="deltanet",
    )(*([z.reshape(batch, seq, Z_MAIN)] * 4), zg.reshape(batch, seq, LANES), conv_w, params, norm)
    return out.reshape(batch * seq, MIX_W)


OUT_TM = 512
G_LANE0 = N_EXPERTS
ROUTE_W_LANE = 4
ROW_TILES = D_MODEL // LANES


def _outproj_router_kernel(h_ref, a_ref, b_ref, wo_ref, gn_ref, whi_ref, wlo_ref, br_ref,
                           hout_ref, xn_ref, comb_ref):
    h_new = h_ref[...] + _dot(a_ref[...], wo_ref[0:MIX_W, :]) + _dot(b_ref[...], wo_ref[MIX_W:2 * MIX_W, :])
    hout_ref[...] = h_new
    hn = _rms(h_new, gn_ref[...])
    for c in range(ROW_TILES):
        xn_ref[:, c, :] = hn[:, c * LANES:(c + 1) * LANES]
    hi_f = _bf16_part(hn)
    hi = hi_f.astype(BF16)
    lo = (hn - hi_f).astype(BF16)
    logits = (_dot(hi, whi_ref[...]) + _dot(lo, whi_ref[...]) + _dot(hi, wlo_ref[...]) + _dot(lo, wlo_ref[...])
              + br_ref[...])

    lane_i = lax.broadcasted_iota(jnp.int32, logits.shape, 1)
    lane = lane_i.astype(F32)
    big = 1e9
    is_group = (lane_i >= G_LANE0) & (lane_i < G_LANE0 + N_GROUPS)
    gl = jnp.where(is_group, logits, NEG_INF)
    gmax = jnp.max(gl, axis=-1, keepdims=True)
    g_sel = jnp.min(jnp.where(gl == gmax, lane, big), axis=-1, keepdims=True) - G_LANE0
    p_top = 1.0 / jnp.sum(jnp.exp(gl - gmax), axis=-1, keepdims=True)
    group_of_lane = (lane_i >> 2).astype(F32)
    in_group = (lane_i < N_EXPERTS) & (group_of_lane == g_sel)
    el = jnp.where(in_group, logits, NEG_INF)
    v1 = jnp.max(el, axis=-1, keepdims=True)
    i1 = jnp.min(jnp.where(el == v1, lane, big), axis=-1, keepdims=True)
    el2 = jnp.where(lane == i1, NEG_INF, el)
    v2 = jnp.max(el2, axis=-1, keepdims=True)
    i2 = jnp.min(jnp.where(el2 == v2, lane, big), axis=-1, keepdims=True)
    e21 = jnp.exp(v2 - v1)
    w1 = p_top / (1.0 + e21)
    local = lane - ROUTE_W_LANE + EXPERTS_PER_GROUP * g_sel
    comb_ref[...] = (jnp.where(lane_i == 0, g_sel, 0.0) + jnp.where(local == i1, w1, 0.0)
                     + jnp.where(local == i2, w1 * e21, 0.0))


def _outproj_router(h, oa, ob, w_out, gain, wr_hi, wr_lo, b_r):
    t = h.shape[0]
    row = lambda w: pl.BlockSpec((OUT_TM, w), lambda i: (i, 0))
    full = lambda s: pl.BlockSpec(s, lambda i: (0, 0))
    return pl.pallas_call(
        _outproj_router_kernel,
        grid=(t // OUT_TM,),
        in_specs=[row(D_MODEL), row(MIX_W), row(MIX_W), full((D_MODEL, D_MODEL)), full((1, D_MODEL)),
                  full((D_MODEL, LANES)), full((D_MODEL, LANES)), full((1, LANES))],
        out_specs=[row(D_MODEL), pl.BlockSpec((OUT_TM, ROW_TILES, LANES), lambda i: (i, 0, 0)), row(LANES)],
        out_shape=[jax.ShapeDtypeStruct((t, D_MODEL), F32), jax.ShapeDtypeStruct((t, ROW_TILES, LANES), F32),
                   jax.ShapeDtypeStruct((t, LANES), F32)],
        compiler_params=_cparams(("parallel",)),
        name="outproj_router",
    )(h, oa, ob, w_out, gain, wr_hi, wr_lo, b_r)


MOE_TM = 512


def _moe_plan(route, t):
    n_rows = t + (N_GROUPS + 1) * MOE_TM
    n_tiles = n_rows // MOE_TM
    group = route[:, 0].astype(jnp.int32)
    onehot = (group[:, None] == jnp.arange(N_GROUPS, dtype=jnp.int32)[None, :]).astype(jnp.int32)
    csum = jnp.cumsum(onehot, axis=0)
    rank = jnp.sum(csum * onehot, axis=1) - 1
    padded = ((csum[-1] + MOE_TM - 1) // MOE_TM) * MOE_TM
    ends = jnp.cumsum(padded)
    pos = jnp.sum((ends - padded)[None, :] * onehot, axis=1) + rank
    real = jnp.full((n_rows,), -1, jnp.int32).at[pos].set(jnp.arange(t, dtype=jnp.int32), unique_indices=True)
    is_pad = real < 0
    dst = jnp.where(is_pad, t + jnp.cumsum(is_pad.astype(jnp.int32)) - 1, real)
    w_sorted = jnp.take(route[:, ROUTE_W_LANE:ROUTE_W_LANE + EXPERTS_PER_GROUP], jnp.where(is_pad, 0, real), axis=0)
    starts = jnp.arange(n_tiles, dtype=jnp.int32) * MOE_TM
    tile_group = jnp.minimum(jnp.sum((starts[:, None] >= ends[None, :]).astype(jnp.int32), axis=1), N_GROUPS - 1)
    return tile_group, dst.reshape(n_tiles, MOE_TM), w_sorted


def _moe_kernel(tg_ref, dst_ref, x_hbm, ws_ref, wg_ref, wu_ref, wd_ref, out_hbm,
                xbuf, ybuf, sem_g, sem_s, *, n_tok):
    t = pl.program_id(0)
    nt = pl.num_programs(0)
    slot = t % 2

    def row_gather(tile, buf, r):
        d = dst_ref[tile, r]
        tok = jnp.where(d < n_tok, d, 0)
        return pltpu.make_async_copy(x_hbm.at[tok], xbuf.at[buf, r], sem_g.at[buf])

    def row_scatter(tile, buf, r):
        return pltpu.make_async_copy(ybuf.at[buf, r], out_hbm.at[dst_ref[tile, r]], sem_s.at[buf])

    def wait_gather(buf):
        pltpu.make_async_copy(x_hbm.at[pl.ds(0, MOE_TM)], xbuf.at[buf], sem_g.at[buf]).wait()

    def wait_scatter(buf):
        pltpu.make_async_copy(ybuf.at[buf], out_hbm.at[pl.ds(0, MOE_TM)], sem_s.at[buf]).wait()

    @pl.when(t == 0)
    def _():
        def body(r, carry):
            row_gather(0, 0, r).start()
            return carry
        lax.fori_loop(0, MOE_TM, body, 0, unroll=8)
        ybuf[1] = jnp.zeros(ybuf.shape[1:], F32)

    wait_gather(slot)

    nxt = jnp.where(t == nt - 1, 0, t + 1)
    prv = jnp.where(t == 0, nt - 1, t - 1)
    other = 1 - slot
    rows_per_expert = MOE_TM // EXPERTS_PER_GROUP
    x = jnp.concatenate([xbuf[slot, :, c, :] for c in range(ROW_TILES)], axis=1).astype(BF16)
    ws = ws_ref[...]
    y = jnp.zeros((MOE_TM, D_MODEL), F32)
    for e in range(EXPERTS_PER_GROUP):
        for r in range(e * rows_per_expert, (e + 1) * rows_per_expert):
            row_gather(nxt, other, r).start()
            row_scatter(prv, other, r).start()
        he = _silu(_dot(x, wg_ref[e])) * _dot(x, wu_ref[e])
        y = y + _dot((he * ws[:, e:e + 1]).astype(BF16), wd_ref[e])
    for c in range(ROW_TILES):
        ybuf[slot, :, c, :] = y[:, c * LANES:(c + 1) * LANES]

    wait_scatter(other)

    @pl.when(t == nt - 1)
    def _():
        wait_gather(other)


def _moe(xn, route, wg, wu, wd):
    t = xn.shape[0]
    tile_group, dst, w_sorted = _moe_plan(route, t)
    n_tiles = dst.shape[0]
    buf = pltpu.VMEM((2, MOE_TM, ROW_TILES, LANES), F32)
    grid_spec = pltpu.PrefetchScalarGridSpec(
        num_scalar_prefetch=2,
        grid=(n_tiles,),
        in_specs=[
            pl.BlockSpec(memory_space=pl.ANY),
            pl.BlockSpec((MOE_TM, EXPERTS_PER_GROUP), lambda i, tg, dst: (i, 0)),
            pl.BlockSpec((EXPERTS_PER_GROUP, D_MODEL, EXPERT_FF), lambda i, tg, dst: (tg[i], 0, 0)),
            pl.BlockSpec((EXPERTS_PER_GROUP, D_MODEL, EXPERT_FF), lambda i, tg, dst: (tg[i], 0, 0)),
            pl.BlockSpec((EXPERTS_PER_GROUP, EXPERT_FF, D_MODEL), lambda i, tg, dst: (tg[i], 0, 0)),
        ],
        out_specs=pl.BlockSpec(memory_space=pl.ANY),
        scratch_shapes=[buf, buf, pltpu.SemaphoreType.DMA((2,)), pltpu.SemaphoreType.DMA((2,))],
    )
    return pl.pallas_call(
        functools.partial(_moe_kernel, n_tok=t),
        grid_spec=grid_spec,
        out_shape=jax.ShapeDtypeStruct((dst.size, ROW_TILES, LANES), F32),
        compiler_params=_cparams(("arbitrary",)),
        name="moe",
    )(tile_group, dst, xn, w_sorted, wg, wu, wd)


PLE_TM = 512


def _ple_kernel(h_ref, y_ref, p_ref, wg_ref, wp_ref, gn_ref, out_ref, *, final_norm):
    h = h_ref[...] + jnp.concatenate([y_ref[:, c, :] for c in range(ROW_TILES)], axis=1)
    gate = _sigmoid(_dot(h.astype(BF16), wg_ref[...]))
    out = h + gate * _dot(p_ref[...].astype(BF16), wp_ref[...])
    if final_norm:
        out = _rms(out, gn_ref[...])
    out_ref[...] = out


def _ple(h, y, p, w_gate, w_proj, gain, final_norm):
    t = h.shape[0]
    return pl.pallas_call(
        functools.partial(_ple_kernel, final_norm=final_norm),
        grid=(t // PLE_TM,),
        in_specs=[
            pl.BlockSpec((PLE_TM, D_MODEL), lambda i: (i, 0)),
            pl.BlockSpec((PLE_TM, ROW_TILES, LANES), lambda i: (i, 0, 0)),
            pl.BlockSpec((PLE_TM, PLE_DIM), lambda i: (i, 0)),
            pl.BlockSpec((D_MODEL, D_MODEL), lambda i: (0, 0)),
            pl.BlockSpec((PLE_DIM, D_MODEL), lambda i: (0, 0)),
            pl.BlockSpec((1, D_MODEL), lambda i: (0, 0)),
        ],
        out_specs=pl.BlockSpec((PLE_TM, D_MODEL), lambda i: (i, 0)),
        out_shape=jax.ShapeDtypeStruct((t, D_MODEL), F32),
        compiler_params=_cparams(("parallel",)),
        name="ple",
    )(h, y, p, w_gate, w_proj, gain)


def _pad_lanes(x):
    return jnp.pad(x, ((0, 0), (0, LANES - x.shape[-1])))


def _row(x):
    return x.reshape(1, -1).astype(F32)


def kernel(x, p, norm_mix, norm_ffn, norm_final, ab_w_in, ab_w_out, a_lam_q1, a_lam_k1, a_lam_q2, a_lam_k2,
           a_subln, b_conv, b_igate_bias, b_fgate_bias, b_norm, cd_w_in, cd_w_out, c_conv, c_a_log, c_dt_bias,
           c_norm, d_fgate_bias, moe_w_group, moe_b_group, moe_w_router, moe_b_router, moe_w_gate, moe_w_up,
           moe_w_down, ple_w_gate, ple_w_proj):
    batch, seq, _ = x.shape
    depth = p.shape[0]
    t = batch * seq
    assert seq % IN_TM == 0 and seq % ATT_TK == 0 and t % MOE_TM == 0
    h = x.reshape(t, D_MODEL)
    cos, sin = _rope_tables(seq)
    ones_tab = jnp.ones((seq, LANES), F32)

    for i in range(depth):
        j = i // 2
        if i % 2 == 0:
            w_in = ab_w_in[j]
            z, zg = _inproj(h, _row(norm_mix[i]), w_in[:, :Z_MAIN].astype(BF16),
                            _pad_lanes(w_in[:, Z_MAIN:]).astype(BF16), cos, sin, seq, rope_chunks=2)
            lam_init = 0.8 - 0.6 * math.exp(-0.3 * i)
            lam_p = jnp.stack([a_lam_q1[j], a_lam_k1[j], a_lam_q2[j], a_lam_k2[j]]).astype(F32)
            out_1 = _attn_a(z, lam_p, a_subln[j].reshape(HEAD_DIM, 1).astype(F32), batch, seq, lam_init)
            gate_bias = _pad_lanes(jnp.concatenate([b_igate_bias[j], b_fgate_bias[j]]).reshape(1, -1))
            out_2 = _mlstm(z, zg, b_conv[j], gate_bias, _row(b_norm[j]), batch, seq)
            w_out = ab_w_out[j]
        else:
            w_in = cd_w_in[j]
            c_main, d_main = 4 * MIX_W, 3 * MIX_W
            w_main = jnp.concatenate([w_in[:, :c_main], w_in[:, c_main + 8:c_main + 8 + d_main]], axis=1)
            w_small = jnp.concatenate([w_in[:, c_main:c_main + 8], w_in[:, c_main + 8 + d_main:]], axis=1)
            z, zg = _inproj(h, _row(norm_mix[i]), w_main.astype(BF16), _pad_lanes(w_small).astype(BF16),
                            ones_tab, ones_tab, seq, rope_chunks=0)
            params = jnp.stack([_pad_lanes(c_a_log[j].reshape(1, -1))[0], _pad_lanes(c_dt_bias[j].reshape(1, -1))[0]])
            out_1 = _deltanet(z, zg, c_conv[j], params, _row(c_norm[j]), batch, seq)
            fd = zg[:, 8:8 + N_HEADS].reshape(batch, seq, N_HEADS).transpose(0, 2, 1)
            bias_rows = jnp.repeat(jnp.tile(d_fgate_bias[j], batch), seq // LANES).reshape(-1, 1)
            cum = _logf_cumsum(fd.reshape(-1, LANES), jnp.broadcast_to(bias_rows, (bias_rows.shape[0], LANES)),
                               seq // LANES)
            kx, qx = _forget_bias_operands(cum.reshape(batch * N_HEADS, seq))
            out_2 = _attn_d(z, kx, qx, batch, seq)
            w_out = cd_w_out[j]

        w_r = _pad_lanes(jnp.concatenate([moe_w_router[i], moe_w_group[i]], axis=1))
        w_r_hi_f = _bf16_part(w_r)
        w_r_hi = w_r_hi_f.astype(BF16)
        w_r_lo = (w_r - w_r_hi_f).astype(BF16)
        b_r = _pad_lanes(jnp.concatenate([moe_b_router[i], moe_b_group[i]]).reshape(1, -1))
        h, xn, comb = _outproj_router(h, out_1, out_2, w_out.astype(BF16), _row(norm_ffn[i]), w_r_hi, w_r_lo, b_r)
        y = _moe(xn, comb, moe_w_gate[i].astype(BF16), moe_w_up[i].astype(BF16), moe_w_down[i].astype(BF16))
        h = _ple(h, y, p[i].reshape(t, PLE_DIM), ple_w_gate[i].astype(BF16), ple_w_proj[i].astype(BF16),
                 _row(norm_final), final_norm=(i == depth - 1))
    return h.reshape(batch, seq, D_MODEL)
```

```python
import functools
import math

import jax
import jax.numpy as jnp
from jax import lax
from jax.experimental import pallas as pl
from jax.experimental.pallas import tpu as pltpu

F32 = jnp.float32
BF16 = jnp.bfloat16
HIGHEST = lax.Precision.HIGHEST

D_MODEL = 1024
HEAD_DIM = 128
N_HEADS = 4
MIX_W = N_HEADS * HEAD_DIM
A_HALF = HEAD_DIM // 2
CHUNK = 64
CONV_W = 4
RMS_EPS = 1e-6
PLE_DIM = 256
N_GROUPS = 4
EXPERTS_PER_GROUP = 4
N_EXPERTS = 16
EXPERT_FF = D_MODEL // 2
ROPE_THETA = 10000.0
Z_MAIN = 7 * MIX_W
LANES = 128

VMEM_LIMIT = 56 * 1024 * 1024

NEG_INF = float("-inf")


def _cparams(sem):
    return pltpu.CompilerParams(dimension_semantics=sem, vmem_limit_bytes=VMEM_LIMIT)


def _dot(a, b):
    return jnp.dot(a, b, preferred_element_type=F32)


def _dot_nt(a, b):
    return lax.dot_general(a, b, (((1,), (1,)), ((), ())), preferred_element_type=F32)


def _dot_exact(a, b):
    return jnp.dot(a, b, preferred_element_type=F32, precision=HIGHEST)


def _bf16_part(x):
    bits = lax.bitcast_convert_type(x, jnp.int32) & jnp.int32(-65536)
    return lax.bitcast_convert_type(bits, F32)


def _sigmoid(x):
    return 1.0 / (1.0 + jnp.exp(-x))


def _silu(x):
    return x * _sigmoid(x)


def _log_sigmoid(x):
    return jnp.minimum(x, 0.0) - jnp.log(1.0 + jnp.exp(-jnp.abs(x)))


def _softplus(x):
    return jnp.maximum(x, 0.0) + jnp.log(1.0 + jnp.exp(-jnp.abs(x)))


def _rms(x, gain):
    return x * lax.rsqrt(jnp.mean(x * x, axis=-1, keepdims=True) + RMS_EPS) * gain


IN_TM = 512
IN_TN = 512


def _inproj_kernel(h_ref, g_ref, w_ref, wg_ref, cos_ref, sin_ref, z_ref, zg_ref, *, rope_chunks):
    hn = _rms(h_ref[...], g_ref[...])
    hb = hn.astype(BF16)
    zg_ref[...] = _dot(hb, wg_ref[...])
    if rope_chunks:
        lane = lax.broadcasted_iota(jnp.int32, (IN_TM, IN_TN), 1)
        first_half = (lane & 32) == 0
        cos = jnp.concatenate([cos_ref[...]] * (IN_TN // LANES), axis=1)
        sin = jnp.concatenate([sin_ref[...]] * (IN_TN // LANES), axis=1)
    for c in range(Z_MAIN // IN_TN):
        zc = _dot(hb, w_ref[:, c * IN_TN:(c + 1) * IN_TN])
        if c < rope_chunks:
            partner = jnp.where(first_half, pltpu.roll(zc, IN_TN - 32, axis=1), pltpu.roll(zc, 32, axis=1))
            zc = zc * cos + partner * sin
        z_ref[:, c * IN_TN:(c + 1) * IN_TN] = zc.astype(BF16)


def _inproj(h, gain, w_main, w_gate, cos, sin, seq, rope_chunks):
    t = h.shape[0]
    per_seq = seq // IN_TM
    return pl.pallas_call(
        functools.partial(_inproj_kernel, rope_chunks=rope_chunks),
        grid=(t // IN_TM,),
        in_specs=[
            pl.BlockSpec((IN_TM, D_MODEL), lambda i: (i, 0)),
            pl.BlockSpec((1, D_MODEL), lambda i: (0, 0)),
            pl.BlockSpec((D_MODEL, Z_MAIN), lambda i: (0, 0)),
            pl.BlockSpec((D_MODEL, LANES), lambda i: (0, 0)),
            pl.BlockSpec((IN_TM, LANES), lambda i: (i % per_seq, 0)),
            pl.BlockSpec((IN_TM, LANES), lambda i: (i % per_seq, 0)),
        ],
        out_specs=[
            pl.BlockSpec((IN_TM, Z_MAIN), lambda i: (i, 0)),
            pl.BlockSpec((IN_TM, LANES), lambda i: (i, 0)),
        ],
        out_shape=[jax.ShapeDtypeStruct((t, Z_MAIN), BF16), jax.ShapeDtypeStruct((t, LANES), F32)],
        compiler_params=_cparams(("parallel",)),
        name="inproj",
    )(h, gain, w_main, w_gate, cos, sin)


def _rope_tables(seq):
    inv = ROPE_THETA ** (-jnp.arange(0, A_HALF, 2, dtype=F32) / A_HALF)
    ang = jnp.arange(seq, dtype=F32)[:, None] * inv[None, :]
    cos, sin = jnp.cos(ang), jnp.sin(ang)
    return jnp.tile(cos, (1, 4)), jnp.tile(jnp.concatenate([-sin, sin], axis=1), (1, 2))


ATT_W = 512
ATT_TK = 512
LOG2E = 1.4426950408889634


def _attn_init(v_ref, vt_ref, m_ref, l_ref, acc_ref):
    @pl.when(pl.program_id(2) == 0)
    def _():
        def body(j, carry):
            off = pl.multiple_of(j * ATT_TK, ATT_TK)
            vt_ref[j] = v_ref[pl.ds(off, ATT_TK), :].astype(F32).T.astype(BF16)
            return carry
        lax.fori_loop(0, vt_ref.shape[0], body, 0)

    m_ref[...] = jnp.full(m_ref.shape, NEG_INF, F32)
    l_ref[...] = jnp.zeros(l_ref.shape, F32)
    acc_ref[...] = jnp.zeros(acc_ref.shape, F32)


def _attn_logits(k_ref, q_t, s_ref, blk, mask):
    off = pl.multiple_of(blk * ATT_TK, ATT_TK)
    s = _dot(k_ref[pl.ds(off, ATT_TK), :], q_t)
    if mask is not None:
        s = jnp.where(mask, s, NEG_INF)
    s_ref[...] = s


def _attn_consume(s_ref, vt_ref, blk, m_ref, l_ref, acc_ref):
    s = s_ref[...]
    m_prev = m_ref[...]
    m_new = jnp.maximum(m_prev, jnp.max(s, axis=0, keepdims=True))
    alpha = jnp.exp2(m_prev - m_new)
    p = jnp.exp2(s - m_new)
    l_ref[...] = alpha * l_ref[...] + jnp.sum(p, axis=0, keepdims=True)
    acc_ref[...] = alpha * acc_ref[...] + _dot(vt_ref[blk], p.astype(BF16))
    m_ref[...] = m_new


def _attn_pipeline(k_ref, vt_ref, q_t, s0_ref, s1_ref, m_ref, l_ref, acc_ref, n_full, diag_blk, diag_mask):
    def consume(s_ref, blk):
        _attn_consume(s_ref, vt_ref, blk, m_ref, l_ref, acc_ref)

    _attn_logits(k_ref, q_t, s0_ref, diag_blk, diag_mask)

    def body(t, carry):
        _attn_logits(k_ref, q_t, s1_ref, 2 * t, None)
        consume(s0_ref, jnp.where(t == 0, diag_blk, 2 * t - 1))
        _attn_logits(k_ref, q_t, s0_ref, 2 * t + 1, None)
        consume(s1_ref, 2 * t)
        return carry

    npairs = n_full // 2
    lax.fori_loop(0, npairs, body, 0)
    in_s0 = jnp.where(npairs == 0, diag_blk, 2 * npairs - 1)

    @pl.when(n_full % 2 == 1)
    def _():
        _attn_logits(k_ref, q_t, s1_ref, n_full - 1, None)
        consume(s0_ref, in_s0)
        consume(s1_ref, n_full - 1)

    @pl.when(n_full % 2 == 0)
    def _():
        consume(s0_ref, in_s0)


def _attn_a_kernel(q_ref, k_ref, v_ref, lam_ref, gain_ref, o_ref, vt_ref, s0_ref, s1_ref, m_ref, l_ref, acc_ref,
                   *, lam_init):
    _attn_init(v_ref, vt_ref, m_ref, l_ref, acc_ref)
    i = pl.program_id(2)
    tq = ATT_W
    q = (q_ref[...].astype(F32) * (A_HALF ** -0.5 * LOG2E)).T
    dim = lax.broadcasted_iota(jnp.int32, q.shape, 0)
    q_t = jnp.concatenate([jnp.where(dim < A_HALF, q, 0.0), jnp.where(dim >= A_HALF, q, 0.0)], axis=1).astype(BF16)
    key = lax.broadcasted_iota(jnp.int32, (ATT_TK, 2 * tq), 0)
    qry = lax.broadcasted_iota(jnp.int32, (ATT_TK, 2 * tq), 1) & (tq - 1)
    _attn_pipeline(k_ref, vt_ref, q_t, s0_ref, s1_ref, m_ref, l_ref, acc_ref, i, i, (key >> 6) <= (qry >> 6))

    lam_p = lam_ref[...]
    lam = (jnp.exp(jnp.sum(lam_p[0:1] * lam_p[1:2], axis=-1, keepdims=True))
           - jnp.exp(jnp.sum(lam_p[2:3] * lam_p[3:4], axis=-1, keepdims=True)) + lam_init)
    o_all = acc_ref[...] * (1.0 / l_ref[...])
    out = o_all[:, 0:tq] - lam * o_all[:, tq:2 * tq]
    out = out * lax.rsqrt(jnp.mean(out * out, axis=0, keepdims=True) + RMS_EPS) * (gain_ref[...] * (1.0 - lam_init))
    o_ref[...] = out.T.astype(BF16)


def _attn_scratch(seq, cols):
    return [pltpu.VMEM((seq // ATT_TK, HEAD_DIM, ATT_TK), BF16), pltpu.VMEM((ATT_TK, cols), F32),
            pltpu.VMEM((ATT_TK, cols), F32), pltpu.VMEM((1, cols), F32), pltpu.VMEM((1, cols), F32),
            pltpu.VMEM((HEAD_DIM, cols), F32)]


def _attn_a(z, lam_p, gain_col, batch, seq, lam_init):
    t = batch * seq
    nq = seq // ATT_W
    return pl.pallas_call(
        functools.partial(_attn_a_kernel, lam_init=lam_init),
        grid=(batch, N_HEADS, nq),
        in_specs=[
            pl.BlockSpec((ATT_W, HEAD_DIM), lambda b, h, i: (b * nq + i, h)),
            pl.BlockSpec((seq, HEAD_DIM), lambda b, h, i: (b, N_HEADS + h)),
            pl.BlockSpec((seq, HEAD_DIM), lambda b, h, i: (b, 2 * N_HEADS + h)),
            pl.BlockSpec((4, A_HALF), lambda b, h, i: (0, 0)),
            pl.BlockSpec((HEAD_DIM, 1), lambda b, h, i: (0, 0)),
        ],
        out_specs=pl.BlockSpec((ATT_W, HEAD_DIM), lambda b, h, i: (b * nq + i, h)),
        out_shape=jax.ShapeDtypeStruct((t, MIX_W), BF16),
        scratch_shapes=_attn_scratch(seq, 2 * ATT_W),
        compiler_params=_cparams(("parallel", "parallel", "arbitrary")),
        name="attn_a",
    )(z, z, z, lam_p, gain_col)


D_AUG = 2 * HEAD_DIM
BIAS_ROWS = 16


def _attn_d_kernel(q_ref, k_ref, v_ref, kx_ref, qx_ref, o_ref, kaug_ref, vt_ref, s0_ref, s1_ref, m_ref, l_ref,
                   acc_ref):
    _attn_init(v_ref, vt_ref, m_ref, l_ref, acc_ref)
    i = pl.program_id(2)

    @pl.when(i == 0)
    def _():
        def body(j, carry):
            off = pl.multiple_of(j * ATT_TK, ATT_TK)
            kaug_ref[pl.ds(off, ATT_TK), 0:HEAD_DIM] = k_ref[pl.ds(off, ATT_TK), :]
            kaug_ref[pl.ds(off, ATT_TK), HEAD_DIM:D_AUG] = kx_ref[0, pl.ds(off, ATT_TK), :]
            return carry
        lax.fori_loop(0, vt_ref.shape[0], body, 0)

    q = (q_ref[...].astype(F32) * (HEAD_DIM ** -0.5 * LOG2E)).T.astype(BF16)
    q_t = jnp.concatenate([q, qx_ref[0], jnp.zeros((D_AUG - HEAD_DIM - BIAS_ROWS, ATT_W), BF16)], axis=0)
    key = lax.broadcasted_iota(jnp.int32, (ATT_TK, ATT_W), 0)
    qry = lax.broadcasted_iota(jnp.int32, (ATT_TK, ATT_W), 1)
    _attn_pipeline(kaug_ref, vt_ref, q_t, s0_ref, s1_ref, m_ref, l_ref, acc_ref, i, i, key <= qry)
    o_ref[...] = (acc_ref[...] * (1.0 / l_ref[...])).T.astype(BF16)


def _attn_d(z, kx, qx, batch, seq):
    t = batch * seq
    nq = seq // ATT_W
    base = 4 * N_HEADS
    return pl.pallas_call(
        _attn_d_kernel,
        grid=(batch, N_HEADS, nq),
        in_specs=[
            pl.BlockSpec((ATT_W, HEAD_DIM), lambda b, h, i: (b * nq + i, base + h)),
            pl.BlockSpec((seq, HEAD_DIM), lambda b, h, i: (b, base + N_HEADS + h)),
            pl.BlockSpec((seq, HEAD_DIM), lambda b, h, i: (b, base + 2 * N_HEADS + h)),
            pl.BlockSpec((1, seq, HEAD_DIM), lambda b, h, i: (b * N_HEADS + h, 0, 0)),
            pl.BlockSpec((1, BIAS_ROWS, ATT_W), lambda b, h, i: (b * N_HEADS + h, 0, i)),
        ],
        out_specs=pl.BlockSpec((ATT_W, HEAD_DIM), lambda b, h, i: (b * nq + i, h)),
        out_shape=jax.ShapeDtypeStruct((t, MIX_W), BF16),
        scratch_shapes=[pltpu.VMEM((seq, D_AUG), BF16)] + _attn_scratch(seq, ATT_W),
        compiler_params=_cparams(("parallel", "parallel", "arbitrary")),
        name="attn_d",
    )(z, z, z, kx, qx)


def _forget_bias_operands(cum):
    c = cum * LOG2E
    hi_f = _bf16_part(c)
    mid_f = _bf16_part(c - hi_f)
    hi, mid, lo = hi_f.astype(BF16), mid_f.astype(BF16), (c - hi_f - mid_f).astype(BF16)
    one = jnp.ones_like(hi)
    kx = jnp.stack([hi, mid, lo, one, one, one], axis=-1)
    kx = jnp.pad(kx, ((0, 0), (0, 0), (0, HEAD_DIM - 6)))
    qx = jnp.stack([-one, -one, -one, hi, mid, lo], axis=1)
    qx = jnp.pad(qx, ((0, 0), (0, BIAS_ROWS - 6), (0, 0)))
    return kx, qx


def _logf_cumsum_kernel(x_ref, b_ref, o_ref, *, rows_per_seq):
    r = x_ref.shape[0]
    lf = _log_sigmoid(x_ref[...] + b_ref[...])
    a = lax.broadcasted_iota(jnp.int32, (LANES, LANES), 0)
    b = lax.broadcasted_iota(jnp.int32, (LANES, LANES), 1)
    within = _dot_exact(lf, (a <= b).astype(F32))
    totals = _dot_exact(lf, jnp.ones((LANES, LANES), F32))
    ra = lax.broadcasted_iota(jnp.int32, (r, r), 0)
    rb = lax.broadcasted_iota(jnp.int32, (r, r), 1)
    earlier = ((rb < ra) & ((rb // rows_per_seq) == (ra // rows_per_seq))).astype(F32)
    o_ref[...] = within + _dot_exact(earlier, totals)


def _logf_cumsum(x, bias_rows, rows_per_seq):
    r = x.shape[0]
    return pl.pallas_call(
        functools.partial(_logf_cumsum_kernel, rows_per_seq=rows_per_seq),
        out_shape=jax.ShapeDtypeStruct((r, LANES), F32),
        compiler_params=pltpu.CompilerParams(vmem_limit_bytes=VMEM_LIMIT),
        name="logf_cumsum",
    )(x, bias_rows)


CONV_TAIL = 8


def _causal_conv_silu(x, prev_tail, w):
    row = lax.broadcasted_iota(jnp.int32, prev_tail.shape, 0)
    acc = x * w[CONV_W - 1:CONV_W, :]
    for s in range(1, CONV_W):
        xs = pltpu.roll(x, s, axis=0)
        head = jnp.where(row >= s, xs[0:CONV_TAIL], pltpu.roll(prev_tail, s, axis=0))
        xs = jnp.concatenate([head, xs[CONV_TAIL:]], axis=0)
        acc = acc + xs * w[CONV_W - 1 - s:CONV_W - s, :]
    return _silu(acc)


def _chunk_masks():
    r = lax.broadcasted_iota(jnp.int32, (CHUNK, CHUNK), 0)
    c = lax.broadcasted_iota(jnp.int32, (CHUNK, CHUNK), 1)
    return r, c


REC_NB_MLSTM = 2
REC_NB_DELTANET = 4


def _rec_batch_block(batch, nb):
    return nb if batch % nb == 0 else 1


def _rec_specs(nb, cols):
    main = [pl.BlockSpec((nb, CHUNK, MIX_W), lambda b, c, j=j: (b, c, j)) for j in cols]
    return main + [pl.BlockSpec((nb, CHUNK, LANES), lambda b, c: (b, c, 0))]


def _mlstm_kernel(q_ref, k_ref, v_ref, og_ref, g_ref, conv_ref, bias_ref, norm_ref, out_ref,
                  pq_ref, pk_ref, c_ref, n_ref, m_ref):
    @pl.when(pl.program_id(1) == 0)
    def _():
        pq_ref[...] = jnp.zeros(pq_ref.shape, F32)
        pk_ref[...] = jnp.zeros(pk_ref.shape, F32)
        c_ref[...] = jnp.zeros(c_ref.shape, F32)
        n_ref[...] = jnp.zeros(n_ref.shape, F32)
        m_ref[...] = jnp.zeros(m_ref.shape, F32)

    lane = lax.broadcasted_iota(jnp.int32, (CHUNK, LANES), 1)
    r, c = _chunk_masks()
    causal = c <= r
    tri = causal.astype(F32)
    units = []
    for bi in range(q_ref.shape[0]):
        xq = q_ref[bi].astype(F32)
        xk = k_ref[bi].astype(F32)
        q_all = _causal_conv_silu(xq, pq_ref[bi], conv_ref[:, 0:MIX_W])
        k_all = _causal_conv_silu(xk, pk_ref[bi], conv_ref[:, MIX_W:2 * MIX_W]) * (HEAD_DIM ** -0.5)
        pq_ref[bi] = xq[CHUNK - CONV_TAIL:CHUNK]
        pk_ref[bi] = xk[CHUNK - CONV_TAIL:CHUNK]
        pre = g_ref[bi] + bias_ref[...]
        e = jnp.where(lane < N_HEADS, pre, _log_sigmoid(pre))
        cum = _dot_exact(tri, e)
        e_t = e.T
        cum_t = cum.T
        for h in range(N_HEADS):
            sl = slice(h * HEAD_DIM, (h + 1) * HEAD_DIM)
            u = dict(bi=bi, h=h, sl=sl, q=q_all[:, sl], k=k_all[:, sl], v=v_ref[bi, :, sl].astype(F32))
            u["qb"] = u["q"].astype(BF16)
            u["kb"] = u["k"].astype(BF16)
            i_col = e[:, h:h + 1]
            i_row = e_t[h:h + 1, :]
            b_col = cum[:, N_HEADS + h:N_HEADS + h + 1]
            b_row = cum_t[N_HEADS + h:N_HEADS + h + 1, :]
            b_last = b_col[CHUNK - 1:CHUNK, :]
            m_st = m_ref[bi, h:h + 1, 0:1]
            u["c_st"] = c_ref[bi, h]
            u["n_st"] = n_ref[bi, h:h + 1, :]
            d_log = jnp.where(causal, b_col - b_row + i_row, NEG_INF)
            inter_log = b_col + m_st
            u["m_t"] = jnp.maximum(inter_log, jnp.max(d_log, axis=-1, keepdims=True))
            u["d_w"] = jnp.exp(d_log - u["m_t"])
            u["inter_w"] = jnp.exp(inter_log - u["m_t"])
            w_log = b_last - b_col + i_col
            u["m_new"] = jnp.maximum(b_last + m_st, jnp.max(w_log, axis=0, keepdims=True))
            u["sw"] = jnp.exp(w_log - u["m_new"])
            u["decay"] = jnp.exp(b_last + m_st - u["m_new"])
            units.append(u)

    for u in units:
        u["qk"] = _dot_nt(u["qb"], u["kb"])
    for u in units:
        u["qc"] = _dot_nt(u["qb"], u["c_st"].astype(BF16))
    for u in units:
        u["upd"] = _dot((u["v"] * u["sw"]).T.astype(BF16), u["kb"])
    for u in units:
        u["a"] = u["qk"] * u["d_w"]
        u["av"] = _dot(u["a"].astype(BF16), u["v"].astype(BF16))
    for u in units:
        bi, h, sl = u["bi"], u["h"], u["sl"]
        num = u["av"] + u["inter_w"] * u["qc"]
        den = (jnp.sum(u["a"], axis=-1, keepdims=True)
               + u["inter_w"] * jnp.sum(u["q"] * u["n_st"], axis=-1, keepdims=True))
        hh = num / jnp.maximum(jnp.abs(den), jnp.exp(-u["m_t"]))
        c_ref[bi, h] = u["decay"] * u["c_st"] + u["upd"]
        n_ref[bi, h:h + 1, :] = u["decay"] * u["n_st"] + jnp.sum(u["k"] * u["sw"], axis=0, keepdims=True)
        m_ref[bi, h:h + 1, :] = jnp.broadcast_to(u["m_new"], (1, LANES))
        gate = _sigmoid(og_ref[bi, :, sl].astype(F32))
        out_ref[bi, :, sl] = (_rms(hh, norm_ref[...]) * gate).astype(BF16)


def _mlstm(z, zg, conv_w, gate_bias, norm, batch, seq):
    nb = _rec_batch_block(batch, REC_NB_MLSTM)
    full = lambda s: pl.BlockSpec(s, lambda b, c: (0, 0))
    out = pl.pallas_call(
        _mlstm_kernel,
        grid=(batch // nb, seq // CHUNK),
        in_specs=_rec_specs(nb, (3, 4, 5, 6)) + [full((CONV_W, 2 * MIX_W)), full((1, LANES)), full((1, HEAD_DIM))],
        out_specs=pl.BlockSpec((nb, CHUNK, MIX_W), lambda b, c: (b, c, 0)),
        out_shape=jax.ShapeDtypeStruct((batch, seq, MIX_W), BF16),
        scratch_shapes=[pltpu.VMEM((nb, CONV_TAIL, MIX_W), F32), pltpu.VMEM((nb, CONV_TAIL, MIX_W), F32),
                        pltpu.VMEM((nb, N_HEADS, HEAD_DIM, HEAD_DIM), F32), pltpu.VMEM((nb, 8, HEAD_DIM), F32),
                        pltpu.VMEM((nb, 8, LANES), F32)],
        compiler_params=_cparams(("parallel", "arbitrary")),
        name="mlstm",
    )(*([z.reshape(batch, seq, Z_MAIN)] * 4), zg.reshape(batch, seq, LANES), conv_w, gate_bias, norm)
    return out.reshape(batch * seq, MIX_W)


def _bdot(a, b):
    return _dot(a.astype(BF16), b.astype(BF16))


def _unit_lower_inverses(l_stricts, r, c):
    same16 = (r >> 4) == (c >> 4)
    same32 = (r >> 5) == (c >> 5)
    eye = (r == c).astype(F32)
    ps = [jnp.where(same16, -l, 0.0) for l in l_stricts]
    invs = [eye + p for p in ps]
    for _ in range(3):
        ps = [_bdot(p, p) for p in ps]
        invs = [inv + _bdot(inv, p) for inv, p in zip(invs, ps)]
    for off_mask in (same32 & jnp.logical_not(same16), jnp.logical_not(same32)):
        ts = [_bdot(jnp.where(off_mask, l, 0.0), inv) for l, inv in zip(l_stricts, invs)]
        invs = [inv - _bdot(inv, t) for inv, t in zip(invs, ts)]
    return invs


def _deltanet_kernel(q_ref, k_ref, v_ref, og_ref, g_ref, conv_ref, par_ref, norm_ref, out_ref,
                     pq_ref, pk_ref, pv_ref, s_ref):
    @pl.when(pl.program_id(1) == 0)
    def _():
        pq_ref[...] = jnp.zeros(pq_ref.shape, F32)
        pk_ref[...] = jnp.zeros(pk_ref.shape, F32)
        pv_ref[...] = jnp.zeros(pv_ref.shape, F32)
        s_ref[...] = jnp.zeros(s_ref.shape, F32)

    lane = lax.broadcasted_iota(jnp.int32, (CHUNK, LANES), 1)
    r, c = _chunk_masks()
    tri = (c <= r).astype(F32)
    units = []
    for bi in range(q_ref.shape[0]):
        xq = q_ref[bi].astype(F32)
        xk = k_ref[bi].astype(F32)
        xv = v_ref[bi].astype(F32)
        q_all = _causal_conv_silu(xq, pq_ref[bi], conv_ref[:, 0:MIX_W])
        k_all = _causal_conv_silu(xk, pk_ref[bi], conv_ref[:, MIX_W:2 * MIX_W])
        v_all = _causal_conv_silu(xv, pv_ref[bi], conv_ref[:, 2 * MIX_W:3 * MIX_W])
        pq_ref[bi] = xq[CHUNK - CONV_TAIL:CHUNK]
        pk_ref[bi] = xk[CHUNK - CONV_TAIL:CHUNK]
        pv_ref[bi] = xv[CHUNK - CONV_TAIL:CHUNK]
        pre = g_ref[bi]
        e = jnp.where(lane < N_HEADS, -jnp.exp(par_ref[0:1, :]) * _softplus(pre + par_ref[1:2, :]), _sigmoid(pre))
        cum = _dot_exact(tri, e)
        cum_t = cum.T
        for h in range(N_HEADS):
            sl = slice(h * HEAD_DIM, (h + 1) * HEAD_DIM)
            q = q_all[:, sl]
            k = k_all[:, sl]
            q = q * lax.rsqrt(jnp.sum(q * q, axis=-1, keepdims=True) + RMS_EPS) * (HEAD_DIM ** -0.5)
            k = k * lax.rsqrt(jnp.sum(k * k, axis=-1, keepdims=True) + RMS_EPS)
            beta = e[:, N_HEADS + h:N_HEADS + h + 1]
            b_col = cum[:, h:h + 1]
            b_row = cum_t[h:h + 1, :]
            b_last = b_col[CHUNK - 1:CHUNK, :]
            e_col = jnp.exp(b_col)
            kbeta = k * beta
            units.append(dict(
                bi=bi, h=h, sl=sl, k=k, kbeta=kbeta, q_dec=q * e_col,
                gam=jnp.exp(jnp.where(c <= r, b_col - b_row, NEG_INF)),
                rhs=jnp.concatenate([v_all[:, sl] * beta, kbeta * e_col], axis=1),
                k_dec=k * jnp.exp(b_last - b_col), chunk_decay=jnp.exp(b_last),
                kq=jnp.concatenate([kbeta, q], axis=0)))

    for u in units:
        both = _dot_nt(u["kq"].astype(BF16), u["k"].astype(BF16)) * jnp.concatenate([u["gam"], u["gam"]], axis=0)
        u["l"] = jnp.where(c < r, both[0:CHUNK], 0.0)
        u["qk"] = both[CHUNK:2 * CHUNK]
    invs = _unit_lower_inverses([u["l"] for u in units], r, c)
    for u, inv in zip(units, invs):
        sol = _bdot(inv, u["rhs"])
        u["u"] = sol[:, 0:HEAD_DIM]
        u["w"] = sol[:, HEAD_DIM:2 * HEAD_DIM]
    for u in units:
        u["st"] = s_ref[u["bi"], u["h"]]
        u["ws"] = _bdot(jnp.concatenate([u["w"], u["q_dec"]], axis=0), u["st"])
    for u in units:
        u["v_new"] = u["u"] - u["ws"][0:CHUNK]
        u["o"] = u["ws"][CHUNK:2 * CHUNK] + _bdot(u["qk"], u["v_new"])
    for u in units:
        s_ref[u["bi"], u["h"]] = u["chunk_decay"] * u["st"] + _bdot(u["k_dec"].T, u["v_new"])
    for u in units:
        bi, sl = u["bi"], u["sl"]
        gate = _silu(og_ref[bi, :, sl].astype(F32))
        out_ref[bi, :, sl] = (_rms(u["o"], norm_ref[...]) * gate).astype(BF16)


def _deltanet(z, zg, conv_w, params, norm, batch, seq):
    nb = _rec_batch_block(batch, REC_NB_DELTANET)
    full = lambda s: pl.BlockSpec(s, lambda b, c: (0, 0))
    out = pl.pallas_call(
        _deltanet_kernel,
        grid=(batch // nb, seq // CHUNK),
        in_specs=_rec_specs(nb, (0, 1, 2, 3)) + [full((CONV_W, 3 * MIX_W)), full((2, LANES)), full((1, HEAD_DIM))],
        out_specs=pl.BlockSpec((nb, CHUNK, MIX_W), lambda b, c: (b, c, 0)),
        out_shape=jax.ShapeDtypeStruct((batch, seq, MIX_W), BF16),
        scratch_shapes=[pltpu.VMEM((nb, CONV_TAIL, MIX_W), F32)] * 3
        + [pltpu.VMEM((nb, N_HEADS, HEAD_DIM, HEAD_DIM), F32)],
        compiler_params=_cparams(("parallel", "arbitrary")),
        name="deltanet",
    )(*([z.reshape(batch, seq, Z_MAIN)] * 4), zg.reshape(batch, seq, LANES), conv_w, params, norm)
    return out.reshape(batch * seq, MIX_W)


OUT_TM = 512
G_LANE0 = N_EXPERTS
ROUTE_W_LANE = 4
ROW_TILES = D_MODEL // LANES


def _outproj_router_kernel(h_ref, a_ref, b_ref, wo_ref, gn_ref, whi_ref, wlo_ref, br_ref,
                           hout_ref, xn_ref, comb_ref):
    h_new = h_ref[...] + _dot(a_ref[...], wo_ref[0:MIX_W, :]) + _dot(b_ref[...], wo_ref[MIX_W:2 * MIX_W, :])
    hout_ref[...] = h_new
    hn = _rms(h_new, gn_ref[...])
    for c in range(ROW_TILES):
        xn_ref[:, c, :] = hn[:, c * LANES:(c + 1) * LANES]
    hi_f = _bf16_part(hn)
    hi = hi_f.astype(BF16)
    lo = (hn - hi_f).astype(BF16)
    logits = (_dot(hi, whi_ref[...]) + _dot(lo, whi_ref[...]) + _dot(hi, wlo_ref[...]) + _dot(lo, wlo_ref[...])
              + br_ref[...])

    lane_i = lax.broadcasted_iota(jnp.int32, logits.shape, 1)
    lane = lane_i.astype(F32)
    big = 1e9
    is_group = (lane_i >= G_LANE0) & (lane_i < G_LANE0 + N_GROUPS)
    gl = jnp.where(is_group, logits, NEG_INF)
    gmax = jnp.max(gl, axis=-1, keepdims=True)
    g_sel = jnp.min(jnp.where(gl == gmax, lane, big), axis=-1, keepdims=True) - G_LANE0
    p_top = 1.0 / jnp.sum(jnp.exp(gl - gmax), axis=-1, keepdims=True)
    group_of_lane = (lane_i >> 2).astype(F32)
    in_group = (lane_i < N_EXPERTS) & (group_of_lane == g_sel)
    el = jnp.where(in_group, logits, NEG_INF)
    v1 = jnp.max(el, axis=-1, keepdims=True)
    i1 = jnp.min(jnp.where(el == v1, lane, big), axis=-1, keepdims=True)
    el2 = jnp.where(lane == i1, NEG_INF, el)
    v2 = jnp.max(el2, axis=-1, keepdims=True)
    i2 = jnp.min(jnp.where(el2 == v2, lane, big), axis=-1, keepdims=True)
    e21 = jnp.exp(v2 - v1)
    w1 = p_top / (1.0 + e21)
    local = lane - ROUTE_W_LANE + EXPERTS_PER_GROUP * g_sel
    comb_ref[...] = (jnp.where(lane_i == 0, g_sel, 0.0) + jnp.where(local == i1, w1, 0.0)
                     + jnp.where(local == i2, w1 * e21, 0.0))


def _outproj_router(h, oa, ob, w_out, gain, wr_hi, wr_lo, b_r):
    t = h.shape[0]
    row = lambda w: pl.BlockSpec((OUT_TM, w), lambda i: (i, 0))
    full = lambda s: pl.BlockSpec(s, lambda i: (0, 0))
    return pl.pallas_call(
        _outproj_router_kernel,
        grid=(t // OUT_TM,),
        in_specs=[row(D_MODEL), row(MIX_W), row(MIX_W), full((D_MODEL, D_MODEL)), full((1, D_MODEL)),
                  full((D_MODEL, LANES)), full((D_MODEL, LANES)), full((1, LANES))],
        out_specs=[row(D_MODEL), pl.BlockSpec((OUT_TM, ROW_TILES, LANES), lambda i: (i, 0, 0)), row(LANES)],
        out_shape=[jax.ShapeDtypeStruct((t, D_MODEL), F32), jax.ShapeDtypeStruct((t, ROW_TILES, LANES), F32),
                   jax.ShapeDtypeStruct((t, LANES), F32)],
        compiler_params=_cparams(("parallel",)),
        name="outproj_router",
    )(h, oa, ob, w_out, gain, wr_hi, wr_lo, b_r)


MOE_TM = 512


def _moe_plan(route, t):
    n_rows = t + (N_GROUPS + 1) * MOE_TM
    n_tiles = n_rows // MOE_TM
    group = route[:, 0].astype(jnp.int32)
    onehot = (group[:, None] == jnp.arange(N_GROUPS, dtype=jnp.int32)[None, :]).astype(jnp.int32)
    csum = jnp.cumsum(onehot, axis=0)
    rank = jnp.sum(csum * onehot, axis=1) - 1
    padded = ((csum[-1] + MOE_TM - 1) // MOE_TM) * MOE_TM
    ends = jnp.cumsum(padded)
    pos = jnp.sum((ends - padded)[None, :] * onehot, axis=1) + rank
    record = jnp.concatenate([jnp.arange(1, t + 1, dtype=F32)[:, None],
                              route[:, ROUTE_W_LANE:ROUTE_W_LANE + EXPERTS_PER_GROUP]], axis=1)
    record = jnp.zeros((n_rows, 1 + EXPERTS_PER_GROUP), F32).at[pos].set(record, unique_indices=True)
    real = record[:, 0].astype(jnp.int32) - 1
    is_pad = real < 0
    dst = jnp.where(is_pad, t + jnp.cumsum(is_pad.astype(jnp.int32)) - 1, real)
    w_sorted = record[:, 1:]
    starts = jnp.arange(n_tiles, dtype=jnp.int32) * MOE_TM
    tile_group = jnp.minimum(jnp.sum((starts[:, None] >= ends[None, :]).astype(jnp.int32), axis=1), N_GROUPS - 1)
    return tile_group, dst.reshape(n_tiles, MOE_TM), w_sorted


def _moe_kernel(tg_ref, dst_ref, x_hbm, ws_ref, wg_ref, wu_ref, wd_ref, out_hbm,
                xbuf, ybuf, sem_g, sem_s, *, n_tok):
    t = pl.program_id(0)
    nt = pl.num_programs(0)
    slot = t % 2

    def row_gather(tile, buf, r):
        d = dst_ref[tile, r]
        tok = jnp.where(d < n_tok, d, 0)
        return pltpu.make_async_copy(x_hbm.at[tok], xbuf.at[buf, r], sem_g.at[buf])

    def row_scatter(tile, buf, r):
        return pltpu.make_async_copy(ybuf.at[buf, r], out_hbm.at[dst_ref[tile, r]], sem_s.at[buf])

    def wait_gather(buf):
        pltpu.make_async_copy(x_hbm.at[pl.ds(0, MOE_TM)], xbuf.at[buf], sem_g.at[buf]).wait()

    def wait_scatter(buf):
        pltpu.make_async_copy(ybuf.at[buf], out_hbm.at[pl.ds(0, MOE_TM)], sem_s.at[buf]).wait()

    @pl.when(t == 0)
    def _():
        def body(r, carry):
            row_gather(0, 0, r).start()
            return carry
        lax.fori_loop(0, MOE_TM, body, 0, unroll=8)
        ybuf[1] = jnp.zeros(ybuf.shape[1:], F32)

    wait_gather(slot)

    nxt = jnp.where(t == nt - 1, 0, t + 1)
    prv = jnp.where(t == 0, nt - 1, t - 1)
    other = 1 - slot
    rows_per_expert = MOE_TM // EXPERTS_PER_GROUP
    x = jnp.concatenate([xbuf[slot, :, c, :] for c in range(ROW_TILES)], axis=1).astype(BF16)
    ws = ws_ref[...]
    y = jnp.zeros((MOE_TM, D_MODEL), F32)
    for e in range(EXPERTS_PER_GROUP):
        for r in range(e * rows_per_expert, (e + 1) * rows_per_expert):
            row_gather(nxt, other, r).start(priority=r % 2)
            row_scatter(prv, other, r).start(priority=r % 2)
        he = _silu(_dot(x, wg_ref[e])) * _dot(x, wu_ref[e])
        y = y + _dot((he * ws[:, e:e + 1]).astype(BF16), wd_ref[e])
    for c in range(ROW_TILES):
        ybuf[slot, :, c, :] = y[:, c * LANES:(c + 1) * LANES]

    wait_scatter(other)

    @pl.when(t == nt - 1)
    def _():
        wait_gather(other)


def _moe(xn, route, wg, wu, wd):
    t = xn.shape[0]
    tile_group, dst, w_sorted = _moe_plan(route, t)
    n_tiles = dst.shape[0]
    buf = pltpu.VMEM((2, MOE_TM, ROW_TILES, LANES), F32)
    grid_spec = pltpu.PrefetchScalarGridSpec(
        num_scalar_prefetch=2,
        grid=(n_tiles,),
        in_specs=[
            pl.BlockSpec(memory_space=pl.ANY),
            pl.BlockSpec((MOE_TM, EXPERTS_PER_GROUP), lambda i, tg, dst: (i, 0)),
            pl.BlockSpec((EXPERTS_PER_GROUP, D_MODEL, EXPERT_FF), lambda i, tg, dst: (tg[i], 0, 0)),
            pl.BlockSpec((EXPERTS_PER_GROUP, D_MODEL, EXPERT_FF), lambda i, tg, dst: (tg[i], 0, 0)),
            pl.BlockSpec((EXPERTS_PER_GROUP, EXPERT_FF, D_MODEL), lambda i, tg, dst: (tg[i], 0, 0)),
        ],
        out_specs=pl.BlockSpec(memory_space=pl.ANY),
        scratch_shapes=[buf, buf, pltpu.SemaphoreType.DMA((2,)), pltpu.SemaphoreType.DMA((2,))],
    )
    return pl.pallas_call(
        functools.partial(_moe_kernel, n_tok=t),
        grid_spec=grid_spec,
        out_shape=jax.ShapeDtypeStruct((dst.size, ROW_TILES, LANES), F32),
        compiler_params=_cparams(("arbitrary",)),
        name="moe",
    )(tile_group, dst, xn, w_sorted, wg, wu, wd)


PLE_TM = 512


def _ple_kernel(h_ref, y_ref, p_ref, wg_ref, wp_ref, gn_ref, out_ref, *, final_norm):
    h = h_ref[...] + jnp.concatenate([y_ref[:, c, :] for c in range(ROW_TILES)], axis=1)
    gate = _sigmoid(_dot(h.astype(BF16), wg_ref[...]))
    out = h + gate * _dot(p_ref[...].astype(BF16), wp_ref[...])
    if final_norm:
        out = _rms(out, gn_ref[...])
    out_ref[...] = out


def _ple(h, y, p, w_gate, w_proj, gain, final_norm):
    t = h.shape[0]
    return pl.pallas_call(
        functools.partial(_ple_kernel, final_norm=final_norm),
        grid=(t // PLE_TM,),
        in_specs=[
            pl.BlockSpec((PLE_TM, D_MODEL), lambda i: (i, 0)),
            pl.BlockSpec((PLE_TM, ROW_TILES, LANES), lambda i: (i, 0, 0)),
            pl.BlockSpec((PLE_TM, PLE_DIM), lambda i: (i, 0)),
            pl.BlockSpec((D_MODEL, D_MODEL), lambda i: (0, 0)),
            pl.BlockSpec((PLE_DIM, D_MODEL), lambda i: (0, 0)),
            pl.BlockSpec((1, D_MODEL), lambda i: (0, 0)),
        ],
        out_specs=pl.BlockSpec((PLE_TM, D_MODEL), lambda i: (i, 0)),
        out_shape=jax.ShapeDtypeStruct((t, D_MODEL), F32),
        compiler_params=_cparams(("parallel",)),
        name="ple",
    )(h, y, p, w_gate, w_proj, gain)


def _pad_lanes(x):
    return jnp.pad(x, ((0, 0), (0, LANES - x.shape[-1])))


def _row(x):
    return x.reshape(1, -1).astype(F32)


def kernel(x, p, norm_mix, norm_ffn, norm_final, ab_w_in, ab_w_out, a_lam_q1, a_lam_k1, a_lam_q2, a_lam_k2,
           a_subln, b_conv, b_igate_bias, b_fgate_bias, b_norm, cd_w_in, cd_w_out, c_conv, c_a_log, c_dt_bias,
           c_norm, d_fgate_bias, moe_w_group, moe_b_group, moe_w_router, moe_b_router, moe_w_gate, moe_w_up,
           moe_w_down, ple_w_gate, ple_w_proj):
    batch, seq, _ = x.shape
    depth = p.shape[0]
    t = batch * seq
    assert seq % IN_TM == 0 and seq % ATT_TK == 0 and t % MOE_TM == 0
    h = x.reshape(t, D_MODEL)
    cos, sin = _rope_tables(seq)
    ones_tab = jnp.ones((seq, LANES), F32)

    for i in range(depth):
        j = i // 2
        if i % 2 == 0:
            w_in = ab_w_in[j]
            z, zg = _inproj(h, _row(norm_mix[i]), w_in[:, :Z_MAIN].astype(BF16),
                            _pad_lanes(w_in[:, Z_MAIN:]).astype(BF16), cos, sin, seq, rope_chunks=2)
            lam_init = 0.8 - 0.6 * math.exp(-0.3 * i)
            lam_p = jnp.stack([a_lam_q1[j], a_lam_k1[j], a_lam_q2[j], a_lam_k2[j]]).astype(F32)
            out_1 = _attn_a(z, lam_p, a_subln[j].reshape(HEAD_DIM, 1).astype(F32), batch, seq, lam_init)
            gate_bias = _pad_lanes(jnp.concatenate([b_igate_bias[j], b_fgate_bias[j]]).reshape(1, -1))
            out_2 = _mlstm(z, zg, b_conv[j], gate_bias, _row(b_norm[j]), batch, seq)
            w_out = ab_w_out[j]
        else:
            w_in = cd_w_in[j]
            c_main, d_main = 4 * MIX_W, 3 * MIX_W
            w_main = jnp.concatenate([w_in[:, :c_main], w_in[:, c_main + 8:c_main + 8 + d_main]], axis=1)
            w_small = jnp.concatenate([w_in[:, c_main:c_main + 8], w_in[:, c_main + 8 + d_main:]], axis=1)
            z, zg = _inproj(h, _row(norm_mix[i]), w_main.astype(BF16), _pad_lanes(w_small).astype(BF16),
                            ones_tab, ones_tab, seq, rope_chunks=0)
            params = jnp.stack([_pad_lanes(c_a_log[j].reshape(1, -1))[0], _pad_lanes(c_dt_bias[j].reshape(1, -1))[0]])
            out_1 = _deltanet(z, zg, c_conv[j], params, _row(c_norm[j]), batch, seq)
            fd = zg[:, 8:8 + N_HEADS].reshape(batch, seq, N_HEADS).transpose(0, 2, 1)
            bias_rows = jnp.repeat(jnp.tile(d_fgate_bias[j], batch), seq // LANES).reshape(-1, 1)
            cum = _logf_cumsum(fd.reshape(-1, LANES), jnp.broadcast_to(bias_rows, (bias_rows.shape[0], LANES)),
                               seq // LANES)
            kx, qx = _forget_bias_operands(cum.reshape(batch * N_HEADS, seq))
            out_2 = _attn_d(z, kx, qx, batch, seq)
            w_out = cd_w_out[j]

        w_r = _pad_lanes(jnp.concatenate([moe_w_router[i], moe_w_group[i]], axis=1))
        w_r_hi_f = _bf16_part(w_r)
        w_r_hi = w_r_hi_f.astype(BF16)
        w_r_lo = (w_r - w_r_hi_f).astype(BF16)
        b_r = _pad_lanes(jnp.concatenate([moe_b_router[i], moe_b_group[i]]).reshape(1, -1))
        h, xn, comb = _outproj_router(h, out_1, out_2, w_out.astype(BF16), _row(norm_ffn[i]), w_r_hi, w_r_lo, b_r)
        y = _moe(xn, comb, moe_w_gate[i].astype(BF16), moe_w_up[i].astype(BF16), moe_w_down[i].astype(BF16))
        h = _ple(h, y, p[i].reshape(t, PLE_DIM), ple_w_gate[i].astype(BF16), ple_w_proj[i].astype(BF16),
                 _row(norm_final), final_norm=(i == depth - 1))
    return h.reshape(batch, seq, D_MODEL)
```

```python
import functools
import math

import jax
import jax.numpy as jnp
from jax import lax
from jax.experimental import pallas as pl
from jax.experimental.pallas import tpu as pltpu

F32 = jnp.float32
BF16 = jnp.bfloat16
HIGHEST = lax.Precision.HIGHEST

D_MODEL = 1024
HEAD_DIM = 128
N_HEADS = 4
MIX_W = N_HEADS * HEAD_DIM
A_HALF = HEAD_DIM // 2
CHUNK = 64
CONV_W = 4
RMS_EPS = 1e-6
PLE_DIM = 256
N_GROUPS = 4
EXPERTS_PER_GROUP = 4
N_EXPERTS = 16
EXPERT_FF = D_MODEL // 2
ROPE_THETA = 10000.0
Z_MAIN = 7 * MIX_W
LANES = 128

VMEM_LIMIT = 56 * 1024 * 1024

NEG_INF = float("-inf")


def _cparams(sem):
    return pltpu.CompilerParams(dimension_semantics=sem, vmem_limit_bytes=VMEM_LIMIT)


def _dot(a, b):
    return jnp.dot(a, b, preferred_element_type=F32)


def _dot_nt(a, b):
    return lax.dot_general(a, b, (((1,), (1,)), ((), ())), preferred_element_type=F32)


def _dot_exact(a, b):
    return jnp.dot(a, b, preferred_element_type=F32, precision=HIGHEST)


def _bf16_part(x):
    bits = lax.bitcast_convert_type(x, jnp.int32) & jnp.int32(-65536)
    return lax.bitcast_convert_type(bits, F32)


def _sigmoid(x):
    return 1.0 / (1.0 + jnp.exp(-x))


def _silu(x):
    return x * _sigmoid(x)


def _log_sigmoid(x):
    return jnp.minimum(x, 0.0) - jnp.log(1.0 + jnp.exp(-jnp.abs(x)))


def _softplus(x):
    return jnp.maximum(x, 0.0) + jnp.log(1.0 + jnp.exp(-jnp.abs(x)))


def _rms(x, gain):
    return x * lax.rsqrt(jnp.mean(x * x, axis=-1, keepdims=True) + RMS_EPS) * gain


IN_TM = 512
IN_TN = 512


def _inproj_kernel(h_ref, g_ref, w_ref, wg_ref, cos_ref, sin_ref, z_ref, zg_ref, *, rope_chunks):
    hn = _rms(h_ref[...], g_ref[...])
    hb = hn.astype(BF16)
    zg_ref[...] = _dot(hb, wg_ref[...])
    if rope_chunks:
        lane = lax.broadcasted_iota(jnp.int32, (IN_TM, IN_TN), 1)
        first_half = (lane & 32) == 0
        cos = jnp.concatenate([cos_ref[...]] * (IN_TN // LANES), axis=1)
        sin = jnp.concatenate([sin_ref[...]] * (IN_TN // LANES), axis=1)
    for c in range(Z_MAIN // IN_TN):
        zc = _dot(hb, w_ref[:, c * IN_TN:(c + 1) * IN_TN])
        if c < rope_chunks:
            partner = jnp.where(first_half, pltpu.roll(zc, IN_TN - 32, axis=1), pltpu.roll(zc, 32, axis=1))
            zc = zc * cos + partner * sin
        z_ref[:, c * IN_TN:(c + 1) * IN_TN] = zc.astype(BF16)


def _inproj(h, gain, w_main, w_gate, cos, sin, seq, rope_chunks):
    t = h.shape[0]
    per_seq = seq // IN_TM
    return pl.pallas_call(
        functools.partial(_inproj_kernel, rope_chunks=rope_chunks),
        grid=(t // IN_TM,),
        in_specs=[
            pl.BlockSpec((IN_TM, D_MODEL), lambda i: (i, 0)),
            pl.BlockSpec((1, D_MODEL), lambda i: (0, 0)),
            pl.BlockSpec((D_MODEL, Z_MAIN), lambda i: (0, 0)),
            pl.BlockSpec((D_MODEL, LANES), lambda i: (0, 0)),
            pl.BlockSpec((IN_TM, LANES), lambda i: (i % per_seq, 0)),
            pl.BlockSpec((IN_TM, LANES), lambda i: (i % per_seq, 0)),
        ],
        out_specs=[
            pl.BlockSpec((IN_TM, Z_MAIN), lambda i: (i, 0)),
            pl.BlockSpec((IN_TM, LANES), lambda i: (i, 0)),
        ],
        out_shape=[jax.ShapeDtypeStruct((t, Z_MAIN), BF16), jax.ShapeDtypeStruct((t, LANES), F32)],
        compiler_params=_cparams(("parallel",)),
        name="inproj",
    )(h, gain, w_main, w_gate, cos, sin)


def _rope_tables(seq):
    inv = ROPE_THETA ** (-jnp.arange(0, A_HALF, 2, dtype=F32) / A_HALF)
    ang = jnp.arange(seq, dtype=F32)[:, None] * inv[None, :]
    cos, sin = jnp.cos(ang), jnp.sin(ang)
    return jnp.tile(cos, (1, 4)), jnp.tile(jnp.concatenate([-sin, sin], axis=1), (1, 2))


ATT_W = 512
ATT_TK = 512
LOG2E = 1.4426950408889634


def _attn_init(v_ref, vt_ref, m_ref, l_ref, acc_ref):
    @pl.when(pl.program_id(2) == 0)
    def _():
        def body(j, carry):
            off = pl.multiple_of(j * ATT_TK, ATT_TK)
            vt_ref[j] = v_ref[pl.ds(off, ATT_TK), :].astype(F32).T.astype(BF16)
            return carry
        lax.fori_loop(0, vt_ref.shape[0], body, 0)

    m_ref[...] = jnp.full(m_ref.shape, NEG_INF, F32)
    l_ref[...] = jnp.zeros(l_ref.shape, F32)
    acc_ref[...] = jnp.zeros(acc_ref.shape, F32)


def _attn_logits(k_ref, q_t, s_ref, blk, mask):
    off = pl.multiple_of(blk * ATT_TK, ATT_TK)
    s = _dot(k_ref[pl.ds(off, ATT_TK), :], q_t)
    if mask is not None:
        s = jnp.where(mask, s, NEG_INF)
    s_ref[...] = s


def _attn_consume(s_ref, vt_ref, blk, m_ref, l_ref, acc_ref):
    s = s_ref[...]
    m_prev = m_ref[...]
    m_new = jnp.maximum(m_prev, jnp.max(s, axis=0, keepdims=True))
    alpha = jnp.exp2(m_prev - m_new)
    p = jnp.exp2(s - m_new)
    l_ref[...] = alpha * l_ref[...] + jnp.sum(p, axis=0, keepdims=True)
    acc_ref[...] = alpha * acc_ref[...] + _dot(vt_ref[blk], p.astype(BF16))
    m_ref[...] = m_new


def _attn_pipeline(k_ref, vt_ref, q_t, s0_ref, s1_ref, m_ref, l_ref, acc_ref, n_full, diag_blk, diag_mask):
    def consume(s_ref, blk):
        _attn_consume(s_ref, vt_ref, blk, m_ref, l_ref, acc_ref)

    _attn_logits(k_ref, q_t, s0_ref, diag_blk, diag_mask)

    def body(t, carry):
        _attn_logits(k_ref, q_t, s1_ref, 2 * t, None)
        consume(s0_ref, jnp.where(t == 0, diag_blk, 2 * t - 1))
        _attn_logits(k_ref, q_t, s0_ref, 2 * t + 1, None)
        consume(s1_ref, 2 * t)
        return carry

    npairs = n_full // 2
    lax.fori_loop(0, npairs, body, 0)
    in_s0 = jnp.where(npairs == 0, diag_blk, 2 * npairs - 1)

    @pl.when(n_full % 2 == 1)
    def _():
        _attn_logits(k_ref, q_t, s1_ref, n_full - 1, None)
        consume(s0_ref, in_s0)
        consume(s1_ref, n_full - 1)

    @pl.when(n_full % 2 == 0)
    def _():
        consume(s0_ref, in_s0)


def _attn_a_kernel(q_ref, k_ref, v_ref, lam_ref, gain_ref, o_ref, vt_ref, s0_ref, s1_ref, m_ref, l_ref, acc_ref,
                   *, lam_init):
    _attn_init(v_ref, vt_ref, m_ref, l_ref, acc_ref)
    i = pl.program_id(2)
    tq = ATT_W
    q = (q_ref[...].astype(F32) * (A_HALF ** -0.5 * LOG2E)).T
    dim = lax.broadcasted_iota(jnp.int32, q.shape, 0)
    q_t = jnp.concatenate([jnp.where(dim < A_HALF, q, 0.0), jnp.where(dim >= A_HALF, q, 0.0)], axis=1).astype(BF16)
    key = lax.broadcasted_iota(jnp.int32, (ATT_TK, 2 * tq), 0)
    qry = lax.broadcasted_iota(jnp.int32, (ATT_TK, 2 * tq), 1) & (tq - 1)
    _attn_pipeline(k_ref, vt_ref, q_t, s0_ref, s1_ref, m_ref, l_ref, acc_ref, i, i, (key >> 6) <= (qry >> 6))

    lam_p = lam_ref[...]
    lam = (jnp.exp(jnp.sum(lam_p[0:1] * lam_p[1:2], axis=-1, keepdims=True))
           - jnp.exp(jnp.sum(lam_p[2:3] * lam_p[3:4], axis=-1, keepdims=True)) + lam_init)
    o_all = acc_ref[...] * (1.0 / l_ref[...])
    out = o_all[:, 0:tq] - lam * o_all[:, tq:2 * tq]
    out = out * lax.rsqrt(jnp.mean(out * out, axis=0, keepdims=True) + RMS_EPS) * (gain_ref[...] * (1.0 - lam_init))
    o_ref[...] = out.T.astype(BF16)


def _attn_scratch(seq, cols):
    return [pltpu.VMEM((seq // ATT_TK, HEAD_DIM, ATT_TK), BF16), pltpu.VMEM((ATT_TK, cols), F32),
            pltpu.VMEM((ATT_TK, cols), F32), pltpu.VMEM((1, cols), F32), pltpu.VMEM((1, cols), F32),
            pltpu.VMEM((HEAD_DIM, cols), F32)]


def _attn_a(z, lam_p, gain_col, batch, seq, lam_init):
    t = batch * seq
    nq = seq // ATT_W
    return pl.pallas_call(
        functools.partial(_attn_a_kernel, lam_init=lam_init),
        grid=(batch, N_HEADS, nq),
        in_specs=[
            pl.BlockSpec((ATT_W, HEAD_DIM), lambda b, h, i: (b * nq + i, h)),
            pl.BlockSpec((seq, HEAD_DIM), lambda b, h, i: (b, N_HEADS + h)),
            pl.BlockSpec((seq, HEAD_DIM), lambda b, h, i: (b, 2 * N_HEADS + h)),
            pl.BlockSpec((4, A_HALF), lambda b, h, i: (0, 0)),
            pl.BlockSpec((HEAD_DIM, 1), lambda b, h, i: (0, 0)),
        ],
        out_specs=pl.BlockSpec((ATT_W, HEAD_DIM), lambda b, h, i: (b * nq + i, h)),
        out_shape=jax.ShapeDtypeStruct((t, MIX_W), BF16),
        scratch_shapes=_attn_scratch(seq, 2 * ATT_W),
        compiler_params=_cparams(("parallel", "parallel", "arbitrary")),
        name="attn_a",
    )(z, z, z, lam_p, gain_col)


D_AUG = 2 * HEAD_DIM
BIAS_ROWS = 16


def _attn_d_kernel(q_ref, k_ref, v_ref, kx_ref, qx_ref, o_ref, kaug_ref, vt_ref, s0_ref, s1_ref, m_ref, l_ref,
                   acc_ref):
    _attn_init(v_ref, vt_ref, m_ref, l_ref, acc_ref)
    i = pl.program_id(2)

    @pl.when(i == 0)
    def _():
        def body(j, carry):
            off = pl.multiple_of(j * ATT_TK, ATT_TK)
            kaug_ref[pl.ds(off, ATT_TK), 0:HEAD_DIM] = k_ref[pl.ds(off, ATT_TK), :]
            kaug_ref[pl.ds(off, ATT_TK), HEAD_DIM:D_AUG] = kx_ref[0, pl.ds(off, ATT_TK), :]
            return carry
        lax.fori_loop(0, vt_ref.shape[0], body, 0)

    q = (q_ref[...].astype(F32) * (HEAD_DIM ** -0.5 * LOG2E)).T.astype(BF16)
    q_t = jnp.concatenate([q, qx_ref[0], jnp.zeros((D_AUG - HEAD_DIM - BIAS_ROWS, ATT_W), BF16)], axis=0)
    key = lax.broadcasted_iota(jnp.int32, (ATT_TK, ATT_W), 0)
    qry = lax.broadcasted_iota(jnp.int32, (ATT_TK, ATT_W), 1)
    _attn_pipeline(kaug_ref, vt_ref, q_t, s0_ref, s1_ref, m_ref, l_ref, acc_ref, i, i, key <= qry)
    o_ref[...] = (acc_ref[...] * (1.0 / l_ref[...])).T.astype(BF16)


def _attn_d(z, kx, qx, batch, seq):
    t = batch * seq
    nq = seq // ATT_W
    base = 4 * N_HEADS
    return pl.pallas_call(
        _attn_d_kernel,
        grid=(batch, N_HEADS, nq),
        in_specs=[
            pl.BlockSpec((ATT_W, HEAD_DIM), lambda b, h, i: (b * nq + i, base + h)),
            pl.BlockSpec((seq, HEAD_DIM), lambda b, h, i: (b, base + N_HEADS + h)),
            pl.BlockSpec((seq, HEAD_DIM), lambda b, h, i: (b, base + 2 * N_HEADS + h)),
            pl.BlockSpec((1, seq, HEAD_DIM), lambda b, h, i: (b * N_HEADS + h, 0, 0)),
            pl.BlockSpec((1, BIAS_ROWS, ATT_W), lambda b, h, i: (b * N_HEADS + h, 0, i)),
        ],
        out_specs=pl.BlockSpec((ATT_W, HEAD_DIM), lambda b, h, i: (b * nq + i, h)),
        out_shape=jax.ShapeDtypeStruct((t, MIX_W), BF16),
        scratch_shapes=[pltpu.VMEM((seq, D_AUG), BF16)] + _attn_scratch(seq, ATT_W),
        compiler_params=_cparams(("parallel", "parallel", "arbitrary")),
        name="attn_d",
    )(z, z, z, kx, qx)


def _forget_bias_operands(cum):
    c = cum * LOG2E
    hi_f = _bf16_part(c)
    mid_f = _bf16_part(c - hi_f)
    hi, mid, lo = hi_f.astype(BF16), mid_f.astype(BF16), (c - hi_f - mid_f).astype(BF16)
    one = jnp.ones_like(hi)
    kx = jnp.stack([hi, mid, lo, one, one, one], axis=-1)
    kx = jnp.pad(kx, ((0, 0), (0, 0), (0, HEAD_DIM - 6)))
    qx = jnp.stack([-one, -one, -one, hi, mid, lo], axis=1)
    qx = jnp.pad(qx, ((0, 0), (0, BIAS_ROWS - 6), (0, 0)))
    return kx, qx


def _logf_cumsum_kernel(x_ref, b_ref, o_ref, *, rows_per_seq):
    r = x_ref.shape[0]
    lf = _log_sigmoid(x_ref[...] + b_ref[...])
    a = lax.broadcasted_iota(jnp.int32, (LANES, LANES), 0)
    b = lax.broadcasted_iota(jnp.int32, (LANES, LANES), 1)
    within = _dot_exact(lf, (a <= b).astype(F32))
    totals = _dot_exact(lf, jnp.ones((LANES, LANES), F32))
    ra = lax.broadcasted_iota(jnp.int32, (r, r), 0)
    rb = lax.broadcasted_iota(jnp.int32, (r, r), 1)
    earlier = ((rb < ra) & ((rb // rows_per_seq) == (ra // rows_per_seq))).astype(F32)
    o_ref[...] = within + _dot_exact(earlier, totals)


def _logf_cumsum(x, bias_rows, rows_per_seq):
    r = x.shape[0]
    return pl.pallas_call(
        functools.partial(_logf_cumsum_kernel, rows_per_seq=rows_per_seq),
        out_shape=jax.ShapeDtypeStruct((r, LANES), F32),
        compiler_params=pltpu.CompilerParams(vmem_limit_bytes=VMEM_LIMIT),
        name="logf_cumsum",
    )(x, bias_rows)


CONV_TAIL = 8


def _causal_conv_silu(x, prev_tail, w):
    row = lax.broadcasted_iota(jnp.int32, prev_tail.shape, 0)
    acc = x * w[CONV_W - 1:CONV_W, :]
    for s in range(1, CONV_W):
        xs = pltpu.roll(x, s, axis=0)
        head = jnp.where(row >= s, xs[0:CONV_TAIL], pltpu.roll(prev_tail, s, axis=0))
        xs = jnp.concatenate([head, xs[CONV_TAIL:]], axis=0)
        acc = acc + xs * w[CONV_W - 1 - s:CONV_W - s, :]
    return _silu(acc)


def _chunk_masks():
    r = lax.broadcasted_iota(jnp.int32, (CHUNK, CHUNK), 0)
    c = lax.broadcasted_iota(jnp.int32, (CHUNK, CHUNK), 1)
    return r, c


REC_NB_MLSTM = 2
REC_NB_DELTANET = 4


def _rec_batch_block(batch, nb):
    return nb if batch % nb == 0 else 1


def _rec_specs(nb, cols):
    main = [pl.BlockSpec((nb, CHUNK, MIX_W), lambda b, c, j=j: (b, c, j)) for j in cols]
    return main + [pl.BlockSpec((nb, CHUNK, LANES), lambda b, c: (b, c, 0))]


def _mlstm_kernel(q_ref, k_ref, v_ref, og_ref, g_ref, conv_ref, bias_ref, norm_ref, out_ref,
                  pq_ref, pk_ref, c_ref, n_ref, m_ref):
    @pl.when(pl.program_id(1) == 0)
    def _():
        pq_ref[...] = jnp.zeros(pq_ref.shape, F32)
        pk_ref[...] = jnp.zeros(pk_ref.shape, F32)
        c_ref[...] = jnp.zeros(c_ref.shape, F32)
        n_ref[...] = jnp.zeros(n_ref.shape, F32)
        m_ref[...] = jnp.zeros(m_ref.shape, F32)

    lane = lax.broadcasted_iota(jnp.int32, (CHUNK, LANES), 1)
    r, c = _chunk_masks()
    causal = c <= r
    tri = causal.astype(F32)
    units = []
    for bi in range(q_ref.shape[0]):
        xq = q_ref[bi].astype(F32)
        xk = k_ref[bi].astype(F32)
        q_all = _causal_conv_silu(xq, pq_ref[bi], conv_ref[:, 0:MIX_W])
        k_all = _causal_conv_silu(xk, pk_ref[bi], conv_ref[:, MIX_W:2 * MIX_W]) * (HEAD_DIM ** -0.5)
        pq_ref[bi] = xq[CHUNK - CONV_TAIL:CHUNK]
        pk_ref[bi] = xk[CHUNK - CONV_TAIL:CHUNK]
        pre = g_ref[bi] + bias_ref[...]
        e = jnp.where(lane < N_HEADS, pre, _log_sigmoid(pre))
        cum = _dot_exact(tri, e)
        e_t = e.T
        cum_t = cum.T
        for h in range(N_HEADS):
            sl = slice(h * HEAD_DIM, (h + 1) * HEAD_DIM)
            u = dict(bi=bi, h=h, sl=sl, q=q_all[:, sl], k=k_all[:, sl], v=v_ref[bi, :, sl].astype(F32))
            u["qb"] = u["q"].astype(BF16)
            u["kb"] = u["k"].astype(BF16)
            i_col = e[:, h:h + 1]
            i_row = e_t[h:h + 1, :]
            b_col = cum[:, N_HEADS + h:N_HEADS + h + 1]
            b_row = cum_t[N_HEADS + h:N_HEADS + h + 1, :]
            b_last = b_col[CHUNK - 1:CHUNK, :]
            m_st = m_ref[bi, h:h + 1, 0:1]
            u["c_st"] = c_ref[bi, h]
            u["n_st"] = n_ref[bi, h:h + 1, :]
            d_log = jnp.where(causal, b_col - b_row + i_row, NEG_INF)
            inter_log = b_col + m_st
            u["m_t"] = jnp.maximum(inter_log, jnp.max(d_log, axis=-1, keepdims=True))
            u["d_w"] = jnp.exp(d_log - u["m_t"])
            u["inter_w"] = jnp.exp(inter_log - u["m_t"])
            w_log = b_last - b_col + i_col
            u["m_new"] = jnp.maximum(b_last + m_st, jnp.max(w_log, axis=0, keepdims=True))
            u["sw"] = jnp.exp(w_log - u["m_new"])
            u["decay"] = jnp.exp(b_last + m_st - u["m_new"])
            units.append(u)

    for u in units:
        u["qk"] = _dot_nt(u["qb"], u["kb"])
    for u in units:
        u["qc"] = _dot_nt(u["qb"], u["c_st"].astype(BF16))
    for u in units:
        u["upd"] = _dot((u["v"] * u["sw"]).T.astype(BF16), u["kb"])
    for u in units:
        u["a"] = u["qk"] * u["d_w"]
        u["av"] = _dot(u["a"].astype(BF16), u["v"].astype(BF16))
    for u in units:
        bi, h, sl = u["bi"], u["h"], u["sl"]
        num = u["av"] + u["inter_w"] * u["qc"]
        den = (jnp.sum(u["a"], axis=-1, keepdims=True)
               + u["inter_w"] * jnp.sum(u["q"] * u["n_st"], axis=-1, keepdims=True))
        hh = num / jnp.maximum(jnp.abs(den), jnp.exp(-u["m_t"]))
        c_ref[bi, h] = u["decay"] * u["c_st"] + u["upd"]
        n_ref[bi, h:h + 1, :] = u["decay"] * u["n_st"] + jnp.sum(u["k"] * u["sw"], axis=0, keepdims=True)
        m_ref[bi, h:h + 1, :] = jnp.broadcast_to(u["m_new"], (1, LANES))
        gate = _sigmoid(og_ref[bi, :, sl].astype(F32))
        out_ref[bi, :, sl] = (_rms(hh, norm_ref[...]) * gate).astype(BF16)


def _mlstm(z, zg, conv_w, gate_bias, norm, batch, seq):
    nb = _rec_batch_block(batch, REC_NB_MLSTM)
    full = lambda s: pl.BlockSpec(s, lambda b, c: (0, 0))
    out = pl.pallas_call(
        _mlstm_kernel,
        grid=(batch // nb, seq // CHUNK),
        in_specs=_rec_specs(nb, (3, 4, 5, 6)) + [full((CONV_W, 2 * MIX_W)), full((1, LANES)), full((1, HEAD_DIM))],
        out_specs=pl.BlockSpec((nb, CHUNK, MIX_W), lambda b, c: (b, c, 0)),
        out_shape=jax.ShapeDtypeStruct((batch, seq, MIX_W), BF16),
        scratch_shapes=[pltpu.VMEM((nb, CONV_TAIL, MIX_W), F32), pltpu.VMEM((nb, CONV_TAIL, MIX_W), F32),
                        pltpu.VMEM((nb, N_HEADS, HEAD_DIM, HEAD_DIM), F32), pltpu.VMEM((nb, 8, HEAD_DIM), F32),
                        pltpu.VMEM((nb, 8, LANES), F32)],
        compiler_params=_cparams(("parallel", "arbitrary")),
        name="mlstm",
    )(*([z.reshape(batch, seq, Z_MAIN)] * 4), zg.reshape(batch, seq, LANES), conv_w, gate_bias, norm)
    return out.reshape(batch * seq, MIX_W)


def _bdot(a, b):
    return _dot(a.astype(BF16), b.astype(BF16))


def _unit_lower_inverses(l_stricts, r, c):
    same16 = (r >> 4) == (c >> 4)
    same32 = (r >> 5) == (c >> 5)
    eye = (r == c).astype(F32)
    ps = [jnp.where(same16, -l, 0.0) for l in l_stricts]
    invs = [eye + p for p in ps]
    for _ in range(3):
        ps = [_bdot(p, p) for p in ps]
        invs = [inv + _bdot(inv, p) for inv, p in zip(invs, ps)]
    for off_mask in (same32 & jnp.logical_not(same16), jnp.logical_not(same32)):
        ts = [_bdot(jnp.where(off_mask, l, 0.0), inv) for l, inv in zip(l_stricts, invs)]
        invs = [inv - _bdot(inv, t) for inv, t in zip(invs, ts)]
    return invs


def _deltanet_kernel(q_ref, k_ref, v_ref, og_ref, g_ref, conv_ref, par_ref, norm_ref, out_ref,
                     pq_ref, pk_ref, pv_ref, s_ref):
    @pl.when(pl.program_id(1) == 0)
    def _():
        pq_ref[...] = jnp.zeros(pq_ref.shape, F32)
        pk_ref[...] = jnp.zeros(pk_ref.shape, F32)
        pv_ref[...] = jnp.zeros(pv_ref.shape, F32)
        s_ref[...] = jnp.zeros(s_ref.shape, F32)

    lane = lax.broadcasted_iota(jnp.int32, (CHUNK, LANES), 1)
    r, c = _chunk_masks()
    tri = (c <= r).astype(F32)
    units = []
    for bi in range(q_ref.shape[0]):
        xq = q_ref[bi].astype(F32)
        xk = k_ref[bi].astype(F32)
        xv = v_ref[bi].astype(F32)
        q_all = _causal_conv_silu(xq, pq_ref[bi], conv_ref[:, 0:MIX_W])
        k_all = _causal_conv_silu(xk, pk_ref[bi], conv_ref[:, MIX_W:2 * MIX_W])
        v_all = _causal_conv_silu(xv, pv_ref[bi], conv_ref[:, 2 * MIX_W:3 * MIX_W])
        pq_ref[bi] = xq[CHUNK - CONV_TAIL:CHUNK]
        pk_ref[bi] = xk[CHUNK - CONV_TAIL:CHUNK]
        pv_ref[bi] = xv[CHUNK - CONV_TAIL:CHUNK]
        pre = g_ref[bi]
        e = jnp.where(lane < N_HEADS, -jnp.exp(par_ref[0:1, :]) * _softplus(pre + par_ref[1:2, :]), _sigmoid(pre))
        cum = _dot_exact(tri, e)
        cum_t = cum.T
        for h in range(N_HEADS):
            sl = slice(h * HEAD_DIM, (h + 1) * HEAD_DIM)
            q = q_all[:, sl]
            k = k_all[:, sl]
            q = q * lax.rsqrt(jnp.sum(q * q, axis=-1, keepdims=True) + RMS_EPS) * (HEAD_DIM ** -0.5)
            k = k * lax.rsqrt(jnp.sum(k * k, axis=-1, keepdims=True) + RMS_EPS)
            beta = e[:, N_HEADS + h:N_HEADS + h + 1]
            b_col = cum[:, h:h + 1]
            b_row = cum_t[h:h + 1, :]
            b_last = b_col[CHUNK - 1:CHUNK, :]
            e_col = jnp.exp(b_col)
            kbeta = k * beta
            units.append(dict(
                bi=bi, h=h, sl=sl, k=k, kbeta=kbeta, q_dec=q * e_col,
                gam=jnp.exp(jnp.where(c <= r, b_col - b_row, NEG_INF)),
                rhs=jnp.concatenate([v_all[:, sl] * beta, kbeta * e_col], axis=1),
                k_dec=k * jnp.exp(b_last - b_col), chunk_decay=jnp.exp(b_last),
                kq=jnp.concatenate([kbeta, q], axis=0)))

    for u in units:
        both = _dot_nt(u["kq"].astype(BF16), u["k"].astype(BF16)) * jnp.concatenate([u["gam"], u["gam"]], axis=0)
        u["l"] = jnp.where(c < r, both[0:CHUNK], 0.0)
        u["qk"] = both[CHUNK:2 * CHUNK]
    invs = _unit_lower_inverses([u["l"] for u in units], r, c)
    for u, inv in zip(units, invs):
        sol = _bdot(inv, u["rhs"])
        u["u"] = sol[:, 0:HEAD_DIM]
        u["w"] = sol[:, HEAD_DIM:2 * HEAD_DIM]
    for u in units:
        u["st"] = s_ref[u["bi"], u["h"]]
        u["ws"] = _bdot(jnp.concatenate([u["w"], u["q_dec"]], axis=0), u["st"])
    for u in units:
        u["v_new"] = u["u"] - u["ws"][0:CHUNK]
        u["o"] = u["ws"][CHUNK:2 * CHUNK] + _bdot(u["qk"], u["v_new"])
    for u in units:
        s_ref[u["bi"], u["h"]] = u["chunk_decay"] * u["st"] + _bdot(u["k_dec"].T, u["v_new"])
    for u in units:
        bi, sl = u["bi"], u["sl"]
        gate = _silu(og_ref[bi, :, sl].astype(F32))
        out_ref[bi, :, sl] = (_rms(u["o"], norm_ref[...]) * gate).astype(BF16)


def _deltanet(z, zg, conv_w, params, norm, batch, seq):
    nb = _rec_batch_block(batch, REC_NB_DELTANET)
    full = lambda s: pl.BlockSpec(s, lambda b, c: (0, 0))
    out = pl.pallas_call(
        _deltanet_kernel,
        grid=(batch // nb, seq // CHUNK),
        in_specs=_rec_specs(nb, (0, 1, 2, 3)) + [full((CONV_W, 3 * MIX_W)), full((2, LANES)), full((1, HEAD_DIM))],
        out_specs=pl.BlockSpec((nb, CHUNK, MIX_W), lambda b, c: (b, c, 0)),
        out_shape=jax.ShapeDtypeStruct((batch, seq, MIX_W), BF16),
        scratch_shapes=[pltpu.VMEM((nb, CONV_TAIL, MIX_W), F32)] * 3
        + [pltpu.VMEM((nb, N_HEADS, HEAD_DIM, HEAD_DIM), F32)],
        compiler_params=_cparams(("parallel", "arbitrary")),
        name="deltanet",
    )(*([z.reshape(batch, seq, Z_MAIN)] * 4), zg.reshape(batch, seq, LANES), conv_w, params, norm)
    return out.reshape(batch * seq, MIX_W)


OUT_TM = 512
G_LANE0 = N_EXPERTS
ROUTE_W_LANE = 4
ROW_TILES = D_MODEL // LANES


def _outproj_router_kernel(h_ref, a_ref, b_ref, wo_ref, gn_ref, whi_ref, wlo_ref, br_ref,
                           hout_ref, xn_ref, comb_ref):
    h_new = h_ref[...] + _dot(a_ref[...], wo_ref[0:MIX_W, :]) + _dot(b_ref[...], wo_ref[MIX_W:2 * MIX_W, :])
    hout_ref[...] = h_new
    hn = _rms(h_new, gn_ref[...])
    for c in range(ROW_TILES):
        xn_ref[:, c, :] = hn[:, c * LANES:(c + 1) * LANES]
    hi_f = _bf16_part(hn)
    hi = hi_f.astype(BF16)
    lo = (hn - hi_f).astype(BF16)
    logits = (_dot(hi, whi_ref[...]) + _dot(lo, whi_ref[...]) + _dot(hi, wlo_ref[...]) + _dot(lo, wlo_ref[...])
              + br_ref[...])

    lane_i = lax.broadcasted_iota(jnp.int32, logits.shape, 1)
    lane = lane_i.astype(F32)
    big = 1e9
    is_group = (lane_i >= G_LANE0) & (lane_i < G_LANE0 + N_GROUPS)
    gl = jnp.where(is_group, logits, NEG_INF)
    gmax = jnp.max(gl, axis=-1, keepdims=True)
    g_sel = jnp.min(jnp.where(gl == gmax, lane, big), axis=-1, keepdims=True) - G_LANE0
    p_top = 1.0 / jnp.sum(jnp.exp(gl - gmax), axis=-1, keepdims=True)
    group_of_lane = (lane_i >> 2).astype(F32)
    in_group = (lane_i < N_EXPERTS) & (group_of_lane == g_sel)
    el = jnp.where(in_group, logits, NEG_INF)
    v1 = jnp.max(el, axis=-1, keepdims=True)
    i1 = jnp.min(jnp.where(el == v1, lane, big), axis=-1, keepdims=True)
    el2 = jnp.where(lane == i1, NEG_INF, el)
    v2 = jnp.max(el2, axis=-1, keepdims=True)
    i2 = jnp.min(jnp.where(el2 == v2, lane, big), axis=-1, keepdims=True)
    e21 = jnp.exp(v2 - v1)
    w1 = p_top / (1.0 + e21)
    local = lane - ROUTE_W_LANE + EXPERTS_PER_GROUP * g_sel
    comb_ref[...] = (jnp.where(lane_i == 0, g_sel, 0.0) + jnp.where(local == i1, w1, 0.0)
                     + jnp.where(local == i2, w1 * e21, 0.0))


def _outproj_router(h, oa, ob, w_out, gain, wr_hi, wr_lo, b_r):
    t = h.shape[0]
    row = lambda w: pl.BlockSpec((OUT_TM, w), lambda i: (i, 0))
    full = lambda s: pl.BlockSpec(s, lambda i: (0, 0))
    return pl.pallas_call(
        _outproj_router_kernel,
        grid=(t // OUT_TM,),
        in_specs=[row(D_MODEL), row(MIX_W), row(MIX_W), full((D_MODEL, D_MODEL)), full((1, D_MODEL)),
                  full((D_MODEL, LANES)), full((D_MODEL, LANES)), full((1, LANES))],
        out_specs=[row(D_MODEL), pl.BlockSpec((OUT_TM, ROW_TILES, LANES), lambda i: (i, 0, 0)), row(LANES)],
        out_shape=[jax.ShapeDtypeStruct((t, D_MODEL), F32), jax.ShapeDtypeStruct((t, ROW_TILES, LANES), F32),
                   jax.ShapeDtypeStruct((t, LANES), F32)],
        compiler_params=_cparams(("parallel",)),
        name="outproj_router",
    )(h, oa, ob, w_out, gain, wr_hi, wr_lo, b_r)


MOE_TM = 512


def _moe_plan(route, t):
    n_rows = t + (N_GROUPS + 1) * MOE_TM
    n_tiles = n_rows // MOE_TM
    group = route[:, 0].astype(jnp.int32)
    onehot = (group[:, None] == jnp.arange(N_GROUPS, dtype=jnp.int32)[None, :]).astype(jnp.int32)
    csum = jnp.cumsum(onehot, axis=0)
    rank = jnp.sum(csum * onehot, axis=1) - 1
    padded = ((csum[-1] + MOE_TM - 1) // MOE_TM) * MOE_TM
    ends = jnp.cumsum(padded)
    pos = jnp.sum((ends - padded)[None, :] * onehot, axis=1) + rank
    record = jnp.concatenate([jnp.arange(1, t + 1, dtype=F32)[:, None],
                              route[:, ROUTE_W_LANE:ROUTE_W_LANE + EXPERTS_PER_GROUP]], axis=1)
    record = jnp.zeros((n_rows, 1 + EXPERTS_PER_GROUP), F32).at[pos].set(record, unique_indices=True)
    real = record[:, 0].astype(jnp.int32) - 1
    is_pad = real < 0
    dst = jnp.where(is_pad, t + jnp.cumsum(is_pad.astype(jnp.int32)) - 1, real)
    w_sorted = record[:, 1:]
    starts = jnp.arange(n_tiles, dtype=jnp.int32) * MOE_TM
    tile_group = jnp.minimum(jnp.sum((starts[:, None] >= ends[None, :]).astype(jnp.int32), axis=1), N_GROUPS - 1)
    return tile_group, dst.reshape(n_tiles, MOE_TM), w_sorted


def _moe_kernel(tg_ref, dst_ref, x_hbm, ws_ref, wg_ref, wu_ref, wd_ref, out_hbm,
                xbuf, ybuf, sem_g, sem_s, *, n_tok):
    t = pl.program_id(0)
    nt = pl.num_programs(0)
    slot = t % 2

    def row_gather(tile, buf, r):
        d = dst_ref[tile, r]
        tok = jnp.where(d < n_tok, d, 0)
        return pltpu.make_async_copy(x_hbm.at[tok], xbuf.at[buf, r], sem_g.at[buf])

    def row_scatter(tile, buf, r):
        return pltpu.make_async_copy(ybuf.at[buf, r], out_hbm.at[dst_ref[tile, r]], sem_s.at[buf])

    def wait_gather(buf):
        pltpu.make_async_copy(x_hbm.at[pl.ds(0, MOE_TM)], xbuf.at[buf], sem_g.at[buf]).wait()

    def wait_scatter(buf):
        pltpu.make_async_copy(ybuf.at[buf], out_hbm.at[pl.ds(0, MOE_TM)], sem_s.at[buf]).wait()

    def start_gather(tile, buf):
        def body(i, carry):
            for u in range(8):
                row_gather(tile, buf, i * 8 + u).start(priority=u % 2)
            return carry
        lax.fori_loop(0, MOE_TM // 8, body, 0)

    nxt = jnp.where(t == nt - 1, 0, t + 1)
    prv = jnp.where(t == 0, nt - 1, t - 1)
    other = 1 - slot

    @pl.when(t == 0)
    def _():
        start_gather(0, 0)
        ybuf[1] = jnp.zeros(ybuf.shape[1:], F32)

    start_gather(nxt, other)
    wait_gather(slot)

    x = jnp.concatenate([xbuf[slot, :, c, :] for c in range(ROW_TILES)], axis=1).astype(BF16)
    ws = ws_ref[...]
    y = jnp.zeros((MOE_TM, D_MODEL), F32)
    for r in range(MOE_TM):
        row_scatter(prv, other, r).start(priority=r % 2)
    for e in range(EXPERTS_PER_GROUP):
        he = _silu(_dot(x, wg_ref[e])) * _dot(x, wu_ref[e])
        y = y + _dot((he * ws[:, e:e + 1]).astype(BF16), wd_ref[e])

    @pl.when(t > 0)
    def _():
        wait_scatter(slot)

    for c in range(ROW_TILES):
        ybuf[slot, :, c, :] = y[:, c * LANES:(c + 1) * LANES]

    @pl.when(t == nt - 1)
    def _():
        wait_scatter(other)
        wait_gather(other)


def _moe(xn, route, wg, wu, wd):
    t = xn.shape[0]
    tile_group, dst, w_sorted = _moe_plan(route, t)
    n_tiles = dst.shape[0]
    buf = pltpu.VMEM((2, MOE_TM, ROW_TILES, LANES), F32)
    grid_spec = pltpu.PrefetchScalarGridSpec(
        num_scalar_prefetch=2,
        grid=(n_tiles,),
        in_specs=[
            pl.BlockSpec(memory_space=pl.ANY),
            pl.BlockSpec((MOE_TM, EXPERTS_PER_GROUP), lambda i, tg, dst: (i, 0)),
            pl.BlockSpec((EXPERTS_PER_GROUP, D_MODEL, EXPERT_FF), lambda i, tg, dst: (tg[i], 0, 0)),
            pl.BlockSpec((EXPERTS_PER_GROUP, D_MODEL, EXPERT_FF), lambda i, tg, dst: (tg[i], 0, 0)),
            pl.BlockSpec((EXPERTS_PER_GROUP, EXPERT_FF, D_MODEL), lambda i, tg, dst: (tg[i], 0, 0)),
        ],
        out_specs=pl.BlockSpec(memory_space=pl.ANY),
        scratch_shapes=[buf, buf, pltpu.SemaphoreType.DMA((2,)), pltpu.SemaphoreType.DMA((2,))],
    )
    return pl.pallas_call(
        functools.partial(_moe_kernel, n_tok=t),
        grid_spec=grid_spec,
        out_shape=jax.ShapeDtypeStruct((dst.size, ROW_TILES, LANES), F32),
        compiler_params=_cparams(("arbitrary",)),
        name="moe",
    )(tile_group, dst, xn, w_sorted, wg, wu, wd)


PLE_TM = 512


def _ple_kernel(h_ref, y_ref, p_ref, wg_ref, wp_ref, gn_ref, out_ref, *, final_norm):
    h = h_ref[...] + jnp.concatenate([y_ref[:, c, :] for c in range(ROW_TILES)], axis=1)
    gate = _sigmoid(_dot(h.astype(BF16), wg_ref[...]))
    out = h + gate * _dot(p_ref[...].astype(BF16), wp_ref[...])
    if final_norm:
        out = _rms(out, gn_ref[...])
    out_ref[...] = out


def _ple(h, y, p, w_gate, w_proj, gain, final_norm):
    t = h.shape[0]
    return pl.pallas_call(
        functools.partial(_ple_kernel, final_norm=final_norm),
        grid=(t // PLE_TM,),
        in_specs=[
            pl.BlockSpec((PLE_TM, D_MODEL), lambda i: (i, 0)),
            pl.BlockSpec((PLE_TM, ROW_TILES, LANES), lambda i: (i, 0, 0)),
            pl.BlockSpec((PLE_TM, PLE_DIM), lambda i: (i, 0)),
            pl.BlockSpec((D_MODEL, D_MODEL), lambda i: (0, 0)),
            pl.BlockSpec((PLE_DIM, D_MODEL), lambda i: (0, 0)),
            pl.BlockSpec((1, D_MODEL), lambda i: (0, 0)),
        ],
        out_specs=pl.BlockSpec((PLE_TM, D_MODEL), lambda i: (i, 0)),
        out_shape=jax.ShapeDtypeStruct((t, D_MODEL), F32),
        compiler_params=_cparams(("parallel",)),
        name="ple",
    )(h, y, p, w_gate, w_proj, gain)


def _pad_lanes(x):
    return jnp.pad(x, ((0, 0), (0, LANES - x.shape[-1])))


def _row(x):
    return x.reshape(1, -1).astype(F32)


def kernel(x, p, norm_mix, norm_ffn, norm_final, ab_w_in, ab_w_out, a_lam_q1, a_lam_k1, a_lam_q2, a_lam_k2,
           a_subln, b_conv, b_igate_bias, b_fgate_bias, b_norm, cd_w_in, cd_w_out, c_conv, c_a_log, c_dt_bias,
           c_norm, d_fgate_bias, moe_w_group, moe_b_group, moe_w_router, moe_b_router, moe_w_gate, moe_w_up,
           moe_w_down, ple_w_gate, ple_w_proj):
    batch, seq, _ = x.shape
    depth = p.shape[0]
    t = batch * seq
    assert seq % IN_TM == 0 and seq % ATT_TK == 0 and t % MOE_TM == 0
    h = x.reshape(t, D_MODEL)
    cos, sin = _rope_tables(seq)
    ones_tab = jnp.ones((seq, LANES), F32)

    for i in range(depth):
        j = i // 2
        if i % 2 == 0:
            w_in = ab_w_in[j]
            z, zg = _inproj(h, _row(norm_mix[i]), w_in[:, :Z_MAIN].astype(BF16),
                            _pad_lanes(w_in[:, Z_MAIN:]).astype(BF16), cos, sin, seq, rope_chunks=2)
            lam_init = 0.8 - 0.6 * math.exp(-0.3 * i)
            lam_p = jnp.stack([a_lam_q1[j], a_lam_k1[j], a_lam_q2[j], a_lam_k2[j]]).astype(F32)
            out_1 = _attn_a(z, lam_p, a_subln[j].reshape(HEAD_DIM, 1).astype(F32), batch, seq, lam_init)
            gate_bias = _pad_lanes(jnp.concatenate([b_igate_bias[j], b_fgate_bias[j]]).reshape(1, -1))
            out_2 = _mlstm(z, zg, b_conv[j], gate_bias, _row(b_norm[j]), batch, seq)
            w_out = ab_w_out[j]
        else:
            w_in = cd_w_in[j]
            c_main, d_main = 4 * MIX_W, 3 * MIX_W
            w_main = jnp.concatenate([w_in[:, :c_main], w_in[:, c_main + 8:c_main + 8 + d_main]], axis=1)
            w_small = jnp.concatenate([w_in[:, c_main:c_main + 8], w_in[:, c_main + 8 + d_main:]], axis=1)
            z, zg = _inproj(h, _row(norm_mix[i]), w_main.astype(BF16), _pad_lanes(w_small).astype(BF16),
                            ones_tab, ones_tab, seq, rope_chunks=0)
            params = jnp.stack([_pad_lanes(c_a_log[j].reshape(1, -1))[0], _pad_lanes(c_dt_bias[j].reshape(1, -1))[0]])
            out_1 = _deltanet(z, zg, c_conv[j], params, _row(c_norm[j]), batch, seq)
            fd = zg[:, 8:8 + N_HEADS].reshape(batch, seq, N_HEADS).transpose(0, 2, 1)
            bias_rows = jnp.repeat(jnp.tile(d_fgate_bias[j], batch), seq // LANES).reshape(-1, 1)
            cum = _logf_cumsum(fd.reshape(-1, LANES), jnp.broadcast_to(bias_rows, (bias_rows.shape[0], LANES)),
                               seq // LANES)
            kx, qx = _forget_bias_operands(cum.reshape(batch * N_HEADS, seq))
            out_2 = _attn_d(z, kx, qx, batch, seq)
            w_out = cd_w_out[j]

        w_r = _pad_lanes(jnp.concatenate([moe_w_router[i], moe_w_group[i]], axis=1))
        w_r_hi_f = _bf16_part(w_r)
        w_r_hi = w_r_hi_f.astype(BF16)
        w_r_lo = (w_r - w_r_hi_f).astype(BF16)
        b_r = _pad_lanes(jnp.concatenate([moe_b_router[i], moe_b_group[i]]).reshape(1, -1))
        h, xn, comb = _outproj_router(h, out_1, out_2, w_out.astype(BF16), _row(norm_ffn[i]), w_r_hi, w_r_lo, b_r)
        y = _moe(xn, comb, moe_w_gate[i].astype(BF16), moe_w_up[i].astype(BF16), moe_w_down[i].astype(BF16))
        h = _ple(h, y, p[i].reshape(t, PLE_DIM), ple_w_gate[i].astype(BF16), ple_w_proj[i].astype(BF16),
                 _row(norm_final), final_norm=(i == depth - 1))
    return h.reshape(batch, seq, D_MODEL)
```

```python
import functools
import math

import jax
import jax.numpy as jnp
from jax import lax
from jax.experimental import pallas as pl
from jax.experimental.pallas import tpu as pltpu

F32 = jnp.float32
BF16 = jnp.bfloat16
HIGHEST = lax.Precision.HIGHEST

D_MODEL = 1024
HEAD_DIM = 128
N_HEADS = 4
MIX_W = N_HEADS * HEAD_DIM
A_HALF = HEAD_DIM // 2
CHUNK = 64
CONV_W = 4
RMS_EPS = 1e-6
PLE_DIM = 256
N_GROUPS = 4
EXPERTS_PER_GROUP = 4
N_EXPERTS = 16
EXPERT_FF = D_MODEL // 2
ROPE_THETA = 10000.0
Z_MAIN = 7 * MIX_W
LANES = 128

VMEM_LIMIT = 56 * 1024 * 1024

NEG_INF = float("-inf")


def _cparams(sem):
    return pltpu.CompilerParams(dimension_semantics=sem, vmem_limit_bytes=VMEM_LIMIT)


def _dot(a, b):
    return jnp.dot(a, b, preferred_element_type=F32)


def _dot_nt(a, b):
    return lax.dot_general(a, b, (((1,), (1,)), ((), ())), preferred_element_type=F32)


def _dot_exact(a, b):
    return jnp.dot(a, b, preferred_element_type=F32, precision=HIGHEST)


def _bf16_part(x):
    bits = lax.bitcast_convert_type(x, jnp.int32) & jnp.int32(-65536)
    return lax.bitcast_convert_type(bits, F32)


def _sigmoid(x):
    return 1.0 / (1.0 + jnp.exp(-x))


def _silu(x):
    return x * _sigmoid(x)


def _log_sigmoid(x):
    return jnp.minimum(x, 0.0) - jnp.log(1.0 + jnp.exp(-jnp.abs(x)))


def _softplus(x):
    return jnp.maximum(x, 0.0) + jnp.log(1.0 + jnp.exp(-jnp.abs(x)))


def _rms(x, gain):
    return x * lax.rsqrt(jnp.mean(x * x, axis=-1, keepdims=True) + RMS_EPS) * gain


IN_TM = 512
IN_TN = 512


def _inproj_kernel(h_ref, g_ref, w_ref, wg_ref, cos_ref, sin_ref, z_ref, zg_ref, *, rope_chunks):
    hn = _rms(h_ref[...], g_ref[...])
    hb = hn.astype(BF16)
    zg_ref[...] = _dot(hb, wg_ref[...])
    if rope_chunks:
        lane = lax.broadcasted_iota(jnp.int32, (IN_TM, IN_TN), 1)
        first_half = (lane & 32) == 0
        cos = jnp.concatenate([cos_ref[...]] * (IN_TN // LANES), axis=1)
        sin = jnp.concatenate([sin_ref[...]] * (IN_TN // LANES), axis=1)
    for c in range(Z_MAIN // IN_TN):
        zc = _dot(hb, w_ref[:, c * IN_TN:(c + 1) * IN_TN])
        if c < rope_chunks:
            partner = jnp.where(first_half, pltpu.roll(zc, IN_TN - 32, axis=1), pltpu.roll(zc, 32, axis=1))
            zc = zc * cos + partner * sin
        z_ref[:, c * IN_TN:(c + 1) * IN_TN] = zc.astype(BF16)


def _inproj(h, gain, w_main, w_gate, cos, sin, seq, rope_chunks):
    t = h.shape[0]
    per_seq = seq // IN_TM
    return pl.pallas_call(
        functools.partial(_inproj_kernel, rope_chunks=rope_chunks),
        grid=(t // IN_TM,),
        in_specs=[
            pl.BlockSpec((IN_TM, D_MODEL), lambda i: (i, 0)),
            pl.BlockSpec((1, D_MODEL), lambda i: (0, 0)),
            pl.BlockSpec((D_MODEL, Z_MAIN), lambda i: (0, 0)),
            pl.BlockSpec((D_MODEL, LANES), lambda i: (0, 0)),
            pl.BlockSpec((IN_TM, LANES), lambda i: (i % per_seq, 0)),
            pl.BlockSpec((IN_TM, LANES), lambda i: (i % per_seq, 0)),
        ],
        out_specs=[
            pl.BlockSpec((IN_TM, Z_MAIN), lambda i: (i, 0)),
            pl.BlockSpec((IN_TM, LANES), lambda i: (i, 0)),
        ],
        out_shape=[jax.ShapeDtypeStruct((t, Z_MAIN), BF16), jax.ShapeDtypeStruct((t, LANES), F32)],
        compiler_params=_cparams(("parallel",)),
        name="inproj",
    )(h, gain, w_main, w_gate, cos, sin)


def _rope_tables(seq):
    inv = ROPE_THETA ** (-jnp.arange(0, A_HALF, 2, dtype=F32) / A_HALF)
    ang = jnp.arange(seq, dtype=F32)[:, None] * inv[None, :]
    cos, sin = jnp.cos(ang), jnp.sin(ang)
    return jnp.tile(cos, (1, 4)), jnp.tile(jnp.concatenate([-sin, sin], axis=1), (1, 2))


ATT_W = 512
ATT_TK = 512
LOG2E = 1.4426950408889634


def _attn_init(v_ref, vt_ref, m_ref, l_ref, acc_ref):
    @pl.when(pl.program_id(2) == 0)
    def _():
        def body(j, carry):
            off = pl.multiple_of(j * ATT_TK, ATT_TK)
            vt_ref[j] = v_ref[pl.ds(off, ATT_TK), :].astype(F32).T.astype(BF16)
            return carry
        lax.fori_loop(0, vt_ref.shape[0], body, 0)

    m_ref[...] = jnp.full(m_ref.shape, NEG_INF, F32)
    l_ref[...] = jnp.zeros(l_ref.shape, F32)
    acc_ref[...] = jnp.zeros(acc_ref.shape, F32)


def _attn_logits(k_ref, q_t, s_ref, blk, mask):
    off = pl.multiple_of(blk * ATT_TK, ATT_TK)
    s = _dot(k_ref[pl.ds(off, ATT_TK), :], q_t)
    if mask is not None:
        s = jnp.where(mask, s, NEG_INF)
    s_ref[...] = s


def _attn_consume(s_ref, vt_ref, blk, m_ref, l_ref, acc_ref):
    s = s_ref[...]
    m_prev = m_ref[...]
    m_new = jnp.maximum(m_prev, jnp.max(s, axis=0, keepdims=True))
    alpha = jnp.exp2(m_prev - m_new)
    p = jnp.exp2(s - m_new)
    l_ref[...] = alpha * l_ref[...] + jnp.sum(p, axis=0, keepdims=True)
    acc_ref[...] = alpha * acc_ref[...] + _dot(vt_ref[blk], p.astype(BF16))
    m_ref[...] = m_new


def _attn_pipeline(k_ref, vt_ref, q_t, s0_ref, s1_ref, m_ref, l_ref, acc_ref, n_full, diag_blk, diag_mask):
    def consume(s_ref, blk):
        _attn_consume(s_ref, vt_ref, blk, m_ref, l_ref, acc_ref)

    _attn_logits(k_ref, q_t, s0_ref, diag_blk, diag_mask)

    def body(t, carry):
        _attn_logits(k_ref, q_t, s1_ref, 2 * t, None)
        consume(s0_ref, jnp.where(t == 0, diag_blk, 2 * t - 1))
        _attn_logits(k_ref, q_t, s0_ref, 2 * t + 1, None)
        consume(s1_ref, 2 * t)
        return carry

    npairs = n_full // 2
    lax.fori_loop(0, npairs, body, 0)
    in_s0 = jnp.where(npairs == 0, diag_blk, 2 * npairs - 1)

    @pl.when(n_full % 2 == 1)
    def _():
        _attn_logits(k_ref, q_t, s1_ref, n_full - 1, None)
        consume(s0_ref, in_s0)
        consume(s1_ref, n_full - 1)

    @pl.when(n_full % 2 == 0)
    def _():
        consume(s0_ref, in_s0)


def _attn_a_kernel(q_ref, k_ref, v_ref, lam_ref, gain_ref, o_ref, vt_ref, s0_ref, s1_ref, m_ref, l_ref, acc_ref,
                   *, lam_init):
    _attn_init(v_ref, vt_ref, m_ref, l_ref, acc_ref)
    i = pl.program_id(2)
    tq = ATT_W
    q = (q_ref[...].astype(F32) * (A_HALF ** -0.5 * LOG2E)).T
    dim = lax.broadcasted_iota(jnp.int32, q.shape, 0)
    q_t = jnp.concatenate([jnp.where(dim < A_HALF, q, 0.0), jnp.where(dim >= A_HALF, q, 0.0)], axis=1).astype(BF16)
    key = lax.broadcasted_iota(jnp.int32, (ATT_TK, 2 * tq), 0)
    qry = lax.broadcasted_iota(jnp.int32, (ATT_TK, 2 * tq), 1) & (tq - 1)
    _attn_pipeline(k_ref, vt_ref, q_t, s0_ref, s1_ref, m_ref, l_ref, acc_ref, i, i, (key >> 6) <= (qry >> 6))

    lam_p = lam_ref[...]
    lam = (jnp.exp(jnp.sum(lam_p[0:1] * lam_p[1:2], axis=-1, keepdims=True))
           - jnp.exp(jnp.sum(lam_p[2:3] * lam_p[3:4], axis=-1, keepdims=True)) + lam_init)
    o_all = acc_ref[...] * (1.0 / l_ref[...])
    out = o_all[:, 0:tq] - lam * o_all[:, tq:2 * tq]
    out = out * lax.rsqrt(jnp.mean(out * out, axis=0, keepdims=True) + RMS_EPS) * (gain_ref[...] * (1.0 - lam_init))
    o_ref[...] = out.T.astype(BF16)


def _attn_scratch(seq, cols):
    return [pltpu.VMEM((seq // ATT_TK, HEAD_DIM, ATT_TK), BF16), pltpu.VMEM((ATT_TK, cols), F32),
            pltpu.VMEM((ATT_TK, cols), F32), pltpu.VMEM((1, cols), F32), pltpu.VMEM((1, cols), F32),
            pltpu.VMEM((HEAD_DIM, cols), F32)]


def _attn_a(z, lam_p, gain_col, batch, seq, lam_init):
    t = batch * seq
    nq = seq // ATT_W
    return pl.pallas_call(
        functools.partial(_attn_a_kernel, lam_init=lam_init),
        grid=(batch, N_HEADS, nq),
        in_specs=[
            pl.BlockSpec((ATT_W, HEAD_DIM), lambda b, h, i: (b * nq + i, h)),
            pl.BlockSpec((seq, HEAD_DIM), lambda b, h, i: (b, N_HEADS + h)),
            pl.BlockSpec((seq, HEAD_DIM), lambda b, h, i: (b, 2 * N_HEADS + h)),
            pl.BlockSpec((4, A_HALF), lambda b, h, i: (0, 0)),
            pl.BlockSpec((HEAD_DIM, 1), lambda b, h, i: (0, 0)),
        ],
        out_specs=pl.BlockSpec((ATT_W, HEAD_DIM), lambda b, h, i: (b * nq + i, h)),
        out_shape=jax.ShapeDtypeStruct((t, MIX_W), BF16),
        scratch_shapes=_attn_scratch(seq, 2 * ATT_W),
        compiler_params=_cparams(("parallel", "parallel", "arbitrary")),
        name="attn_a",
    )(z, z, z, lam_p, gain_col)


D_AUG = 2 * HEAD_DIM
BIAS_ROWS = 16


def _attn_d_kernel(q_ref, k_ref, v_ref, kx_ref, qx_ref, o_ref, kaug_ref, vt_ref, s0_ref, s1_ref, m_ref, l_ref,
                   acc_ref):
    _attn_init(v_ref, vt_ref, m_ref, l_ref, acc_ref)
    i = pl.program_id(2)

    @pl.when(i == 0)
    def _():
        def body(j, carry):
            off = pl.multiple_of(j * ATT_TK, ATT_TK)
            kaug_ref[pl.ds(off, ATT_TK), 0:HEAD_DIM] = k_ref[pl.ds(off, ATT_TK), :]
            kaug_ref[pl.ds(off, ATT_TK), HEAD_DIM:D_AUG] = kx_ref[0, pl.ds(off, ATT_TK), :]
            return carry
        lax.fori_loop(0, vt_ref.shape[0], body, 0)

    q = (q_ref[...].astype(F32) * (HEAD_DIM ** -0.5 * LOG2E)).T.astype(BF16)
    q_t = jnp.concatenate([q, qx_ref[0], jnp.zeros((D_AUG - HEAD_DIM - BIAS_ROWS, ATT_W), BF16)], axis=0)
    key = lax.broadcasted_iota(jnp.int32, (ATT_TK, ATT_W), 0)
    qry = lax.broadcasted_iota(jnp.int32, (ATT_TK, ATT_W), 1)
    _attn_pipeline(kaug_ref, vt_ref, q_t, s0_ref, s1_ref, m_ref, l_ref, acc_ref, i, i, key <= qry)
    o_ref[...] = (acc_ref[...] * (1.0 / l_ref[...])).T.astype(BF16)


def _attn_d(z, kx, qx, batch, seq):
    t = batch * seq
    nq = seq // ATT_W
    base = 4 * N_HEADS
    return pl.pallas_call(
        _attn_d_kernel,
        grid=(batch, N_HEADS, nq),
        in_specs=[
            pl.BlockSpec((ATT_W, HEAD_DIM), lambda b, h, i: (b * nq + i, base + h)),
            pl.BlockSpec((seq, HEAD_DIM), lambda b, h, i: (b, base + N_HEADS + h)),
            pl.BlockSpec((seq, HEAD_DIM), lambda b, h, i: (b, base + 2 * N_HEADS + h)),
            pl.BlockSpec((1, seq, HEAD_DIM), lambda b, h, i: (b * N_HEADS + h, 0, 0)),
            pl.BlockSpec((1, BIAS_ROWS, ATT_W), lambda b, h, i: (b * N_HEADS + h, 0, i)),
        ],
        out_specs=pl.BlockSpec((ATT_W, HEAD_DIM), lambda b, h, i: (b * nq + i, h)),
        out_shape=jax.ShapeDtypeStruct((t, MIX_W), BF16),
        scratch_shapes=[pltpu.VMEM((seq, D_AUG), BF16)] + _attn_scratch(seq, ATT_W),
        compiler_params=_cparams(("parallel", "parallel", "arbitrary")),
        name="attn_d",
    )(z, z, z, kx, qx)


def _forget_bias_operands(cum):
    c = cum * LOG2E
    hi_f = _bf16_part(c)
    mid_f = _bf16_part(c - hi_f)
    hi, mid, lo = hi_f.astype(BF16), mid_f.astype(BF16), (c - hi_f - mid_f).astype(BF16)
    one = jnp.ones_like(hi)
    kx = jnp.stack([hi, mid, lo, one, one, one], axis=-1)
    kx = jnp.pad(kx, ((0, 0), (0, 0), (0, HEAD_DIM - 6)))
    qx = jnp.stack([-one, -one, -one, hi, mid, lo], axis=1)
    qx = jnp.pad(qx, ((0, 0), (0, BIAS_ROWS - 6), (0, 0)))
    return kx, qx


def _logf_cumsum_kernel(x_ref, b_ref, o_ref, *, rows_per_seq):
    r = x_ref.shape[0]
    lf = _log_sigmoid(x_ref[...] + b_ref[...])
    a = lax.broadcasted_iota(jnp.int32, (LANES, LANES), 0)
    b = lax.broadcasted_iota(jnp.int32, (LANES, LANES), 1)
    within = _dot_exact(lf, (a <= b).astype(F32))
    totals = _dot_exact(lf, jnp.ones((LANES, LANES), F32))
    ra = lax.broadcasted_iota(jnp.int32, (r, r), 0)
    rb = lax.broadcasted_iota(jnp.int32, (r, r), 1)
    earlier = ((rb < ra) & ((rb // rows_per_seq) == (ra // rows_per_seq))).astype(F32)
    o_ref[...] = within + _dot_exact(earlier, totals)


def _logf_cumsum(x, bias_rows, rows_per_seq):
    r = x.shape[0]
    return pl.pallas_call(
        functools.partial(_logf_cumsum_kernel, rows_per_seq=rows_per_seq),
        out_shape=jax.ShapeDtypeStruct((r, LANES), F32),
        compiler_params=pltpu.CompilerParams(vmem_limit_bytes=VMEM_LIMIT),
        name="logf_cumsum",
    )(x, bias_rows)


CONV_TAIL = 8


def _causal_conv_silu(x, prev_tail, w):
    row = lax.broadcasted_iota(jnp.int32, prev_tail.shape, 0)
    acc = x * w[CONV_W - 1:CONV_W, :]
    for s in range(1, CONV_W):
        xs = pltpu.roll(x, s, axis=0)
        head = jnp.where(row >= s, xs[0:CONV_TAIL], pltpu.roll(prev_tail, s, axis=0))
        xs = jnp.concatenate([head, xs[CONV_TAIL:]], axis=0)
        acc = acc + xs * w[CONV_W - 1 - s:CONV_W - s, :]
    return _silu(acc)


def _chunk_masks():
    r = lax.broadcasted_iota(jnp.int32, (CHUNK, CHUNK), 0)
    c = lax.broadcasted_iota(jnp.int32, (CHUNK, CHUNK), 1)
    return r, c


REC_NB_MLSTM = 2
REC_NB_DELTANET = 4


def _rec_batch_block(batch, nb):
    return nb if batch % nb == 0 else 1


def _rec_specs(nb, cols):
    main = [pl.BlockSpec((nb, CHUNK, MIX_W), lambda b, c, j=j: (b, c, j)) for j in cols]
    return main + [pl.BlockSpec((nb, CHUNK, LANES), lambda b, c: (b, c, 0))]


def _mlstm_kernel(q_ref, k_ref, v_ref, og_ref, g_ref, conv_ref, bias_ref, norm_ref, out_ref,
                  pq_ref, pk_ref, c_ref, n_ref, m_ref):
    @pl.when(pl.program_id(1) == 0)
    def _():
        pq_ref[...] = jnp.zeros(pq_ref.shape, F32)
        pk_ref[...] = jnp.zeros(pk_ref.shape, F32)
        c_ref[...] = jnp.zeros(c_ref.shape, F32)
        n_ref[...] = jnp.zeros(n_ref.shape, F32)
        m_ref[...] = jnp.zeros(m_ref.shape, F32)

    lane = lax.broadcasted_iota(jnp.int32, (CHUNK, LANES), 1)
    r, c = _chunk_masks()
    causal = c <= r
    tri = causal.astype(F32)
    units = []
    for bi in range(q_ref.shape[0]):
        xq = q_ref[bi].astype(F32)
        xk = k_ref[bi].astype(F32)
        q_all = _causal_conv_silu(xq, pq_ref[bi], conv_ref[:, 0:MIX_W])
        k_all = _causal_conv_silu(xk, pk_ref[bi], conv_ref[:, MIX_W:2 * MIX_W]) * (HEAD_DIM ** -0.5)
        pq_ref[bi] = xq[CHUNK - CONV_TAIL:CHUNK]
        pk_ref[bi] = xk[CHUNK - CONV_TAIL:CHUNK]
        pre = g_ref[bi] + bias_ref[...]
        e = jnp.where(lane < N_HEADS, pre, _log_sigmoid(pre))
        cum = _dot_exact(tri, e)
        e_t = e.T
        cum_t = cum.T
        for h in range(N_HEADS):
            sl = slice(h * HEAD_DIM, (h + 1) * HEAD_DIM)
            u = dict(bi=bi, h=h, sl=sl, q=q_all[:, sl], k=k_all[:, sl], v=v_ref[bi, :, sl].astype(F32))
            u["qb"] = u["q"].astype(BF16)
            u["kb"] = u["k"].astype(BF16)
            i_col = e[:, h:h + 1]
            i_row = e_t[h:h + 1, :]
            b_col = cum[:, N_HEADS + h:N_HEADS + h + 1]
            b_row = cum_t[N_HEADS + h:N_HEADS + h + 1, :]
            b_last = b_col[CHUNK - 1:CHUNK, :]
            m_st = m_ref[bi, h:h + 1, 0:1]
            u["c_st"] = c_ref[bi, h]
            u["n_st"] = n_ref[bi, h:h + 1, :]
            d_log = jnp.where(causal, b_col - b_row + i_row, NEG_INF)
            inter_log = b_col + m_st
            u["m_t"] = jnp.maximum(inter_log, jnp.max(d_log, axis=-1, keepdims=True))
            u["d_w"] = jnp.exp(d_log - u["m_t"])
            u["inter_w"] = jnp.exp(inter_log - u["m_t"])
            w_log = b_last - b_col + i_col
            u["m_new"] = jnp.maximum(b_last + m_st, jnp.max(w_log, axis=0, keepdims=True))
            u["sw"] = jnp.exp(w_log - u["m_new"])
            u["decay"] = jnp.exp(b_last + m_st - u["m_new"])
            units.append(u)

    for u in units:
        u["qk"] = _dot_nt(u["qb"], u["kb"])
    for u in units:
        u["qc"] = _dot_nt(u["qb"], u["c_st"].astype(BF16))
    for u in units:
        u["upd"] = _dot((u["v"] * u["sw"]).T.astype(BF16), u["kb"])
    for u in units:
        u["a"] = u["qk"] * u["d_w"]
        u["av"] = _dot(u["a"].astype(BF16), u["v"].astype(BF16))
    for u in units:
        bi, h, sl = u["bi"], u["h"], u["sl"]
        num = u["av"] + u["inter_w"] * u["qc"]
        den = (jnp.sum(u["a"], axis=-1, keepdims=True)
               + u["inter_w"] * jnp.sum(u["q"] * u["n_st"], axis=-1, keepdims=True))
        hh = num / jnp.maximum(jnp.abs(den), jnp.exp(-u["m_t"]))
        c_ref[bi, h] = u["decay"] * u["c_st"] + u["upd"]
        n_ref[bi, h:h + 1, :] = u["decay"] * u["n_st"] + jnp.sum(u["k"] * u["sw"], axis=0, keepdims=True)
        m_ref[bi, h:h + 1, :] = jnp.broadcast_to(u["m_new"], (1, LANES))
        gate = _sigmoid(og_ref[bi, :, sl].astype(F32))
        out_ref[bi, :, sl] = (_rms(hh, norm_ref[...]) * gate).astype(BF16)


def _mlstm(z, zg, conv_w, gate_bias, norm, batch, seq):
    nb = _rec_batch_block(batch, REC_NB_MLSTM)
    full = lambda s: pl.BlockSpec(s, lambda b, c: (0, 0))
    out = pl.pallas_call(
        _mlstm_kernel,
        grid=(batch // nb, seq // CHUNK),
        in_specs=_rec_specs(nb, (3, 4, 5, 6)) + [full((CONV_W, 2 * MIX_W)), full((1, LANES)), full((1, HEAD_DIM))],
        out_specs=pl.BlockSpec((nb, CHUNK, MIX_W), lambda b, c: (b, c, 0)),
        out_shape=jax.ShapeDtypeStruct((batch, seq, MIX_W), BF16),
        scratch_shapes=[pltpu.VMEM((nb, CONV_TAIL, MIX_W), F32), pltpu.VMEM((nb, CONV_TAIL, MIX_W), F32),
                        pltpu.VMEM((nb, N_HEADS, HEAD_DIM, HEAD_DIM), F32), pltpu.VMEM((nb, 8, HEAD_DIM), F32),
                        pltpu.VMEM((nb, 8, LANES), F32)],
        compiler_params=_cparams(("parallel", "arbitrary")),
        name="mlstm",
    )(*([z.reshape(batch, seq, Z_MAIN)] * 4), zg.reshape(batch, seq, LANES), conv_w, gate_bias, norm)
    return out.reshape(batch * seq, MIX_W)


def _bdot(a, b):
    return _dot(a.astype(BF16), b.astype(BF16))


def _unit_lower_inverses(l_stricts, r, c):
    same16 = (r >> 4) == (c >> 4)
    same32 = (r >> 5) == (c >> 5)
    eye = (r == c).astype(F32)
    ps = [jnp.where(same16, -l, 0.0) for l in l_stricts]
    invs = [eye + p for p in ps]
    for _ in range(3):
        ps = [_bdot(p, p) for p in ps]
        invs = [inv + _bdot(inv, p) for inv, p in zip(invs, ps)]
    for off_mask in (same32 & jnp.logical_not(same16), jnp.logical_not(same32)):
        ts = [_bdot(jnp.where(off_mask, l, 0.0), inv) for l, inv in zip(l_stricts, invs)]
        invs = [inv - _bdot(inv, t) for inv, t in zip(invs, ts)]
    return invs


def _deltanet_kernel(q_ref, k_ref, v_ref, og_ref, g_ref, conv_ref, par_ref, norm_ref, out_ref,
                     pq_ref, pk_ref, pv_ref, s_ref):
    @pl.when(pl.program_id(1) == 0)
    def _():
        pq_ref[...] = jnp.zeros(pq_ref.shape, F32)
        pk_ref[...] = jnp.zeros(pk_ref.shape, F32)
        pv_ref[...] = jnp.zeros(pv_ref.shape, F32)
        s_ref[...] = jnp.zeros(s_ref.shape, F32)

    lane = lax.broadcasted_iota(jnp.int32, (CHUNK, LANES), 1)
    r, c = _chunk_masks()
    tri = (c <= r).astype(F32)
    units = []
    for bi in range(q_ref.shape[0]):
        xq = q_ref[bi].astype(F32)
        xk = k_ref[bi].astype(F32)
        xv = v_ref[bi].astype(F32)
        q_all = _causal_conv_silu(xq, pq_ref[bi], conv_ref[:, 0:MIX_W])
        k_all = _causal_conv_silu(xk, pk_ref[bi], conv_ref[:, MIX_W:2 * MIX_W])
        v_all = _causal_conv_silu(xv, pv_ref[bi], conv_ref[:, 2 * MIX_W:3 * MIX_W])
        pq_ref[bi] = xq[CHUNK - CONV_TAIL:CHUNK]
        pk_ref[bi] = xk[CHUNK - CONV_TAIL:CHUNK]
        pv_ref[bi] = xv[CHUNK - CONV_TAIL:CHUNK]
        pre = g_ref[bi]
        e = jnp.where(lane < N_HEADS, -jnp.exp(par_ref[0:1, :]) * _softplus(pre + par_ref[1:2, :]), _sigmoid(pre))
        cum = _dot_exact(tri, e)
        cum_t = cum.T
        for h in range(N_HEADS):
            sl = slice(h * HEAD_DIM, (h + 1) * HEAD_DIM)
            q = q_all[:, sl]
            k = k_all[:, sl]
            q = q * lax.rsqrt(jnp.sum(q * q, axis=-1, keepdims=True) + RMS_EPS) * (HEAD_DIM ** -0.5)
            k = k * lax.rsqrt(jnp.sum(k * k, axis=-1, keepdims=True) + RMS_EPS)
            beta = e[:, N_HEADS + h:N_HEADS + h + 1]
            b_col = cum[:, h:h + 1]
            b_row = cum_t[h:h + 1, :]
            b_last = b_col[CHUNK - 1:CHUNK, :]
            e_col = jnp.exp(b_col)
            kbeta = k * beta
            units.append(dict(
                bi=bi, h=h, sl=sl, k=k, kbeta=kbeta, q_dec=q * e_col,
                gam=jnp.exp(jnp.where(c <= r, b_col - b_row, NEG_INF)),
                rhs=jnp.concatenate([v_all[:, sl] * beta, kbeta * e_col], axis=1),
                k_dec=k * jnp.exp(b_last - b_col), chunk_decay=jnp.exp(b_last),
                kq=jnp.concatenate([kbeta, q], axis=0)))

    for u in units:
        both = _dot_nt(u["kq"].astype(BF16), u["k"].astype(BF16)) * jnp.concatenate([u["gam"], u["gam"]], axis=0)
        u["l"] = jnp.where(c < r, both[0:CHUNK], 0.0)
        u["qk"] = both[CHUNK:2 * CHUNK]
    invs = _unit_lower_inverses([u["l"] for u in units], r, c)
    for u, inv in zip(units, invs):
        sol = _bdot(inv, u["rhs"])
        u["u"] = sol[:, 0:HEAD_DIM]
        u["w"] = sol[:, HEAD_DIM:2 * HEAD_DIM]
    for u in units:
        u["st"] = s_ref[u["bi"], u["h"]]
        u["ws"] = _bdot(jnp.concatenate([u["w"], u["q_dec"]], axis=0), u["st"])
    for u in units:
        u["v_new"] = u["u"] - u["ws"][0:CHUNK]
        u["o"] = u["ws"][CHUNK:2 * CHUNK] + _bdot(u["qk"], u["v_new"])
    for u in units:
        s_ref[u["bi"], u["h"]] = u["chunk_decay"] * u["st"] + _bdot(u["k_dec"].T, u["v_new"])
    for u in units:
        bi, sl = u["bi"], u["sl"]
        gate = _silu(og_ref[bi, :, sl].astype(F32))
        out_ref[bi, :, sl] = (_rms(u["o"], norm_ref[...]) * gate).astype(BF16)


def _deltanet(z, zg, conv_w, params, norm, batch, seq):
    nb = _rec_batch_block(batch, REC_NB_DELTANET)
    full = lambda s: pl.BlockSpec(s, lambda b, c: (0, 0))
    out = pl.pallas_call(
        _deltanet_kernel,
        grid=(batch // nb, seq // CHUNK),
        in_specs=_rec_specs(nb, (0, 1, 2, 3)) + [full((CONV_W, 3 * MIX_W)), full((2, LANES)), full((1, HEAD_DIM))],
        out_specs=pl.BlockSpec((nb, CHUNK, MIX_W), lambda b, c: (b, c, 0)),
        out_shape=jax.ShapeDtypeStruct((batch, seq, MIX_W), BF16),
        scratch_shapes=[pltpu.VMEM((nb, CONV_TAIL, MIX_W), F32)] * 3
        + [pltpu.VMEM((nb, N_HEADS, HEAD_DIM, HEAD_DIM), F32)],
        compiler_params=_cparams(("parallel", "arbitrary")),
        name="deltanet",
    )(*([z.reshape(batch, seq, Z_MAIN)] * 4), zg.reshape(batch, seq, LANES), conv_w, params, norm)
    return out.reshape(batch * seq, MIX_W)


OUT_TM = 512
G_LANE0 = N_EXPERTS
ROUTE_W_LANE = 4
ROW_TILES = D_MODEL // LANES


def _outproj_router_kernel(h_ref, a_ref, b_ref, wo_ref, gn_ref, whi_ref, wlo_ref, br_ref,
                           hout_ref, xn_ref, comb_ref):
    h_new = h_ref[...] + _dot(a_ref[...], wo_ref[0:MIX_W, :]) + _dot(b_ref[...], wo_ref[MIX_W:2 * MIX_W, :])
    hout_ref[...] = h_new
    hn = _rms(h_new, gn_ref[...])
    for c in range(ROW_TILES):
        xn_ref[:, c, :] = hn[:, c * LANES:(c + 1) * LANES]
    hi_f = _bf16_part(hn)
    hi = hi_f.astype(BF16)
    lo = (hn - hi_f).astype(BF16)
    logits = (_dot(hi, whi_ref[...]) + _dot(lo, whi_ref[...]) + _dot(hi, wlo_ref[...]) + _dot(lo, wlo_ref[...])
              + br_ref[...])

    lane_i = lax.broadcasted_iota(jnp.int32, logits.shape, 1)
    lane = lane_i.astype(F32)
    big = 1e9
    is_group = (lane_i >= G_LANE0) & (lane_i < G_LANE0 + N_GROUPS)
    gl = jnp.where(is_group, logits, NEG_INF)
    gmax = jnp.max(gl, axis=-1, keepdims=True)
    g_sel = jnp.min(jnp.where(gl == gmax, lane, big), axis=-1, keepdims=True) - G_LANE0
    p_top = 1.0 / jnp.sum(jnp.exp(gl - gmax), axis=-1, keepdims=True)
    group_of_lane = (lane_i >> 2).astype(F32)
    in_group = (lane_i < N_EXPERTS) & (group_of_lane == g_sel)
    el = jnp.where(in_group, logits, NEG_INF)
    v1 = jnp.max(el, axis=-1, keepdims=True)
    i1 = jnp.min(jnp.where(el == v1, lane, big), axis=-1, keepdims=True)
    el2 = jnp.where(lane == i1, NEG_INF, el)
    v2 = jnp.max(el2, axis=-1, keepdims=True)
    i2 = jnp.min(jnp.where(el2 == v2, lane, big), axis=-1, keepdims=True)
    e21 = jnp.exp(v2 - v1)
    w1 = p_top / (1.0 + e21)
    local = lane - ROUTE_W_LANE + EXPERTS_PER_GROUP * g_sel
    comb_ref[...] = (jnp.where(lane_i == 0, g_sel, 0.0) + jnp.where(local == i1, w1, 0.0)
                     + jnp.where(local == i2, w1 * e21, 0.0))


def _outproj_router(h, oa, ob, w_out, gain, wr_hi, wr_lo, b_r):
    t = h.shape[0]
    row = lambda w: pl.BlockSpec((OUT_TM, w), lambda i: (i, 0))
    full = lambda s: pl.BlockSpec(s, lambda i: (0, 0))
    return pl.pallas_call(
        _outproj_router_kernel,
        grid=(t // OUT_TM,),
        in_specs=[row(D_MODEL), row(MIX_W), row(MIX_W), full((D_MODEL, D_MODEL)), full((1, D_MODEL)),
                  full((D_MODEL, LANES)), full((D_MODEL, LANES)), full((1, LANES))],
        out_specs=[row(D_MODEL), pl.BlockSpec((OUT_TM, ROW_TILES, LANES), lambda i: (i, 0, 0)), row(LANES)],
        out_shape=[jax.ShapeDtypeStruct((t, D_MODEL), F32), jax.ShapeDtypeStruct((t, ROW_TILES, LANES), F32),
                   jax.ShapeDtypeStruct((t, LANES), F32)],
        compiler_params=_cparams(("parallel",)),
        name="outproj_router",
    )(h, oa, ob, w_out, gain, wr_hi, wr_lo, b_r)


MOE_TM = 512


def _moe_plan(route, t):
    n_rows = t + (N_GROUPS + 1) * MOE_TM
    n_tiles = n_rows // MOE_TM
    group = route[:, 0].astype(jnp.int32)
    onehot = (group[:, None] == jnp.arange(N_GROUPS, dtype=jnp.int32)[None, :]).astype(jnp.int32)
    csum = jnp.cumsum(onehot, axis=0)
    rank = jnp.sum(csum * onehot, axis=1) - 1
    padded = ((csum[-1] + MOE_TM - 1) // MOE_TM) * MOE_TM
    ends = jnp.cumsum(padded)
    pos = jnp.sum((ends - padded)[None, :] * onehot, axis=1) + rank
    record = jnp.concatenate([jnp.arange(1, t + 1, dtype=F32)[:, None],
                              route[:, ROUTE_W_LANE:ROUTE_W_LANE + EXPERTS_PER_GROUP]], axis=1)
    record = jnp.zeros((n_rows, 1 + EXPERTS_PER_GROUP), F32).at[pos].set(record, unique_indices=True)
    real = record[:, 0].astype(jnp.int32) - 1
    is_pad = real < 0
    dst = jnp.where(is_pad, t + jnp.cumsum(is_pad.astype(jnp.int32)) - 1, real)
    w_sorted = record[:, 1:]
    starts = jnp.arange(n_tiles, dtype=jnp.int32) * MOE_TM
    tile_group = jnp.minimum(jnp.sum((starts[:, None] >= ends[None, :]).astype(jnp.int32), axis=1), N_GROUPS - 1)
    return tile_group, dst.reshape(n_tiles, MOE_TM), w_sorted, pos


def _gather_rows_step(idx_of, src_hbm, buf, sem, tile_rows):
    t = pl.program_id(0)
    nt = pl.num_programs(0)
    slot = t % 2
    other = 1 - slot

    def start(tile, b):
        def body(i, carry):
            for u in range(8):
                r = i * 8 + u
                pltpu.make_async_copy(src_hbm.at[idx_of(tile, r)], buf.at[b, r], sem.at[b]).start(priority=u % 2)
            return carry
        lax.fori_loop(0, tile_rows // 8, body, 0)

    def wait(b):
        pltpu.make_async_copy(src_hbm.at[pl.ds(0, tile_rows)], buf.at[b], sem.at[b]).wait()

    @pl.when(t == 0)
    def _():
        start(0, 0)

    start(jnp.where(t == nt - 1, 0, t + 1), other)
    wait(slot)

    def finish():
        @pl.when(t == nt - 1)
        def _():
            wait(other)

    return slot, finish


def _moe_kernel(tg_ref, dst_ref, x_hbm, ws_ref, wg_ref, wu_ref, wd_ref, out_ref, xbuf, sem_g, *, n_tok):
    def token_of(tile, r):
        d = dst_ref[tile, r]
        return jnp.where(d < n_tok, d, 0)

    slot, finish = _gather_rows_step(token_of, x_hbm, xbuf, sem_g, MOE_TM)
    x = jnp.concatenate([xbuf[slot, :, c, :] for c in range(ROW_TILES)], axis=1).astype(BF16)
    ws = ws_ref[...]
    y = jnp.zeros((MOE_TM, D_MODEL), F32)
    for e in range(EXPERTS_PER_GROUP):
        he = _silu(_dot(x, wg_ref[e])) * _dot(x, wu_ref[e])
        y = y + _dot((he * ws[:, e:e + 1]).astype(BF16), wd_ref[e])
    for c in range(ROW_TILES):
        out_ref[:, c, :] = y[:, c * LANES:(c + 1) * LANES]
    finish()


def _moe(xn, route, wg, wu, wd):
    t = xn.shape[0]
    tile_group, dst, w_sorted, pos = _moe_plan(route, t)
    n_tiles = dst.shape[0]
    buf = pltpu.VMEM((2, MOE_TM, ROW_TILES, LANES), F32)
    grid_spec = pltpu.PrefetchScalarGridSpec(
        num_scalar_prefetch=2,
        grid=(n_tiles,),
        in_specs=[
            pl.BlockSpec(memory_space=pl.ANY),
            pl.BlockSpec((MOE_TM, EXPERTS_PER_GROUP), lambda i, tg, dst: (i, 0)),
            pl.BlockSpec((EXPERTS_PER_GROUP, D_MODEL, EXPERT_FF), lambda i, tg, dst: (tg[i], 0, 0)),
            pl.BlockSpec((EXPERTS_PER_GROUP, D_MODEL, EXPERT_FF), lambda i, tg, dst: (tg[i], 0, 0)),
            pl.BlockSpec((EXPERTS_PER_GROUP, EXPERT_FF, D_MODEL), lambda i, tg, dst: (tg[i], 0, 0)),
        ],
        out_specs=pl.BlockSpec((MOE_TM, ROW_TILES, LANES), lambda i, tg, dst: (i, 0, 0)),
        scratch_shapes=[buf, pltpu.SemaphoreType.DMA((2,))],
    )
    y_sorted = pl.pallas_call(
        functools.partial(_moe_kernel, n_tok=t),
        grid_spec=grid_spec,
        out_shape=jax.ShapeDtypeStruct((dst.size, ROW_TILES, LANES), F32),
        compiler_params=_cparams(("arbitrary",)),
        name="moe",
    )(tile_group, dst, xn, w_sorted, wg, wu, wd)
    return y_sorted, pos


PLE_TM = 512


def _ple_kernel(pos_ref, h_ref, y_hbm, p_ref, wg_ref, wp_ref, gn_ref, out_ref, ybuf, sem, *, final_norm):
    slot, finish = _gather_rows_step(lambda tile, r: pos_ref[tile * PLE_TM + r], y_hbm, ybuf, sem, PLE_TM)
    h = h_ref[...] + jnp.concatenate([ybuf[slot, :, c, :] for c in range(ROW_TILES)], axis=1)
    gate = _sigmoid(_dot(h.astype(BF16), wg_ref[...]))
    out = h + gate * _dot(p_ref[...].astype(BF16), wp_ref[...])
    if final_norm:
        out = _rms(out, gn_ref[...])
    out_ref[...] = out
    finish()


def _ple(h, y_sorted, pos, p, w_gate, w_proj, gain, final_norm):
    t = h.shape[0]
    grid_spec = pltpu.PrefetchScalarGridSpec(
        num_scalar_prefetch=1,
        grid=(t // PLE_TM,),
        in_specs=[
            pl.BlockSpec((PLE_TM, D_MODEL), lambda i, pos: (i, 0)),
            pl.BlockSpec(memory_space=pl.ANY),
            pl.BlockSpec((PLE_TM, PLE_DIM), lambda i, pos: (i, 0)),
            pl.BlockSpec((D_MODEL, D_MODEL), lambda i, pos: (0, 0)),
            pl.BlockSpec((PLE_DIM, D_MODEL), lambda i, pos: (0, 0)),
            pl.BlockSpec((1, D_MODEL), lambda i, pos: (0, 0)),
        ],
        out_specs=pl.BlockSpec((PLE_TM, D_MODEL), lambda i, pos: (i, 0)),
        scratch_shapes=[pltpu.VMEM((2, PLE_TM, ROW_TILES, LANES), F32), pltpu.SemaphoreType.DMA((2,))],
    )
    return pl.pallas_call(
        functools.partial(_ple_kernel, final_norm=final_norm),
        grid_spec=grid_spec,
        out_shape=jax.ShapeDtypeStruct((t, D_MODEL), F32),
        compiler_params=_cparams(("arbitrary",)),
        name="ple",
    )(pos, h, y_sorted, p, w_gate, w_proj, gain)


def _pad_lanes(x):
    return jnp.pad(x, ((0, 0), (0, LANES - x.shape[-1])))


def _row(x):
    return x.reshape(1, -1).astype(F32)


def kernel(x, p, norm_mix, norm_ffn, norm_final, ab_w_in, ab_w_out, a_lam_q1, a_lam_k1, a_lam_q2, a_lam_k2,
           a_subln, b_conv, b_igate_bias, b_fgate_bias, b_norm, cd_w_in, cd_w_out, c_conv, c_a_log, c_dt_bias,
           c_norm, d_fgate_bias, moe_w_group, moe_b_group, moe_w_router, moe_b_router, moe_w_gate, moe_w_up,
           moe_w_down, ple_w_gate, ple_w_proj):
    batch, seq, _ = x.shape
    depth = p.shape[0]
    t = batch * seq
    assert seq % IN_TM == 0 and seq % ATT_TK == 0 and t % MOE_TM == 0
    h = x.reshape(t, D_MODEL)
    cos, sin = _rope_tables(seq)
    ones_tab = jnp.ones((seq, LANES), F32)

    for i in range(depth):
        j = i // 2
        if i % 2 == 0:
            w_in = ab_w_in[j]
            z, zg = _inproj(h, _row(norm_mix[i]), w_in[:, :Z_MAIN].astype(BF16),
                            _pad_lanes(w_in[:, Z_MAIN:]).astype(BF16), cos, sin, seq, rope_chunks=2)
            lam_init = 0.8 - 0.6 * math.exp(-0.3 * i)
            lam_p = jnp.stack([a_lam_q1[j], a_lam_k1[j], a_lam_q2[j], a_lam_k2[j]]).astype(F32)
            out_1 = _attn_a(z, lam_p, a_subln[j].reshape(HEAD_DIM, 1).astype(F32), batch, seq, lam_init)
            gate_bias = _pad_lanes(jnp.concatenate([b_igate_bias[j], b_fgate_bias[j]]).reshape(1, -1))
            out_2 = _mlstm(z, zg, b_conv[j], gate_bias, _row(b_norm[j]), batch, seq)
            w_out = ab_w_out[j]
        else:
            w_in = cd_w_in[j]
            c_main, d_main = 4 * MIX_W, 3 * MIX_W
            w_main = jnp.concatenate([w_in[:, :c_main], w_in[:, c_main + 8:c_main + 8 + d_main]], axis=1)
            w_small = jnp.concatenate([w_in[:, c_main:c_main + 8], w_in[:, c_main + 8 + d_main:]], axis=1)
            z, zg = _inproj(h, _row(norm_mix[i]), w_main.astype(BF16), _pad_lanes(w_small).astype(BF16),
                            ones_tab, ones_tab, seq, rope_chunks=0)
            params = jnp.stack([_pad_lanes(c_a_log[j].reshape(1, -1))[0], _pad_lanes(c_dt_bias[j].reshape(1, -1))[0]])
            out_1 = _deltanet(z, zg, c_conv[j], params, _row(c_norm[j]), batch, seq)
            fd = zg[:, 8:8 + N_HEADS].reshape(batch, seq, N_HEADS).transpose(0, 2, 1)
            bias_rows = jnp.repeat(jnp.tile(d_fgate_bias[j], batch), seq // LANES).reshape(-1, 1)
            cum = _logf_cumsum(fd.reshape(-1, LANES), jnp.broadcast_to(bias_rows, (bias_rows.shape[0], LANES)),
                               seq // LANES)
            kx, qx = _forget_bias_operands(cum.reshape(batch * N_HEADS, seq))
            out_2 = _attn_d(z, kx, qx, batch, seq)
            w_out = cd_w_out[j]

        w_r = _pad_lanes(jnp.concatenate([moe_w_router[i], moe_w_group[i]], axis=1))
        w_r_hi_f = _bf16_part(w_r)
        w_r_hi = w_r_hi_f.astype(BF16)
        w_r_lo = (w_r - w_r_hi_f).astype(BF16)
        b_r = _pad_lanes(jnp.concatenate([moe_b_router[i], moe_b_group[i]]).reshape(1, -1))
        h, xn, comb = _outproj_router(h, out_1, out_2, w_out.astype(BF16), _row(norm_ffn[i]), w_r_hi, w_r_lo, b_r)
        y, pos = _moe(xn, comb, moe_w_gate[i].astype(BF16), moe_w_up[i].astype(BF16), moe_w_down[i].astype(BF16))
        h = _ple(h, y, pos, p[i].reshape(t, PLE_DIM), ple_w_gate[i].astype(BF16), ple_w_proj[i].astype(BF16),
                 _row(norm_final), final_norm=(i == depth - 1))
    return h.reshape(batch, seq, D_MODEL)
```

```python
import functools
import math

import jax
import jax.numpy as jnp
from jax import lax
from jax.experimental import pallas as pl
from jax.experimental.pallas import tpu as pltpu

F32 = jnp.float32
BF16 = jnp.bfloat16
HIGHEST = lax.Precision.HIGHEST

D_MODEL = 1024
HEAD_DIM = 128
N_HEADS = 4
MIX_W = N_HEADS * HEAD_DIM
A_HALF = HEAD_DIM // 2
CHUNK = 64
CONV_W = 4
RMS_EPS = 1e-6
PLE_DIM = 256
N_GROUPS = 4
EXPERTS_PER_GROUP = 4
N_EXPERTS = 16
EXPERT_FF = D_MODEL // 2
ROPE_THETA = 10000.0
Z_MAIN = 7 * MIX_W
LANES = 128

VMEM_LIMIT = 56 * 1024 * 1024

NEG_INF = float("-inf")


def _cparams(sem):
    return pltpu.CompilerParams(dimension_semantics=sem, vmem_limit_bytes=VMEM_LIMIT)


def _dot(a, b):
    return jnp.dot(a, b, preferred_element_type=F32)


def _dot_nt(a, b):
    return lax.dot_general(a, b, (((1,), (1,)), ((), ())), preferred_element_type=F32)


def _dot_exact(a, b):
    return jnp.dot(a, b, preferred_element_type=F32, precision=HIGHEST)


def _bf16_part(x):
    bits = lax.bitcast_convert_type(x, jnp.int32) & jnp.int32(-65536)
    return lax.bitcast_convert_type(bits, F32)


def _sigmoid(x):
    return 1.0 / (1.0 + jnp.exp(-x))


def _silu(x):
    return x * _sigmoid(x)


def _log_sigmoid(x):
    return jnp.minimum(x, 0.0) - jnp.log(1.0 + jnp.exp(-jnp.abs(x)))


def _softplus(x):
    return jnp.maximum(x, 0.0) + jnp.log(1.0 + jnp.exp(-jnp.abs(x)))


def _rms(x, gain):
    return x * lax.rsqrt(jnp.mean(x * x, axis=-1, keepdims=True) + RMS_EPS) * gain


IN_TM = 512
IN_TN = 512


def _inproj_kernel(h_ref, g_ref, w_ref, wg_ref, cos_ref, sin_ref, z_ref, zg_ref, *, rope_chunks):
    hn = _rms(h_ref[...], g_ref[...])
    hb = hn.astype(BF16)
    zg_ref[...] = _dot(hb, wg_ref[...])
    if rope_chunks:
        lane = lax.broadcasted_iota(jnp.int32, (IN_TM, IN_TN), 1)
        first_half = (lane & 32) == 0
        cos = jnp.concatenate([cos_ref[...]] * (IN_TN // LANES), axis=1)
        sin = jnp.concatenate([sin_ref[...]] * (IN_TN // LANES), axis=1)
    for c in range(Z_MAIN // IN_TN):
        zc = _dot(hb, w_ref[:, c * IN_TN:(c + 1) * IN_TN])
        if c < rope_chunks:
            partner = jnp.where(first_half, pltpu.roll(zc, IN_TN - 32, axis=1), pltpu.roll(zc, 32, axis=1))
            zc = zc * cos + partner * sin
        z_ref[:, c * IN_TN:(c + 1) * IN_TN] = zc.astype(BF16)


def _inproj(h, gain, w_main, w_gate, cos, sin, seq, rope_chunks):
    t = h.shape[0]
    per_seq = seq // IN_TM
    return pl.pallas_call(
        functools.partial(_inproj_kernel, rope_chunks=rope_chunks),
        grid=(t // IN_TM,),
        in_specs=[
            pl.BlockSpec((IN_TM, D_MODEL), lambda i: (i, 0)),
            pl.BlockSpec((1, D_MODEL), lambda i: (0, 0)),
            pl.BlockSpec((D_MODEL, Z_MAIN), lambda i: (0, 0)),
            pl.BlockSpec((D_MODEL, LANES), lambda i: (0, 0)),
            pl.BlockSpec((IN_TM, LANES), lambda i: (i % per_seq, 0)),
            pl.BlockSpec((IN_TM, LANES), lambda i: (i % per_seq, 0)),
        ],
        out_specs=[
            pl.BlockSpec((IN_TM, Z_MAIN), lambda i: (i, 0)),
            pl.BlockSpec((IN_TM, LANES), lambda i: (i, 0)),
        ],
        out_shape=[jax.ShapeDtypeStruct((t, Z_MAIN), BF16), jax.ShapeDtypeStruct((t, LANES), F32)],
        compiler_params=_cparams(("parallel",)),
        name="inproj",
    )(h, gain, w_main, w_gate, cos, sin)


def _rope_tables(seq):
    inv = ROPE_THETA ** (-jnp.arange(0, A_HALF, 2, dtype=F32) / A_HALF)
    ang = jnp.arange(seq, dtype=F32)[:, None] * inv[None, :]
    cos, sin = jnp.cos(ang), jnp.sin(ang)
    return jnp.tile(cos, (1, 4)), jnp.tile(jnp.concatenate([-sin, sin], axis=1), (1, 2))


ATT_W = 512
ATT_TK = 512
LOG2E = 1.4426950408889634


def _attn_init(v_ref, vt_ref, m_ref, l_ref, acc_ref):
    @pl.when(pl.program_id(2) == 0)
    def _():
        def body(j, carry):
            off = pl.multiple_of(j * ATT_TK, ATT_TK)
            vt_ref[j] = v_ref[pl.ds(off, ATT_TK), :].astype(F32).T.astype(BF16)
            return carry
        lax.fori_loop(0, vt_ref.shape[0], body, 0)

    m_ref[...] = jnp.full(m_ref.shape, NEG_INF, F32)
    l_ref[...] = jnp.zeros(l_ref.shape, F32)
    acc_ref[...] = jnp.zeros(acc_ref.shape, F32)


def _attn_logits(k_ref, q_t, s_ref, blk, mask):
    off = pl.multiple_of(blk * ATT_TK, ATT_TK)
    s = _dot(k_ref[pl.ds(off, ATT_TK), :], q_t)
    if mask is not None:
        s = jnp.where(mask, s, NEG_INF)
    s_ref[...] = s


def _attn_consume(s_ref, vt_ref, blk, m_ref, l_ref, acc_ref):
    s = s_ref[...]
    m_prev = m_ref[...]
    m_new = jnp.maximum(m_prev, jnp.max(s, axis=0, keepdims=True))
    alpha = jnp.exp2(m_prev - m_new)
    p = jnp.exp2(s - m_new)
    l_ref[...] = alpha * l_ref[...] + jnp.sum(p, axis=0, keepdims=True)
    acc_ref[...] = alpha * acc_ref[...] + _dot(vt_ref[blk], p.astype(BF16))
    m_ref[...] = m_new


def _attn_pipeline(k_ref, vt_ref, q_t, s0_ref, s1_ref, m_ref, l_ref, acc_ref, n_full, diag_blk, diag_mask):
    def consume(s_ref, blk):
        _attn_consume(s_ref, vt_ref, blk, m_ref, l_ref, acc_ref)

    _attn_logits(k_ref, q_t, s0_ref, diag_blk, diag_mask)

    def body(t, carry):
        _attn_logits(k_ref, q_t, s1_ref, 2 * t, None)
        consume(s0_ref, jnp.where(t == 0, diag_blk, 2 * t - 1))
        _attn_logits(k_ref, q_t, s0_ref, 2 * t + 1, None)
        consume(s1_ref, 2 * t)
        return carry

    npairs = n_full // 2
    lax.fori_loop(0, npairs, body, 0)
    in_s0 = jnp.where(npairs == 0, diag_blk, 2 * npairs - 1)

    @pl.when(n_full % 2 == 1)
    def _():
        _attn_logits(k_ref, q_t, s1_ref, n_full - 1, None)
        consume(s0_ref, in_s0)
        consume(s1_ref, n_full - 1)

    @pl.when(n_full % 2 == 0)
    def _():
        consume(s0_ref, in_s0)


def _attn_a_kernel(q_ref, k_ref, v_ref, lam_ref, gain_ref, o_ref, vt_ref, s0_ref, s1_ref, m_ref, l_ref, acc_ref,
                   *, lam_init):
    _attn_init(v_ref, vt_ref, m_ref, l_ref, acc_ref)
    i = pl.program_id(2)
    tq = ATT_W
    q = (q_ref[...].astype(F32) * (A_HALF ** -0.5 * LOG2E)).T
    dim = lax.broadcasted_iota(jnp.int32, q.shape, 0)
    q_t = jnp.concatenate([jnp.where(dim < A_HALF, q, 0.0), jnp.where(dim >= A_HALF, q, 0.0)], axis=1).astype(BF16)
    key = lax.broadcasted_iota(jnp.int32, (ATT_TK, 2 * tq), 0)
    qry = lax.broadcasted_iota(jnp.int32, (ATT_TK, 2 * tq), 1) & (tq - 1)
    _attn_pipeline(k_ref, vt_ref, q_t, s0_ref, s1_ref, m_ref, l_ref, acc_ref, i, i, (key >> 6) <= (qry >> 6))

    lam_p = lam_ref[...]
    lam = (jnp.exp(jnp.sum(lam_p[0:1] * lam_p[1:2], axis=-1, keepdims=True))
           - jnp.exp(jnp.sum(lam_p[2:3] * lam_p[3:4], axis=-1, keepdims=True)) + lam_init)
    o_all = acc_ref[...] * (1.0 / l_ref[...])
    out = o_all[:, 0:tq] - lam * o_all[:, tq:2 * tq]
    out = out * lax.rsqrt(jnp.mean(out * out, axis=0, keepdims=True) + RMS_EPS) * (gain_ref[...] * (1.0 - lam_init))
    o_ref[...] = out.T.astype(BF16)


def _attn_scratch(seq, cols):
    return [pltpu.VMEM((seq // ATT_TK, HEAD_DIM, ATT_TK), BF16), pltpu.VMEM((ATT_TK, cols), F32),
            pltpu.VMEM((ATT_TK, cols), F32), pltpu.VMEM((1, cols), F32), pltpu.VMEM((1, cols), F32),
            pltpu.VMEM((HEAD_DIM, cols), F32)]


def _attn_a(z, lam_p, gain_col, batch, seq, lam_init):
    t = batch * seq
    nq = seq // ATT_W
    return pl.pallas_call(
        functools.partial(_attn_a_kernel, lam_init=lam_init),
        grid=(batch, N_HEADS, nq),
        in_specs=[
            pl.BlockSpec((ATT_W, HEAD_DIM), lambda b, h, i: (b * nq + i, h)),
            pl.BlockSpec((seq, HEAD_DIM), lambda b, h, i: (b, N_HEADS + h)),
            pl.BlockSpec((seq, HEAD_DIM), lambda b, h, i: (b, 2 * N_HEADS + h)),
            pl.BlockSpec((4, A_HALF), lambda b, h, i: (0, 0)),
            pl.BlockSpec((HEAD_DIM, 1), lambda b, h, i: (0, 0)),
        ],
        out_specs=pl.BlockSpec((ATT_W, HEAD_DIM), lambda b, h, i: (b * nq + i, h)),
        out_shape=jax.ShapeDtypeStruct((t, MIX_W), BF16),
        scratch_shapes=_attn_scratch(seq, 2 * ATT_W),
        compiler_params=_cparams(("parallel", "parallel", "arbitrary")),
        name="attn_a",
    )(z, z, z, lam_p, gain_col)


D_AUG = 2 * HEAD_DIM
BIAS_ROWS = 16


def _attn_d_kernel(q_ref, k_ref, v_ref, kx_ref, qx_ref, o_ref, kaug_ref, vt_ref, s0_ref, s1_ref, m_ref, l_ref,
                   acc_ref):
    _attn_init(v_ref, vt_ref, m_ref, l_ref, acc_ref)
    i = pl.program_id(2)

    @pl.when(i == 0)
    def _():
        def body(j, carry):
            off = pl.multiple_of(j * ATT_TK, ATT_TK)
            kaug_ref[pl.ds(off, ATT_TK), 0:HEAD_DIM] = k_ref[pl.ds(off, ATT_TK), :]
            kaug_ref[pl.ds(off, ATT_TK), HEAD_DIM:D_AUG] = kx_ref[0, pl.ds(off, ATT_TK), :]
            return carry
        lax.fori_loop(0, vt_ref.shape[0], body, 0)

    q = (q_ref[...].astype(F32) * (HEAD_DIM ** -0.5 * LOG2E)).T.astype(BF16)
    q_t = jnp.concatenate([q, qx_ref[0], jnp.zeros((D_AUG - HEAD_DIM - BIAS_ROWS, ATT_W), BF16)], axis=0)
    key = lax.broadcasted_iota(jnp.int32, (ATT_TK, ATT_W), 0)
    qry = lax.broadcasted_iota(jnp.int32, (ATT_TK, ATT_W), 1)
    _attn_pipeline(kaug_ref, vt_ref, q_t, s0_ref, s1_ref, m_ref, l_ref, acc_ref, i, i, key <= qry)
    o_ref[...] = (acc_ref[...] * (1.0 / l_ref[...])).T.astype(BF16)


def _attn_d(z, kx, qx, batch, seq):
    t = batch * seq
    nq = seq // ATT_W
    base = 4 * N_HEADS
    return pl.pallas_call(
        _attn_d_kernel,
        grid=(batch, N_HEADS, nq),
        in_specs=[
            pl.BlockSpec((ATT_W, HEAD_DIM), lambda b, h, i: (b * nq + i, base + h)),
            pl.BlockSpec((seq, HEAD_DIM), lambda b, h, i: (b, base + N_HEADS + h)),
            pl.BlockSpec((seq, HEAD_DIM), lambda b, h, i: (b, base + 2 * N_HEADS + h)),
            pl.BlockSpec((1, seq, HEAD_DIM), lambda b, h, i: (b * N_HEADS + h, 0, 0)),
            pl.BlockSpec((1, BIAS_ROWS, ATT_W), lambda b, h, i: (b * N_HEADS + h, 0, i)),
        ],
        out_specs=pl.BlockSpec((ATT_W, HEAD_DIM), lambda b, h, i: (b * nq + i, h)),
        out_shape=jax.ShapeDtypeStruct((t, MIX_W), BF16),
        scratch_shapes=[pltpu.VMEM((seq, D_AUG), BF16)] + _attn_scratch(seq, ATT_W),
        compiler_params=_cparams(("parallel", "parallel", "arbitrary")),
        name="attn_d",
    )(z, z, z, kx, qx)


def _forget_bias_operands(cum):
    c = cum * LOG2E
    hi_f = _bf16_part(c)
    mid_f = _bf16_part(c - hi_f)
    hi, mid, lo = hi_f.astype(BF16), mid_f.astype(BF16), (c - hi_f - mid_f).astype(BF16)
    one = jnp.ones_like(hi)
    kx = jnp.stack([hi, mid, lo, one, one, one], axis=-1)
    kx = jnp.pad(kx, ((0, 0), (0, 0), (0, HEAD_DIM - 6)))
    qx = jnp.stack([-one, -one, -one, hi, mid, lo], axis=1)
    qx = jnp.pad(qx, ((0, 0), (0, BIAS_ROWS - 6), (0, 0)))
    return kx, qx


def _logf_cumsum_kernel(x_ref, b_ref, o_ref, *, rows_per_seq):
    r = x_ref.shape[0]
    lf = _log_sigmoid(x_ref[...] + b_ref[...])
    a = lax.broadcasted_iota(jnp.int32, (LANES, LANES), 0)
    b = lax.broadcasted_iota(jnp.int32, (LANES, LANES), 1)
    within = _dot_exact(lf, (a <= b).astype(F32))
    totals = _dot_exact(lf, jnp.ones((LANES, LANES), F32))
    ra = lax.broadcasted_iota(jnp.int32, (r, r), 0)
    rb = lax.broadcasted_iota(jnp.int32, (r, r), 1)
    earlier = ((rb < ra) & ((rb // rows_per_seq) == (ra // rows_per_seq))).astype(F32)
    o_ref[...] = within + _dot_exact(earlier, totals)


def _logf_cumsum(x, bias_rows, rows_per_seq):
    r = x.shape[0]
    return pl.pallas_call(
        functools.partial(_logf_cumsum_kernel, rows_per_seq=rows_per_seq),
        out_shape=jax.ShapeDtypeStruct((r, LANES), F32),
        compiler_params=pltpu.CompilerParams(vmem_limit_bytes=VMEM_LIMIT),
        name="logf_cumsum",
    )(x, bias_rows)


CONV_TAIL = 8


def _causal_conv_silu(x, prev_tail, w):
    row = lax.broadcasted_iota(jnp.int32, prev_tail.shape, 0)
    acc = x * w[CONV_W - 1:CONV_W, :]
    for s in range(1, CONV_W):
        xs = pltpu.roll(x, s, axis=0)
        head = jnp.where(row >= s, xs[0:CONV_TAIL], pltpu.roll(prev_tail, s, axis=0))
        xs = jnp.concatenate([head, xs[CONV_TAIL:]], axis=0)
        acc = acc + xs * w[CONV_W - 1 - s:CONV_W - s, :]
    return _silu(acc)


def _chunk_masks():
    r = lax.broadcasted_iota(jnp.int32, (CHUNK, CHUNK), 0)
    c = lax.broadcasted_iota(jnp.int32, (CHUNK, CHUNK), 1)
    return r, c


REC_NB_MLSTM = 2
REC_NB_DELTANET = 4


def _rec_batch_block(batch, nb):
    return nb if batch % nb == 0 else 1


def _rec_specs(nb, cols):
    main = [pl.BlockSpec((nb, CHUNK, MIX_W), lambda b, c, j=j: (b, c, j)) for j in cols]
    return main + [pl.BlockSpec((nb, CHUNK, LANES), lambda b, c: (b, c, 0))]


def _mlstm_kernel(q_ref, k_ref, v_ref, og_ref, g_ref, conv_ref, bias_ref, norm_ref, out_ref,
                  pq_ref, pk_ref, c_ref, n_ref, m_ref):
    @pl.when(pl.program_id(1) == 0)
    def _():
        pq_ref[...] = jnp.zeros(pq_ref.shape, F32)
        pk_ref[...] = jnp.zeros(pk_ref.shape, F32)
        c_ref[...] = jnp.zeros(c_ref.shape, F32)
        n_ref[...] = jnp.zeros(n_ref.shape, F32)
        m_ref[...] = jnp.zeros(m_ref.shape, F32)

    lane = lax.broadcasted_iota(jnp.int32, (CHUNK, LANES), 1)
    r, c = _chunk_masks()
    causal = c <= r
    tri = causal.astype(F32)
    units = []
    for bi in range(q_ref.shape[0]):
        xq = q_ref[bi].astype(F32)
        xk = k_ref[bi].astype(F32)
        q_all = _causal_conv_silu(xq, pq_ref[bi], conv_ref[:, 0:MIX_W])
        k_all = _causal_conv_silu(xk, pk_ref[bi], conv_ref[:, MIX_W:2 * MIX_W]) * (HEAD_DIM ** -0.5)
        pq_ref[bi] = xq[CHUNK - CONV_TAIL:CHUNK]
        pk_ref[bi] = xk[CHUNK - CONV_TAIL:CHUNK]
        pre = g_ref[bi] + bias_ref[...]
        e = jnp.where(lane < N_HEADS, pre, _log_sigmoid(pre))
        cum = _dot_exact(tri, e)
        e_t = e.T
        cum_t = cum.T
        for h in range(N_HEADS):
            sl = slice(h * HEAD_DIM, (h + 1) * HEAD_DIM)
            u = dict(bi=bi, h=h, sl=sl, q=q_all[:, sl], k=k_all[:, sl], v=v_ref[bi, :, sl].astype(F32))
            u["qb"] = u["q"].astype(BF16)
            u["kb"] = u["k"].astype(BF16)
            i_col = e[:, h:h + 1]
            i_row = e_t[h:h + 1, :]
            b_col = cum[:, N_HEADS + h:N_HEADS + h + 1]
            b_row = cum_t[N_HEADS + h:N_HEADS + h + 1, :]
            b_last = b_col[CHUNK - 1:CHUNK, :]
            m_st = m_ref[bi, h:h + 1, 0:1]
            u["c_st"] = c_ref[bi, h]
            u["n_st"] = n_ref[bi, h:h + 1, :]
            d_log = jnp.where(causal, b_col - b_row + i_row, NEG_INF)
            inter_log = b_col + m_st
            u["m_t"] = jnp.maximum(inter_log, jnp.max(d_log, axis=-1, keepdims=True))
            u["d_w"] = jnp.exp(d_log - u["m_t"])
            u["inter_w"] = jnp.exp(inter_log - u["m_t"])
            w_log = b_last - b_col + i_col
            u["m_new"] = jnp.maximum(b_last + m_st, jnp.max(w_log, axis=0, keepdims=True))
            u["sw"] = jnp.exp(w_log - u["m_new"])
            u["decay"] = jnp.exp(b_last + m_st - u["m_new"])
            units.append(u)

    for u in units:
        u["qk"] = _dot_nt(u["qb"], u["kb"])
    for u in units:
        u["qc"] = _dot_nt(u["qb"], u["c_st"].astype(BF16))
    for u in units:
        u["upd"] = _dot((u["v"] * u["sw"]).T.astype(BF16), u["kb"])
    for u in units:
        u["a"] = u["qk"] * u["d_w"]
        u["av"] = _dot(u["a"].astype(BF16), u["v"].astype(BF16))
    for u in units:
        bi, h, sl = u["bi"], u["h"], u["sl"]
        num = u["av"] + u["inter_w"] * u["qc"]
        den = (jnp.sum(u["a"], axis=-1, keepdims=True)
               + u["inter_w"] * jnp.sum(u["q"] * u["n_st"], axis=-1, keepdims=True))
        hh = num / jnp.maximum(jnp.abs(den), jnp.exp(-u["m_t"]))
        c_ref[bi, h] = u["decay"] * u["c_st"] + u["upd"]
        n_ref[bi, h:h + 1, :] = u["decay"] * u["n_st"] + jnp.sum(u["k"] * u["sw"], axis=0, keepdims=True)
        m_ref[bi, h:h + 1, :] = jnp.broadcast_to(u["m_new"], (1, LANES))
        gate = _sigmoid(og_ref[bi, :, sl].astype(F32))
        out_ref[bi, :, sl] = (_rms(hh, norm_ref[...]) * gate).astype(BF16)


def _mlstm(z, zg, conv_w, gate_bias, norm, batch, seq):
    nb = _rec_batch_block(batch, REC_NB_MLSTM)
    full = lambda s: pl.BlockSpec(s, lambda b, c: (0, 0))
    out = pl.pallas_call(
        _mlstm_kernel,
        grid=(batch // nb, seq // CHUNK),
        in_specs=_rec_specs(nb, (3, 4, 5, 6)) + [full((CONV_W, 2 * MIX_W)), full((1, LANES)), full((1, HEAD_DIM))],
        out_specs=pl.BlockSpec((nb, CHUNK, MIX_W), lambda b, c: (b, c, 0)),
        out_shape=jax.ShapeDtypeStruct((batch, seq, MIX_W), BF16),
        scratch_shapes=[pltpu.VMEM((nb, CONV_TAIL, MIX_W), F32), pltpu.VMEM((nb, CONV_TAIL, MIX_W), F32),
                        pltpu.VMEM((nb, N_HEADS, HEAD_DIM, HEAD_DIM), F32), pltpu.VMEM((nb, 8, HEAD_DIM), F32),
                        pltpu.VMEM((nb, 8, LANES), F32)],
        compiler_params=_cparams(("parallel", "arbitrary")),
        name="mlstm",
    )(*([z.reshape(batch, seq, Z_MAIN)] * 4), zg.reshape(batch, seq, LANES), conv_w, gate_bias, norm)
    return out.reshape(batch * seq, MIX_W)


def _bdot(a, b):
    return _dot(a.astype(BF16), b.astype(BF16))


def _unit_lower_inverses(l_stricts, r, c):
    same16 = (r >> 4) == (c >> 4)
    same32 = (r >> 5) == (c >> 5)
    eye = (r == c).astype(F32)
    ps = [jnp.where(same16, -l, 0.0) for l in l_stricts]
    invs = [eye + p for p in ps]
    for _ in range(3):
        ps = [_bdot(p, p) for p in ps]
        invs = [inv + _bdot(inv, p) for inv, p in zip(invs, ps)]
    for off_mask in (same32 & jnp.logical_not(same16), jnp.logical_not(same32)):
        ts = [_bdot(jnp.where(off_mask, l, 0.0), inv) for l, inv in zip(l_stricts, invs)]
        invs = [inv - _bdot(inv, t) for inv, t in zip(invs, ts)]
    return invs


def _deltanet_kernel(q_ref, k_ref, v_ref, og_ref, g_ref, conv_ref, par_ref, norm_ref, out_ref,
                     pq_ref, pk_ref, pv_ref, s_ref):
    @pl.when(pl.program_id(1) == 0)
    def _():
        pq_ref[...] = jnp.zeros(pq_ref.shape, F32)
        pk_ref[...] = jnp.zeros(pk_ref.shape, F32)
        pv_ref[...] = jnp.zeros(pv_ref.shape, F32)
        s_ref[...] = jnp.zeros(s_ref.shape, F32)

    lane = lax.broadcasted_iota(jnp.int32, (CHUNK, LANES), 1)
    r, c = _chunk_masks()
    tri = (c <= r).astype(F32)
    units = []
    for bi in range(q_ref.shape[0]):
        xq = q_ref[bi].astype(F32)
        xk = k_ref[bi].astype(F32)
        xv = v_ref[bi].astype(F32)
        q_all = _causal_conv_silu(xq, pq_ref[bi], conv_ref[:, 0:MIX_W])
        k_all = _causal_conv_silu(xk, pk_ref[bi], conv_ref[:, MIX_W:2 * MIX_W])
        v_all = _causal_conv_silu(xv, pv_ref[bi], conv_ref[:, 2 * MIX_W:3 * MIX_W])
        pq_ref[bi] = xq[CHUNK - CONV_TAIL:CHUNK]
        pk_ref[bi] = xk[CHUNK - CONV_TAIL:CHUNK]
        pv_ref[bi] = xv[CHUNK - CONV_TAIL:CHUNK]
        pre = g_ref[bi]
        e = jnp.where(lane < N_HEADS, -jnp.exp(par_ref[0:1, :]) * _softplus(pre + par_ref[1:2, :]), _sigmoid(pre))
        cum = _dot_exact(tri, e)
        cum_t = cum.T
        for h in range(N_HEADS):
            sl = slice(h * HEAD_DIM, (h + 1) * HEAD_DIM)
            q = q_all[:, sl]
            k = k_all[:, sl]
            q = q * lax.rsqrt(jnp.sum(q * q, axis=-1, keepdims=True) + RMS_EPS) * (HEAD_DIM ** -0.5)
            k = k * lax.rsqrt(jnp.sum(k * k, axis=-1, keepdims=True) + RMS_EPS)
            beta = e[:, N_HEADS + h:N_HEADS + h + 1]
            b_col = cum[:, h:h + 1]
            b_row = cum_t[h:h + 1, :]
            b_last = b_col[CHUNK - 1:CHUNK, :]
            e_col = jnp.exp(b_col)
            kbeta = k * beta
            units.append(dict(
                bi=bi, h=h, sl=sl, k=k, kbeta=kbeta, q_dec=q * e_col,
                gam=jnp.exp(jnp.where(c <= r, b_col - b_row, NEG_INF)),
                rhs=jnp.concatenate([v_all[:, sl] * beta, kbeta * e_col], axis=1),
                k_dec=k * jnp.exp(b_last - b_col), chunk_decay=jnp.exp(b_last),
                kq=jnp.concatenate([kbeta, q], axis=0)))

    for u in units:
        both = _dot_nt(u["kq"].astype(BF16), u["k"].astype(BF16)) * jnp.concatenate([u["gam"], u["gam"]], axis=0)
        u["l"] = jnp.where(c < r, both[0:CHUNK], 0.0)
        u["qk"] = both[CHUNK:2 * CHUNK]
    invs = _unit_lower_inverses([u["l"] for u in units], r, c)
    for u, inv in zip(units, invs):
        sol = _bdot(inv, u["rhs"])
        u["u"] = sol[:, 0:HEAD_DIM]
        u["w"] = sol[:, HEAD_DIM:2 * HEAD_DIM]
    for u in units:
        u["st"] = s_ref[u["bi"], u["h"]]
        u["ws"] = _bdot(jnp.concatenate([u["w"], u["q_dec"]], axis=0), u["st"])
    for u in units:
        u["v_new"] = u["u"] - u["ws"][0:CHUNK]
        u["o"] = u["ws"][CHUNK:2 * CHUNK] + _bdot(u["qk"], u["v_new"])
    for u in units:
        s_ref[u["bi"], u["h"]] = u["chunk_decay"] * u["st"] + _bdot(u["k_dec"].T, u["v_new"])
    for u in units:
        bi, sl = u["bi"], u["sl"]
        gate = _silu(og_ref[bi, :, sl].astype(F32))
        out_ref[bi, :, sl] = (_rms(u["o"], norm_ref[...]) * gate).astype(BF16)


def _deltanet(z, zg, conv_w, params, norm, batch, seq):
    nb = _rec_batch_block(batch, REC_NB_DELTANET)
    full = lambda s: pl.BlockSpec(s, lambda b, c: (0, 0))
    out = pl.pallas_call(
        _deltanet_kernel,
        grid=(batch // nb, seq // CHUNK),
        in_specs=_rec_specs(nb, (0, 1, 2, 3)) + [full((CONV_W, 3 * MIX_W)), full((2, LANES)), full((1, HEAD_DIM))],
        out_specs=pl.BlockSpec((nb, CHUNK, MIX_W), lambda b, c: (b, c, 0)),
        out_shape=jax.ShapeDtypeStruct((batch, seq, MIX_W), BF16),
        scratch_shapes=[pltpu.VMEM((nb, CONV_TAIL, MIX_W), F32)] * 3
        + [pltpu.VMEM((nb, N_HEADS, HEAD_DIM, HEAD_DIM), F32)],
        compiler_params=_cparams(("parallel", "arbitrary")),
        name="deltanet",
    )(*([z.reshape(batch, seq, Z_MAIN)] * 4), zg.reshape(batch, seq, LANES), conv_w, params, norm)
    return out.reshape(batch * seq, MIX_W)


OUT_TM = 512
G_LANE0 = N_EXPERTS
ROUTE_W_LANE = 4


def _outproj_router_kernel(h_ref, a_ref, b_ref, wo_ref, gn_ref, whi_ref, wlo_ref, br_ref,
                           hout_ref, xn_ref, comb_ref):
    h_new = h_ref[...] + _dot(a_ref[...], wo_ref[0:MIX_W, :]) + _dot(b_ref[...], wo_ref[MIX_W:2 * MIX_W, :])
    hout_ref[...] = h_new
    hn = _rms(h_new, gn_ref[...])
    xn_ref[...] = hn
    hi_f = _bf16_part(hn)
    hi = hi_f.astype(BF16)
    lo = (hn - hi_f).astype(BF16)
    logits = (_dot(hi, whi_ref[...]) + _dot(lo, whi_ref[...]) + _dot(hi, wlo_ref[...]) + _dot(lo, wlo_ref[...])
              + br_ref[...])

    lane_i = lax.broadcasted_iota(jnp.int32, logits.shape, 1)
    lane = lane_i.astype(F32)
    big = 1e9
    is_group = (lane_i >= G_LANE0) & (lane_i < G_LANE0 + N_GROUPS)
    gl = jnp.where(is_group, logits, NEG_INF)
    gmax = jnp.max(gl, axis=-1, keepdims=True)
    g_sel = jnp.min(jnp.where(gl == gmax, lane, big), axis=-1, keepdims=True) - G_LANE0
    p_top = 1.0 / jnp.sum(jnp.exp(gl - gmax), axis=-1, keepdims=True)
    group_of_lane = (lane_i >> 2).astype(F32)
    in_group = (lane_i < N_EXPERTS) & (group_of_lane == g_sel)
    el = jnp.where(in_group, logits, NEG_INF)
    v1 = jnp.max(el, axis=-1, keepdims=True)
    i1 = jnp.min(jnp.where(el == v1, lane, big), axis=-1, keepdims=True)
    el2 = jnp.where(lane == i1, NEG_INF, el)
    v2 = jnp.max(el2, axis=-1, keepdims=True)
    i2 = jnp.min(jnp.where(el2 == v2, lane, big), axis=-1, keepdims=True)
    e21 = jnp.exp(v2 - v1)
    w1 = p_top / (1.0 + e21)
    local = lane - ROUTE_W_LANE + EXPERTS_PER_GROUP * g_sel
    comb_ref[...] = (jnp.where(lane_i == 0, g_sel, 0.0) + jnp.where(local == i1, w1, 0.0)
                     + jnp.where(local == i2, w1 * e21, 0.0))


def _outproj_router(h, oa, ob, w_out, gain, wr_hi, wr_lo, b_r):
    t = h.shape[0]
    row = lambda w: pl.BlockSpec((OUT_TM, w), lambda i: (i, 0))
    full = lambda s: pl.BlockSpec(s, lambda i: (0, 0))
    return pl.pallas_call(
        _outproj_router_kernel,
        grid=(t // OUT_TM,),
        in_specs=[row(D_MODEL), row(MIX_W), row(MIX_W), full((D_MODEL, D_MODEL)), full((1, D_MODEL)),
                  full((D_MODEL, LANES)), full((D_MODEL, LANES)), full((1, LANES))],
        out_specs=[row(D_MODEL), row(D_MODEL), row(LANES)],
        out_shape=[jax.ShapeDtypeStruct((t, D_MODEL), F32), jax.ShapeDtypeStruct((t, D_MODEL), F32),
                   jax.ShapeDtypeStruct((t, LANES), F32)],
        compiler_params=_cparams(("parallel",)),
        name="outproj_router",
    )(h, oa, ob, w_out, gain, wr_hi, wr_lo, b_r)


MOE_TM = 512


def _moe_plan(route, t):
    n_rows = t + (N_GROUPS + 1) * MOE_TM
    n_tiles = n_rows // MOE_TM
    group = route[:, 0].astype(jnp.int32)
    onehot = (group[:, None] == jnp.arange(N_GROUPS, dtype=jnp.int32)[None, :]).astype(jnp.int32)
    csum = jnp.cumsum(onehot, axis=0)
    rank = jnp.sum(csum * onehot, axis=1) - 1
    padded = ((csum[-1] + MOE_TM - 1) // MOE_TM) * MOE_TM
    ends = jnp.cumsum(padded)
    pos = jnp.sum((ends - padded)[None, :] * onehot, axis=1) + rank
    record = jnp.concatenate([jnp.arange(1, t + 1, dtype=F32)[:, None],
                              route[:, ROUTE_W_LANE:ROUTE_W_LANE + EXPERTS_PER_GROUP]], axis=1)
    record = jnp.zeros((n_rows, 1 + EXPERTS_PER_GROUP), F32).at[pos].set(record, unique_indices=True)
    real = record[:, 0].astype(jnp.int32) - 1
    is_pad = real < 0
    dst = jnp.where(is_pad, t + jnp.cumsum(is_pad.astype(jnp.int32)) - 1, real)
    w_sorted = record[:, 1:]
    starts = jnp.arange(n_tiles, dtype=jnp.int32) * MOE_TM
    tile_group = jnp.minimum(jnp.sum((starts[:, None] >= ends[None, :]).astype(jnp.int32), axis=1), N_GROUPS - 1)
    return tile_group, dst.reshape(n_tiles, MOE_TM), w_sorted


def _moe_kernel(tg_ref, dst_ref, x_hbm, ws_ref, wg_ref, wu_ref, wd_ref, out_hbm,
                xbuf, ybuf, sem_g, sem_s, *, n_tok):
    t = pl.program_id(0)
    nt = pl.num_programs(0)
    slot = t % 2

    def row_gather(tile, buf, r):
        d = dst_ref[tile, r]
        tok = jnp.where(d < n_tok, d, 0)
        return pltpu.make_async_copy(x_hbm.at[pl.ds(tok, 1)], xbuf.at[buf, pl.ds(r, 1)], sem_g.at[buf])

    def row_scatter(tile, buf, r):
        return pltpu.make_async_copy(ybuf.at[buf, pl.ds(r, 1)], out_hbm.at[pl.ds(dst_ref[tile, r], 1)], sem_s.at[buf])

    def wait_gather(buf):
        pltpu.make_async_copy(x_hbm.at[pl.ds(0, MOE_TM)], xbuf.at[buf], sem_g.at[buf]).wait()

    def wait_scatter(buf):
        pltpu.make_async_copy(ybuf.at[buf], out_hbm.at[pl.ds(0, MOE_TM)], sem_s.at[buf]).wait()

    def start_gather(tile, buf):
        def body(i, carry):
            for u in range(8):
                row_gather(tile, buf, i * 8 + u).start(priority=u % 2)
            return carry
        lax.fori_loop(0, MOE_TM // 8, body, 0)

    nxt = jnp.where(t == nt - 1, 0, t + 1)
    prv = jnp.where(t == 0, nt - 1, t - 1)
    other = 1 - slot

    @pl.when(t == 0)
    def _():
        start_gather(0, 0)
        ybuf[1] = jnp.zeros(ybuf.shape[1:], F32)

    start_gather(nxt, other)
    wait_gather(slot)

    x = xbuf[slot].astype(BF16)
    ws = ws_ref[...]
    y = jnp.zeros((MOE_TM, D_MODEL), F32)
    for r in range(MOE_TM):
        row_scatter(prv, other, r).start(priority=r % 2)
    for e in range(EXPERTS_PER_GROUP):
        he = _silu(_dot(x, wg_ref[e])) * _dot(x, wu_ref[e])
        y = y + _dot((he * ws[:, e:e + 1]).astype(BF16), wd_ref[e])

    @pl.when(t > 0)
    def _():
        wait_scatter(slot)

    ybuf[slot] = y

    @pl.when(t == nt - 1)
    def _():
        wait_scatter(other)
        wait_gather(other)


def _moe(xn, route, wg, wu, wd):
    t = xn.shape[0]
    tile_group, dst, w_sorted = _moe_plan(route, t)
    n_tiles = dst.shape[0]
    buf = pltpu.VMEM((2, MOE_TM, D_MODEL), F32)
    grid_spec = pltpu.PrefetchScalarGridSpec(
        num_scalar_prefetch=2,
        grid=(n_tiles,),
        in_specs=[
            pl.BlockSpec(memory_space=pl.ANY),
            pl.BlockSpec((MOE_TM, EXPERTS_PER_GROUP), lambda i, tg, dst: (i, 0)),
            pl.BlockSpec((EXPERTS_PER_GROUP, D_MODEL, EXPERT_FF), lambda i, tg, dst: (tg[i], 0, 0)),
            pl.BlockSpec((EXPERTS_PER_GROUP, D_MODEL, EXPERT_FF), lambda i, tg, dst: (tg[i], 0, 0)),
            pl.BlockSpec((EXPERTS_PER_GROUP, EXPERT_FF, D_MODEL), lambda i, tg, dst: (tg[i], 0, 0)),
        ],
        out_specs=pl.BlockSpec(memory_space=pl.ANY),
        scratch_shapes=[buf, buf, pltpu.SemaphoreType.DMA((2,)), pltpu.SemaphoreType.DMA((2,))],
    )
    return pl.pallas_call(
        functools.partial(_moe_kernel, n_tok=t),
        grid_spec=grid_spec,
        out_shape=jax.ShapeDtypeStruct((dst.size, D_MODEL), F32),
        compiler_params=_cparams(("arbitrary",)),
        name="moe",
    )(tile_group, dst, xn, w_sorted, wg, wu, wd)


PLE_TM = 512


def _ple_kernel(h_ref, y_ref, p_ref, wg_ref, wp_ref, gn_ref, out_ref, *, final_norm):
    h = h_ref[...] + y_ref[...]
    gate = _sigmoid(_dot(h.astype(BF16), wg_ref[...]))
    out = h + gate * _dot(p_ref[...].astype(BF16), wp_ref[...])
    if final_norm:
        out = _rms(out, gn_ref[...])
    out_ref[...] = out


def _ple(h, y, p, w_gate, w_proj, gain, final_norm):
    t = h.shape[0]
    return pl.pallas_call(
        functools.partial(_ple_kernel, final_norm=final_norm),
        grid=(t // PLE_TM,),
        in_specs=[
            pl.BlockSpec((PLE_TM, D_MODEL), lambda i: (i, 0)),
            pl.BlockSpec((PLE_TM, D_MODEL), lambda i: (i, 0)),
            pl.BlockSpec((PLE_TM, PLE_DIM), lambda i: (i, 0)),
            pl.BlockSpec((D_MODEL, D_MODEL), lambda i: (0, 0)),
            pl.BlockSpec((PLE_DIM, D_MODEL), lambda i: (0, 0)),
            pl.BlockSpec((1, D_MODEL), lambda i: (0, 0)),
        ],
        out_specs=pl.BlockSpec((PLE_TM, D_MODEL), lambda i: (i, 0)),
        out_shape=jax.ShapeDtypeStruct((t, D_MODEL), F32),
        compiler_params=_cparams(("parallel",)),
        name="ple",
    )(h, y, p, w_gate, w_proj, gain)


def _pad_lanes(x):
    return jnp.pad(x, ((0, 0), (0, LANES - x.shape[-1])))


def _row(x):
    return x.reshape(1, -1).astype(F32)


def kernel(x, p, norm_mix, norm_ffn, norm_final, ab_w_in, ab_w_out, a_lam_q1, a_lam_k1, a_lam_q2, a_lam_k2,
           a_subln, b_conv, b_igate_bias, b_fgate_bias, b_norm, cd_w_in, cd_w_out, c_conv, c_a_log, c_dt_bias,
           c_norm, d_fgate_bias, moe_w_group, moe_b_group, moe_w_router, moe_b_router, moe_w_gate, moe_w_up,
           moe_w_down, ple_w_gate, ple_w_proj):
    batch, seq, _ = x.shape
    depth = p.shape[0]
    t = batch * seq
    assert seq % IN_TM == 0 and seq % ATT_TK == 0 and t % MOE_TM == 0
    h = x.reshape(t, D_MODEL)
    cos, sin = _rope_tables(seq)
    ones_tab = jnp.ones((seq, LANES), F32)

    for i in range(depth):
        j = i // 2
        if i % 2 == 0:
            w_in = ab_w_in[j]
            z, zg = _inproj(h, _row(norm_mix[i]), w_in[:, :Z_MAIN].astype(BF16),
                            _pad_lanes(w_in[:, Z_MAIN:]).astype(BF16), cos, sin, seq, rope_chunks=2)
            lam_init = 0.8 - 0.6 * math.exp(-0.3 * i)
            lam_p = jnp.stack([a_lam_q1[j], a_lam_k1[j], a_lam_q2[j], a_lam_k2[j]]).astype(F32)
            out_1 = _attn_a(z, lam_p, a_subln[j].reshape(HEAD_DIM, 1).astype(F32), batch, seq, lam_init)
            gate_bias = _pad_lanes(jnp.concatenate([b_igate_bias[j], b_fgate_bias[j]]).reshape(1, -1))
            out_2 = _mlstm(z, zg, b_conv[j], gate_bias, _row(b_norm[j]), batch, seq)
            w_out = ab_w_out[j]
        else:
            w_in = cd_w_in[j]
            c_main, d_main = 4 * MIX_W, 3 * MIX_W
            w_main = jnp.concatenate([w_in[:, :c_main], w_in[:, c_main + 8:c_main + 8 + d_main]], axis=1)
            w_small = jnp.concatenate([w_in[:, c_main:c_main + 8], w_in[:, c_main + 8 + d_main:]], axis=1)
            z, zg = _inproj(h, _row(norm_mix[i]), w_main.astype(BF16), _pad_lanes(w_small).astype(BF16),
                            ones_tab, ones_tab, seq, rope_chunks=0)
            params = jnp.stack([_pad_lanes(c_a_log[j].reshape(1, -1))[0], _pad_lanes(c_dt_bias[j].reshape(1, -1))[0]])
            out_1 = _deltanet(z, zg, c_conv[j], params, _row(c_norm[j]), batch, seq)
            fd = zg[:, 8:8 + N_HEADS].reshape(batch, seq, N_HEADS).transpose(0, 2, 1)
            bias_rows = jnp.repeat(jnp.tile(d_fgate_bias[j], batch), seq // LANES).reshape(-1, 1)
            cum = _logf_cumsum(fd.reshape(-1, LANES), jnp.broadcast_to(bias_rows, (bias_rows.shape[0], LANES)),
                               seq // LANES)
            kx, qx = _forget_bias_operands(cum.reshape(batch * N_HEADS, seq))
            out_2 = _attn_d(z, kx, qx, batch, seq)
            w_out = cd_w_out[j]

        w_r = _pad_lanes(jnp.concatenate([moe_w_router[i], moe_w_group[i]], axis=1))
        w_r_hi_f = _bf16_part(w_r)
        w_r_hi = w_r_hi_f.astype(BF16)
        w_r_lo = (w_r - w_r_hi_f).astype(BF16)
        b_r = _pad_lanes(jnp.concatenate([moe_b_router[i], moe_b_group[i]]).reshape(1, -1))
        h, xn, comb = _outproj_router(h, out_1, out_2, w_out.astype(BF16), _row(norm_ffn[i]), w_r_hi, w_r_lo, b_r)
        y = _moe(xn, comb, moe_w_gate[i].astype(BF16), moe_w_up[i].astype(BF16), moe_w_down[i].astype(BF16))
        h = _ple(h, y, p[i].reshape(t, PLE_DIM), ple_w_gate[i].astype(BF16), ple_w_proj[i].astype(BF16),
                 _row(norm_final), final_norm=(i == depth - 1))
    return h.reshape(batch, seq, D_MODEL)
```
